```python
import math
import jax
import jax.numpy as jnp
from jax import lax
import numpy as np

D_MODEL = 1024
BATCH = 4
SEQ = 4096
DEPTH = 4

S5_W = D_MODEL // 2
S5_GROUP = 16
S5_GROUPS = S5_W // S5_GROUP
S5_STATE = 64
GLA_HEADS = 4
GLA_DV = (D_MODEL // 2) // GLA_HEADS
GLA_DK = GLA_DV // 2
GLA_QK = GLA_HEADS * GLA_DK
GLA_V = GLA_HEADS * GLA_DV
GLA_RANK = 16
GLA_GATE_NORM = 16.0
AB_IN = S5_W + 2 * GLA_QK + 2 * GLA_V + GLA_RANK
AB_OUT = S5_W + GLA_V
ML_HEADS = 8
ML_W = D_MODEL
ML_DH = ML_W // ML_HEADS
ML_IN = 4 * ML_W + 2 * ML_HEADS
CONV_W = 4
CHUNK = 64
N_EXPERTS = 16
N_EXPERT_GROUPS = 4
EXPERTS_PER_GROUP = N_EXPERTS // N_EXPERT_GROUPS
TOP_K = 2
D_EXPERT = 256

kernel_name = "hybrid_s5_gla_mlstm_moe_deepnorm_adaln"

F32 = jnp.float32


def layer_norm(x, g, b, eps=1e-5):
    xf = x.astype(F32)
    mu = jnp.mean(xf, -1, keepdims=True)
    var = jnp.mean(jnp.square(xf - mu), -1, keepdims=True)
    return ((xf - mu) * lax.rsqrt(var + eps) * g.astype(F32) + b.astype(F32)).astype(x.dtype)


def head_rmsnorm(h, g, eps=1e-6):
    h = h * lax.rsqrt(jnp.mean(h * h, -1, keepdims=True) + eps)
    return h.reshape(h.shape[0], h.shape[1], -1) * g.astype(F32)


def to_chunks(t):
    bsz, L, H = t.shape[:3]
    t = t.reshape(bsz, L // CHUNK, CHUNK, H, *t.shape[3:])
    return jnp.moveaxis(t, (1, 2), (0, 3))


def from_chunks(t):
    t = jnp.moveaxis(t, (0, 3), (1, 2))
    return t.reshape(t.shape[0], t.shape[1] * t.shape[2], t.shape[3], *t.shape[4:])


def causal_mask():
    return jnp.tril(jnp.ones((CHUNK, CHUNK), dtype=bool))


def s5_combine(e1, e2):
    a1r, a1i, b1r, b1i = e1
    a2r, a2i, b2r, b2i = e2
    return (a2r * a1r - a2i * a1i,
            a2r * a1i + a2i * a1r,
            a2r * b1r - a2i * b1i + b2r,
            a2r * b1i + a2i * b1r + b2i)


def s5_mixer(u, lam_re, lam_im, log_dt, b_re, b_im, c_re, c_im, d_skip, glu_w, glu_b):
    bsz, L, _ = u.shape
    uf = u.astype(F32)
    lre, lim = lam_re.astype(F32), lam_im.astype(F32)
    dt = jnp.exp(log_dt.astype(F32))[:, None]
    mag = jnp.exp(lre * dt)
    ang = lim * dt
    abar_re, abar_im = mag * jnp.cos(ang), mag * jnp.sin(ang)
    den = lre * lre + lim * lim
    nre, nim = abar_re - 1.0, abar_im
    coef_re = (nre * lre + nim * lim) / den
    coef_im = (nim * lre - nre * lim) / den
    bre, bim = b_re.astype(F32), b_im.astype(F32)
    bb_re = coef_re[..., None] * bre - coef_im[..., None] * bim
    bb_im = coef_re[..., None] * bim + coef_im[..., None] * bre
    ut = jnp.swapaxes(uf.reshape(bsz, L, S5_GROUPS, S5_GROUP), 0, 1)
    bu_re = jnp.einsum('lbgc,gpc->lbgp', ut, bb_re)
    bu_im = jnp.einsum('lbgc,gpc->lbgp', ut, bb_im)
    a_re = jnp.broadcast_to(abar_re[None, None], (L, 1, S5_GROUPS, S5_STATE))
    a_im = jnp.broadcast_to(abar_im[None, None], (L, 1, S5_GROUPS, S5_STATE))
    _, _, s_re, s_im = lax.associative_scan(s5_combine, (a_re, a_im, bu_re, bu_im), axis=0)
    y = (jnp.einsum('lbgp,gcp->lbgc', s_re, c_re.astype(F32))
         - jnp.einsum('lbgp,gcp->lbgc', s_im, c_im.astype(F32)))
    y = jnp.swapaxes(y, 0, 1).reshape(bsz, L, S5_W) + d_skip.astype(F32) * uf
    z = jax.nn.gelu(y)
    z = z * jax.nn.sigmoid(z @ glu_w.astype(F32) + glu_b.astype(F32))
    return z


def gla_chunked(q, k, v, log_a):
    bsz = q.shape[0]
    q = q * (GLA_DK ** -0.5)
    mask = causal_mask()[:, :, None]

    def step(S, inp):
        qc, kc, vc, gc = inp
        bc = jnp.cumsum(gc, axis=-2)
        inter = jnp.einsum('bhtk,bhkv->bhtv', qc * jnp.exp(bc), S)
        diff = bc[:, :, :, None, :] - bc[:, :, None, :, :]
        decay = jnp.exp(jnp.where(mask, diff, -jnp.inf))
        att = jnp.einsum('bhtsk,bhsk->bhts', qc[:, :, :, None, :] * decay, kc)
        intra = jnp.einsum('bhts,bhsv->bhtv', att, vc)
        blast = bc[:, :, -1:, :]
        S_new = (jnp.exp(blast[:, :, 0, :])[..., None] * S
                 + jnp.einsum('bhsk,bhsv->bhkv', kc * jnp.exp(blast - bc), vc))
        return S_new, inter + intra

    S0 = jnp.zeros((bsz, GLA_HEADS, GLA_DK, GLA_DV), F32)
    _, o = lax.scan(step, S0, (to_chunks(q), to_chunks(k), to_chunks(v), to_chunks(log_a)))
    return from_chunks(o)


def ab_mixer(h, w_in, lam_re, lam_im, log_dt, b_re, b_im, c_re, c_im, d_skip,
             glu_w, glu_b, gate_w, gate_b, norm_g, w_out):
    bsz, L, _ = h.shape
    p = h @ w_in
    o1 = S5_W
    o2 = o1 + GLA_QK
    o3 = o2 + GLA_QK
    o4 = o3 + GLA_V
    o5 = o4 + GLA_V
    u, q, k, v, r, a_lr = jnp.split(p, [o1, o2, o3, o4, o5], axis=-1)
    y_a = s5_mixer(u, lam_re, lam_im, log_dt, b_re, b_im, c_re, c_im, d_skip, glu_w, glu_b)
    log_a = jax.nn.log_sigmoid((a_lr @ gate_w + gate_b).astype(F32)) / GLA_GATE_NORM
    hd = lambda t, d: t.astype(F32).reshape(bsz, L, GLA_HEADS, d)
    o = gla_chunked(hd(q, GLA_DK), hd(k, GLA_DK), hd(v, GLA_DV), hd(log_a, GLA_DK))
    y_b = head_rmsnorm(o, norm_g) * jax.nn.silu(r.astype(F32))
    y = jnp.concatenate([y_a, y_b], axis=-1).astype(h.dtype)
    return y @ w_out


def mlstm_chunked(q, k, v, i_pre, f_pre):
    bsz = q.shape[0]
    k = k * (ML_DH ** -0.5)
    log_f = jax.nn.log_sigmoid(f_pre)
    mask = causal_mask()

    def step(carry, inp):
        Cs, ns, m = carry
        qc, kc, vc, ic, fc = inp
        b = jnp.cumsum(fc, axis=-1)
        dmat = jnp.where(mask, b[..., :, None] - b[..., None, :] + ic[..., None, :], -jnp.inf)
        inter_log = b + m[..., None]
        m_t = jnp.maximum(inter_log, jnp.max(dmat, axis=-1))
        w_intra = jnp.exp(dmat - m_t[..., None])
        w_inter = jnp.exp(inter_log - m_t)
        s = jnp.einsum('bhtd,bhsd->bhts', qc, kc) * w_intra
        num = (w_inter[..., None] * jnp.einsum('bhvd,bhtd->bhtv', Cs, qc)
               + jnp.einsum('bhts,bhsv->bhtv', s, vc))
        den = w_inter * jnp.einsum('bhd,bhtd->bht', ns, qc) + jnp.sum(s, axis=-1)
        hc = num / jnp.maximum(jnp.abs(den), jnp.exp(-m_t))[..., None]
        g_s = b[..., -1:] - b + ic
        m_new = jnp.maximum(b[..., -1] + m, jnp.max(g_s, axis=-1))
        w_s = jnp.exp(g_s - m_new[..., None])
        dec = jnp.exp(b[..., -1] + m - m_new)
        C_new = dec[..., None, None] * Cs + jnp.einsum('bhsv,bhsd->bhvd', vc * w_s[..., None], kc)
        n_new = dec[..., None] * ns + jnp.einsum('bhs,bhsd->bhd', w_s, kc)
        return (C_new, n_new, m_new), hc

    init = (jnp.zeros((bsz, ML_HEADS, ML_DH, ML_DH), F32),
            jnp.zeros((bsz, ML_HEADS, ML_DH), F32),
            jnp.zeros((bsz, ML_HEADS), F32))
    _, hs = lax.scan(step, init, (to_chunks(q), to_chunks(k), to_chunks(v),
                                  to_chunks(i_pre), to_chunks(log_f)))
    return from_chunks(hs)


def causal_depthwise_conv(t, w, b):
    ch = t.shape[-1]
    y = lax.conv_general_dilated(t, w[:, None, :], window_strides=(1,), padding=[(CONV_W - 1, 0)],
                                 dimension_numbers=('NWC', 'WIO', 'NWC'), feature_group_count=ch)
    return y + b


def mlstm_mixer(h, w_in, conv_w, conv_b, igate_b, fgate_b, norm_g, w_out):
    bsz, L, _ = h.shape
    p = h @ w_in
    qk, v, o, ig, fg = jnp.split(p, [2 * ML_W, 3 * ML_W, 4 * ML_W, 4 * ML_W + ML_HEADS], axis=-1)
    qk = jax.nn.silu(causal_depthwise_conv(qk, conv_w, conv_b))
    q, k = jnp.split(qk, 2, axis=-1)
    hd = lambda t: t.astype(F32).reshape(bsz, L, ML_HEADS, ML_DH)
    i_pre = (ig + igate_b).astype(F32)
    f_pre = (fg + fgate_b).astype(F32)
    hh = mlstm_chunked(hd(q), hd(k), hd(v), i_pre, f_pre)
    y = jax.nn.sigmoid(o.astype(F32)) * head_rmsnorm(hh, norm_g)
    return y.astype(h.dtype) @ w_out


def grouped_moe(h, router_w, router_bias, w_gate, w_up, w_down):
    bsz, L, d = h.shape
    t = h.reshape(-1, d)
    T = t.shape[0]
    aff = jax.nn.sigmoid(t.astype(F32) @ router_w.astype(F32))
    sel = (aff + router_bias.astype(F32)).reshape(T, N_EXPERT_GROUPS, EXPERTS_PER_GROUP)
    grp_score = jnp.sum(lax.top_k(sel, TOP_K)[0], axis=-1)
    g_idx = jnp.argmax(grp_score, axis=-1)
    in_grp = jnp.take_along_axis(sel, g_idx[:, None, None], axis=1)[:, 0]
    _, loc = lax.top_k(in_grp, TOP_K)
    e_idx = g_idx[:, None] * EXPERTS_PER_GROUP + loc
    a_sel = jnp.take_along_axis(aff, e_idx, axis=-1)
    wts = a_sel / jnp.sum(a_sel, axis=-1, keepdims=True)
    gates = jnp.sum(jax.nn.one_hot(e_idx, N_EXPERTS, dtype=F32) * wts[..., None], axis=1)
    out = jnp.zeros((T, d), F32)
    for e in range(N_EXPERTS):
        he = jax.nn.silu(t @ w_gate[e]) * (t @ w_up[e])
        out = out + gates[:, e:e + 1] * (he @ w_down[e]).astype(F32)
    return out.astype(h.dtype).reshape(bsz, L, d)


def setup_inputs(seed: int = 0) -> dict:
    key = jax.random.key(seed)
    ks = iter(jax.random.split(key, 48))
    nrm = lambda shape, std: std * jax.random.normal(next(ks), shape, F32)
    n_even = (DEPTH + 1) // 2
    n_odd = DEPTH // 2
    beta = (8.0 * DEPTH) ** -0.25
    dinv = D_MODEL ** -0.5
    lam_im = (jnp.pi * jnp.arange(S5_STATE, dtype=F32))[None, None] + nrm((n_even, S5_GROUPS, S5_STATE), 0.01)
    fgate_b = jnp.linspace(3.0, 6.0, ML_HEADS, dtype=F32)[None] + nrm((n_odd, ML_HEADS), 0.1)
    return {
        "x": nrm((BATCH, SEQ, D_MODEL), 1.0),
        "c": nrm((BATCH, D_MODEL), 1.0),
        "router_w": nrm((D_MODEL, N_EXPERTS), dinv),
        "router_bias": nrm((N_EXPERTS,), 0.01),
        "ada_w": nrm((DEPTH, D_MODEL, 6 * D_MODEL), 0.1 * dinv),
        "ada_b": nrm((DEPTH, 6 * D_MODEL), 0.02),
        "ln1_g": 1.0 + nrm((DEPTH, D_MODEL), 0.02),
        "ln1_b": nrm((DEPTH, D_MODEL), 0.02),
        "ln2_g": 1.0 + nrm((DEPTH, D_MODEL), 0.02),
        "ln2_b": nrm((DEPTH, D_MODEL), 0.02),
        "moe_w_gate": nrm((DEPTH, N_EXPERTS, D_MODEL, D_EXPERT), dinv),
        "moe_w_up": nrm((DEPTH, N_EXPERTS, D_MODEL, D_EXPERT), dinv),
        "moe_w_down": nrm((DEPTH, N_EXPERTS, D_EXPERT, D_MODEL), beta * D_EXPERT ** -0.5),
        "ab_w_in": nrm((n_even, D_MODEL, AB_IN), dinv),
        "s5_lam_re": -0.5 * jnp.exp(nrm((n_even, S5_GROUPS, S5_STATE), 0.02)),
        "s5_lam_im": lam_im,
        "s5_log_dt": jax.random.uniform(next(ks), (n_even, S5_GROUPS), F32,
                                        minval=math.log(0.001), maxval=math.log(0.1)),
        "s5_b_re": nrm((n_even, S5_GROUPS, S5_STATE, S5_GROUP), (2.0 * S5_GROUP) ** -0.5),
        "s5_b_im": nrm((n_even, S5_GROUPS, S5_STATE, S5_GROUP), (2.0 * S5_GROUP) ** -0.5),
        "s5_c_re": nrm((n_even, S5_GROUPS, S5_GROUP, S5_STATE), S5_STATE ** -0.5),
        "s5_c_im": nrm((n_even, S5_GROUPS, S5_GROUP, S5_STATE), S5_STATE ** -0.5),
        "s5_d": nrm((n_even, S5_W), 1.0),
        "s5_glu_w": nrm((n_even, S5_W, S5_W), S5_W ** -0.5),
        "s5_glu_b": nrm((n_even, S5_W), 0.02),
        "gla_gate_w": nrm((n_even, GLA_RANK, GLA_QK), GLA_RANK ** -0.5),
        "gla_gate_b": nrm((n_even, GLA_QK), 0.1),
        "gla_norm_g": 1.0 + nrm((n_even, GLA_V), 0.02),
        "ab_w_out": nrm((n_even, AB_OUT, D_MODEL), beta * AB_OUT ** -0.5),
        "ml_w_in": nrm((n_odd, D_MODEL, ML_IN), dinv),
        "ml_conv_w": nrm((n_odd, CONV_W, 2 * ML_W), CONV_W ** -0.5),
        "ml_conv_b": nrm((n_odd, 2 * ML_W), 0.02),
        "ml_igate_b": nrm((n_odd, ML_HEADS), 0.1),
        "ml_fgate_b": fgate_b,
        "ml_norm_g": 1.0 + nrm((n_odd, ML_W), 0.02),
        "ml_w_out": nrm((n_odd, ML_W, D_MODEL), beta * ML_W ** -0.5),
    }


def reference(x, c, router_w, router_bias, ada_w, ada_b, ln1_g, ln1_b, ln2_g, ln2_b,
              moe_w_gate, moe_w_up, moe_w_down, ab_w_in, s5_lam_re, s5_lam_im, s5_log_dt,
              s5_b_re, s5_b_im, s5_c_re, s5_c_im, s5_d, s5_glu_w, s5_glu_b,
              gla_gate_w, gla_gate_b, gla_norm_g, ab_w_out, ml_w_in, ml_conv_w, ml_conv_b,
              ml_igate_b, ml_fgate_b, ml_norm_g, ml_w_out):
    alpha = (2.0 * DEPTH) ** 0.25
    c_act = jax.nn.silu(c)
    for layer in range(DEPTH):
        j = layer // 2
        mod = c_act @ ada_w[layer] + ada_b[layer]
        sh1, sc1, g1, sh2, sc2, g2 = jnp.split(mod[:, None, :], 6, axis=-1)
        h = x * (1.0 + sc1) + sh1
        if layer % 2 == 0:
            y = ab_mixer(h, ab_w_in[j], s5_lam_re[j], s5_lam_im[j], s5_log_dt[j],
                         s5_b_re[j], s5_b_im[j], s5_c_re[j], s5_c_im[j], s5_d[j],
                         s5_glu_w[j], s5_glu_b[j], gla_gate_w[j], gla_gate_b[j],
                         gla_norm_g[j], ab_w_out[j])
        else:
            y = mlstm_mixer(h, ml_w_in[j], ml_conv_w[j], ml_conv_b[j], ml_igate_b[j],
                            ml_fgate_b[j], ml_norm_g[j], ml_w_out[j])
        x = layer_norm(alpha * x + (1.0 + g1) * y, ln1_g[layer], ln1_b[layer])
        h = x * (1.0 + sc2) + sh2
        y = grouped_moe(h, router_w, router_bias, moe_w_gate[layer], moe_w_up[layer], moe_w_down[layer])
        x = layer_norm(alpha * x + (1.0 + g2) * y, ln2_g[layer], ln2_b[layer])
    return x
```

```python
import functools
import math

import jax
import jax.numpy as jnp
from jax import lax
from jax.experimental import pallas as pl
from jax.experimental.pallas import tpu as pltpu

F32 = jnp.float32
BF16 = jnp.bfloat16

D_MODEL = 1024
DEPTH = 4
S5_W = 512
S5_GROUP = 16
S5_GROUPS = 32
S5_STATE = 64
S5_CHUNK = 16
GLA_HEADS = 4
GLA_DV = 128
GLA_DK = 64
GLA_QK = 256
GLA_V = 512
GLA_RANK = 16
GLA_GATE_NORM = 16.0
GLA_SUB = 16
AB_IN = S5_W + 2 * GLA_QK + 2 * GLA_V + GLA_RANK
AB_IN_PAD = 2176
ML_HEADS = 8
ML_W = 1024
ML_DH = 128
ML_IN = 4 * ML_W + 2 * ML_HEADS
ML_IN_PAD = 4224
CONV_W = 4
CHUNK = 64
N_EXPERTS = 16
N_GROUPS = 4
EPG = 4
D_EXPERT = 256
N_PAIRS = 6
N_CLASSES = N_GROUPS * N_PAIRS
PAIR_LO = (0, 0, 0, 1, 1, 2)
PAIR_HI = (1, 2, 3, 2, 3, 3)
ALPHA = (2.0 * DEPTH) ** 0.25
LN_EPS = 1e-5
RMS_EPS = 1e-6

V7X_VMEM_BYTES = 64 * 1024 * 1024
VMEM_LIMIT = (V7X_VMEM_BYTES * 3) // 4
ROW_TILE = 256
SEQ_TILE = 256
ROUTE_TILE = 512
DMA_BATCH = 64


def _cparams(*sem):
    return pltpu.CompilerParams(dimension_semantics=sem, vmem_limit_bytes=VMEM_LIMIT)


def _sigmoid(x):
    return 1.0 / (1.0 + jnp.exp(-x))


def _silu(x):
    return x * _sigmoid(x)


def _log_sigmoid(z):
    return jnp.minimum(z, 0.0) - jnp.log1p(jnp.exp(-jnp.abs(z)))


def _gelu_tanh(x):
    return 0.5 * x * (1.0 + jnp.tanh(math.sqrt(2.0 / math.pi) * (x + 0.044715 * (x * x * x))))


def _layer_norm(x, g, b):
    mu = jnp.mean(x, axis=-1, keepdims=True)
    xc = x - mu
    var = jnp.mean(xc * xc, axis=-1, keepdims=True)
    return xc * lax.rsqrt(var + LN_EPS) * g + b


def _split_bf16(x):
    hi = x.astype(BF16)
    lo = (x - hi.astype(F32)).astype(BF16)
    return hi, lo


def _dot(a, b):
    return jnp.dot(a, b, preferred_element_type=F32)


def _dot_nt(a, b):
    return lax.dot_general(a, b, (((1,), (1,)), ((), ())), preferred_element_type=F32)


def _dot_tn(a, b):
    return lax.dot_general(a, b, (((0,), (0,)), ((), ())), preferred_element_type=F32)


def _ada_body(c_ref, w_ref, b_ref, o_ref):
    c = c_ref[...]
    o_ref[0] = _dot(_silu(c), w_ref[0]) + b_ref[0]


def _ada_mod(c, ada_w, ada_b):
    bsz = c.shape[0]
    tn = 1536
    cp = jnp.zeros((8, D_MODEL), F32).at[:bsz].set(c)
    out = pl.pallas_call(
        _ada_body,
        grid=(DEPTH, 6 * D_MODEL // tn),
        in_specs=[pl.BlockSpec((8, D_MODEL), lambda l, j: (0, 0)),
                  pl.BlockSpec((1, D_MODEL, tn), lambda l, j: (l, 0, j)),
                  pl.BlockSpec((1, 1, tn), lambda l, j: (l, 0, j))],
        out_specs=pl.BlockSpec((1, 8, tn), lambda l, j: (l, 0, j)),
        out_shape=jax.ShapeDtypeStruct((DEPTH, 8, 6 * D_MODEL), F32),
        compiler_params=_cparams("parallel", "parallel"),
        name="ada_mod",
    )(cp, ada_w, ada_b.reshape(DEPTH, 1, 6 * D_MODEL))
    return out[:, :bsz].reshape(DEPTH, bsz, 6, D_MODEL)


def _ab_in_body(x_ref, mod_ref, w_ref, gw_ref, gb_ref, u_ref, q_ref, k_ref, v_ref, r_ref, la_ref):
    x = x_ref[...]
    h = (x * (1.0 + mod_ref[0, 1:2, :]) + mod_ref[0, 0:1, :]).astype(BF16)

    def seg(a, b):
        return _dot(h, w_ref[:, a:b])

    u_ref[...] = seg(0, 512)
    q_ref[...] = seg(512, 768)
    k_ref[...] = seg(768, 1024)
    v_ref[...] = seg(1024, 1536)
    r_ref[...] = seg(1536, 2048)
    a_lr = seg(2048, AB_IN_PAD)
    z = _dot(a_lr.astype(BF16), gw_ref[...]) + gb_ref[...]
    la_ref[...] = _log_sigmoid(z) * (1.0 / GLA_GATE_NORM)


def _ab_in(x, mod6, w_in, gate_w, gate_b, seq):
    t = x.shape[0]
    tm = ROW_TILE
    per_b = seq // tm
    w = jnp.zeros((D_MODEL, AB_IN_PAD), BF16).at[:, :AB_IN].set(w_in.astype(BF16))
    gw = jnp.zeros((AB_IN_PAD - 2048, GLA_QK), BF16).at[:GLA_RANK].set(gate_w.astype(BF16))
    row = lambda n: pl.BlockSpec((tm, n), lambda i: (i, 0))
    full = lambda a: pl.BlockSpec(a.shape, lambda i: (0,) * a.ndim)
    outs = [S5_W, GLA_QK, GLA_QK, GLA_V, GLA_V, GLA_QK]
    return pl.pallas_call(
        _ab_in_body,
        grid=(t // tm,),
        in_specs=[row(D_MODEL),
                  pl.BlockSpec((1, 6, D_MODEL), lambda i: (i // per_b, 0, 0)),
                  full(w), full(gw), pl.BlockSpec((1, GLA_QK), lambda i: (0, 0))],
        out_specs=[row(n) for n in outs],
        out_shape=[jax.ShapeDtypeStruct((t, n), F32) for n in outs],
        compiler_params=_cparams("parallel"),
        name="ab_in",
    )(x, mod6, w, gw, gate_b.reshape(1, GLA_QK))


def _s5_prep(lam_re, lam_im, log_dt, b_re, b_im, c_re, c_im):
    hp = lax.Precision.HIGHEST
    n = S5_CHUNK
    lre, lim = lam_re.astype(F32), lam_im.astype(F32)
    dt = jnp.exp(log_dt.astype(F32))[:, None]
    d = jnp.arange(n + 1, dtype=F32)[:, None, None]
    mag = jnp.exp(lre * dt * d)
    ang = lim * dt * d
    pw_re, pw_im = mag * jnp.cos(ang), mag * jnp.sin(ang)
    den = lre * lre + lim * lim
    nre, nim = pw_re[1] - 1.0, pw_im[1]
    coef_re = (nre * lre + nim * lim) / den
    coef_im = (nim * lre - nre * lim) / den
    bre, bim = b_re.astype(F32), b_im.astype(F32)
    bb_re = coef_re[..., None] * bre - coef_im[..., None] * bim
    bb_im = coef_re[..., None] * bim + coef_im[..., None] * bre
    cre, cim = c_re.astype(F32), c_im.astype(F32)
    cp_re = cre[None] * pw_re[:, :, None, :] - cim[None] * pw_im[:, :, None, :]
    cp_im = cre[None] * pw_im[:, :, None, :] + cim[None] * pw_re[:, :, None, :]
    kern = (jnp.einsum('dgcp,gpe->dgce', cp_re, bb_re, precision=hp)
            - jnp.einsum('dgcp,gpe->dgce', cp_im, bb_im, precision=hp))
    jj = jnp.arange(n)[:, None]
    tt = jnp.arange(n)[None, :]
    lag = tt - jj
    kg = jnp.where((lag >= 0)[:, :, None, None, None], kern[jnp.clip(lag, 0, n)], 0.0)
    m_intra = jnp.transpose(kg, (2, 0, 4, 1, 3)).reshape(S5_GROUPS, n * S5_GROUP, n * S5_GROUP)
    pws_re, pws_im = pw_re[n - 1 - jnp.arange(n)], pw_im[n - 1 - jnp.arange(n)]
    ws_re = pws_re[..., None] * bb_re[None] - pws_im[..., None] * bb_im[None]
    ws_im = pws_re[..., None] * bb_im[None] + pws_im[..., None] * bb_re[None]
    w_state = jnp.transpose(jnp.concatenate([ws_re, ws_im], axis=2), (1, 0, 3, 2))
    w_state = w_state.reshape(S5_GROUPS, n * S5_GROUP, 2 * S5_STATE)
    wi_re = jnp.transpose(cp_re[1:], (1, 3, 0, 2))
    wi_im = -jnp.transpose(cp_im[1:], (1, 3, 0, 2))
    w_inter = jnp.concatenate([wi_re, wi_im], axis=1).reshape(S5_GROUPS, 2 * S5_STATE, n * S5_GROUP)
    a_chunk = jnp.stack([pw_re[n], pw_im[n]], axis=1)
    return m_intra.astype(BF16), w_state.astype(BF16), w_inter.astype(BF16), a_chunk


def _s5_body(u_ref, m_ref, ws_ref, wi_ref, a_ref, y_ref, ere, eim, sre, sim):
    u = u_ref[0]
    e = _dot(u, ws_ref[0])
    ere[...] = e[:, :S5_STATE]
    eim[...] = e[:, S5_STATE:]
    ar = jnp.broadcast_to(a_ref[0, 0:1, :], (8, S5_STATE))
    ai = jnp.broadcast_to(a_ref[0, 1:2, :], (8, S5_STATE))
    lo = lax.broadcasted_iota(jnp.int32, (8, S5_STATE), 0) < 4
    n_tiles = ere.shape[0] // 8

    def tile(i, carry):
        c_r, c_i = carry
        r0 = pl.multiple_of(i * 8, 8)
        e_r, e_i = ere[pl.ds(r0, 8), :], eim[pl.ds(r0, 8), :]
        e_r4, e_i4 = pltpu.roll(e_r, 4, 0), pltpu.roll(e_i, 4, 0)
        t_r = ar * c_r - ai * c_i + e_r4
        t_i = ar * c_i + ai * c_r + e_i4
        sp_r, sp_i = jnp.where(lo, c_r, t_r), jnp.where(lo, c_i, t_i)
        sre[pl.ds(r0, 8), :] = sp_r
        sim[pl.ds(r0, 8), :] = sp_i
        n_r = ar * sp_r - ai * sp_i + e_r
        n_i = ar * sp_i + ai * sp_r + e_i
        return (jnp.where(lo, pltpu.roll(n_r, 4, 0), n_r), jnp.where(lo, pltpu.roll(n_i, 4, 0), n_i))

    z = jnp.zeros((8, S5_STATE), F32)
    lax.fori_loop(0, n_tiles, tile, (z, z))
    y = _dot(u, m_ref[0])
    y = y + _dot(sre[...].astype(BF16), wi_ref[0, :S5_STATE, :])
    y = y + _dot(sim[...].astype(BF16), wi_ref[0, S5_STATE:, :])
    y_ref[0] = y


def _s5_conv(u, tables, bsz, seq):
    assert bsz == 4
    m_intra, w_state, w_inter, a_chunk = tables
    nk = seq // S5_CHUNK
    rows = nk * bsz
    lanes = S5_CHUNK * S5_GROUP
    uf = u.astype(BF16).reshape(bsz, nk, S5_CHUNK, S5_GROUPS, S5_GROUP)
    uf = jnp.transpose(uf, (3, 1, 0, 2, 4)).reshape(S5_GROUPS, rows, lanes)
    grp = lambda shape: pl.BlockSpec((1,) + shape, lambda g: (g, 0, 0))
    yf = pl.pallas_call(
        _s5_body,
        grid=(S5_GROUPS,),
        in_specs=[grp((rows, lanes)), grp((lanes, lanes)), grp((lanes, 2 * S5_STATE)),
                  grp((2 * S5_STATE, lanes)), grp((2, S5_STATE))],
        out_specs=grp((rows, lanes)),
        out_shape=jax.ShapeDtypeStruct((S5_GROUPS, rows, lanes), F32),
        scratch_shapes=[pltpu.VMEM((rows, S5_STATE), F32)] * 4,
        compiler_params=_cparams("parallel"),
        name="s5_conv",
    )(uf, m_intra, w_state, w_inter, a_chunk)
    yf = yf.reshape(S5_GROUPS, nk, bsz, S5_CHUNK, S5_GROUP)
    return jnp.transpose(yf, (2, 1, 3, 0, 4)).reshape(bsz * seq, S5_W)


def _gla_body(q_ref, k_ref, g_ref, v_ref, r_ref, ng_ref, y_ref, s_ref, bc_ref, o_ref):
    tl = q_ref.shape[0]
    scale = GLA_DK ** -0.5

    @pl.when(pl.program_id(1) == 0)
    def _():
        s_ref[...] = jnp.zeros_like(s_ref)

    row = lax.broadcasted_iota(jnp.int32, (CHUNK, CHUNK), 0)
    col = lax.broadcasted_iota(jnp.int32, (CHUNK, CHUNK), 1)
    tri = col <= row
    tref = col < (row // GLA_SUB) * GLA_SUB
    cum_mat = jnp.concatenate([tri, tref], axis=0).astype(BF16)
    lane_head = lax.broadcasted_iota(jnp.int32, (GLA_SUB, GLA_QK), 1) // GLA_DK
    bd_mask = (lax.broadcasted_iota(jnp.int32, (GLA_QK, GLA_V), 0) // GLA_DK
               == lax.broadcasted_iota(jnp.int32, (GLA_QK, GLA_V), 1) // GLA_DV)
    ones_bd = bd_mask.astype(BF16)
    ones_v = jnp.ones((CHUNK, GLA_V), BF16)

    def chunk(c, carry):
        r0 = pl.multiple_of(c * CHUNK, CHUNK)
        qc = q_ref[pl.ds(r0, CHUNK), :] * scale
        kc = k_ref[pl.ds(r0, CHUNK), :]
        g_hi, g_lo = _split_bf16(g_ref[pl.ds(r0, CHUNK), :])
        vb = v_ref[pl.ds(r0, CHUNK), :].astype(BF16)
        br = _dot(cum_mat, g_hi) + _dot(cum_mat, g_lo)
        bc, ref = br[:CHUNK], br[CHUNK:]
        bc_ref[pl.ds(r0, CHUNK), :] = bc
        qt = qc * jnp.exp(bc - ref)
        state = s_ref[...]
        o = _dot((qt * jnp.exp(ref)).astype(BF16), state.astype(BF16))
        blocks = [o[:GLA_SUB]]
        for i in range(1, CHUNK // GLA_SUB):
            ri = bc[GLA_SUB * i - 1:GLA_SUB * i, :]
            kt = (kc * jnp.exp(jnp.minimum(ri - bc, 0.0))).astype(BF16)
            qi = qt[GLA_SUB * i:GLA_SUB * (i + 1)]
            lhs = jnp.concatenate([jnp.where(lane_head == h, qi, 0.0) for h in range(GLA_HEADS)], axis=0)
            att = _dot_nt(lhs.astype(BF16), kt)
            att = jnp.where(col < GLA_SUB * i, att, 0.0)
            ov = _dot(att.astype(BF16), vb)
            oi = jnp.concatenate([ov[GLA_SUB * h:GLA_SUB * (h + 1), GLA_DV * h:GLA_DV * (h + 1)]
                                  for h in range(GLA_HEADS)], axis=1)
            blocks.append(o[GLA_SUB * i:GLA_SUB * (i + 1)] + oi)
        o_ref[pl.ds(r0, CHUNK), :] = jnp.concatenate(blocks, axis=0)
        bl = bc[CHUNK - 1:CHUNK, :]
        kh = (kc * jnp.exp(bl - bc)).astype(BF16)
        upd = _dot_tn(kh, vb)
        dcol = _dot_tn(g_hi, ones_v) + _dot_tn(g_lo, ones_v)
        s_ref[...] = jnp.exp(dcol) * state + jnp.where(bd_mask, upd, 0.0)
        return carry

    lax.fori_loop(0, tl // CHUNK, chunk, 0)

    q = q_ref[...] * scale
    k = k_ref[...]
    v = v_ref[...]
    bc = bc_ref[...]
    rmod = lax.broadcasted_iota(jnp.int32, (tl, 1), 0) % GLA_SUB
    od = jnp.zeros((tl, GLA_V), F32)
    for d in range(GLA_SUB):
        kd = k if d == 0 else pltpu.roll(k, d, 0)
        bd = bc if d == 0 else pltpu.roll(bc, d, 0)
        vd = v if d == 0 else pltpu.roll(v, d, 0)
        valid = rmod >= d
        e = jnp.exp(jnp.where(valid, bc - bd, 0.0))
        p = jnp.where(valid, q * kd * e, 0.0).astype(BF16)
        od = od + _dot(p, ones_bd) * vd
    o = o_ref[...] + od
    ng = ng_ref[...]
    outs = []
    for h in range(GLA_HEADS):
        oh = o[:, GLA_DV * h:GLA_DV * (h + 1)]
        outs.append(oh * lax.rsqrt(jnp.mean(oh * oh, axis=-1, keepdims=True) + RMS_EPS))
    y_ref[...] = jnp.concatenate(outs, axis=1) * ng * _silu(r_ref[...])


def _gla(q, k, la, v, r, norm_g, bsz, seq):
    tl = SEQ_TILE
    nl = seq // tl
    blk = lambda n: pl.BlockSpec((tl, n), lambda b, l: (b * nl + l, 0))
    return pl.pallas_call(
        _gla_body,
        grid=(bsz, nl),
        in_specs=[blk(GLA_QK), blk(GLA_QK), blk(GLA_QK), blk(GLA_V), blk(GLA_V),
                  pl.BlockSpec((1, GLA_V), lambda b, l: (0, 0))],
        out_specs=blk(GLA_V),
        out_shape=jax.ShapeDtypeStruct((bsz * seq, GLA_V), F32),
        scratch_shapes=[pltpu.VMEM((GLA_QK, GLA_V), F32), pltpu.VMEM((tl, GLA_QK), F32),
                        pltpu.VMEM((tl, GLA_V), F32)],
        compiler_params=_cparams("parallel", "arbitrary"),
        name="gla",
    )(q, k, la, v, r, norm_g.reshape(1, GLA_V))


def _residual_ln(x, y, mod_ref, gate_row, lg_ref, lb_ref):
    return _layer_norm(ALPHA * x + (1.0 + mod_ref[0, gate_row:gate_row + 1, :]) * y, lg_ref[...], lb_ref[...])


def _ab_out_body(ys_ref, u_ref, yb_ref, x_ref, mod_ref, d_ref, gw_ref, gb_ref, wa_ref, wb_ref, lg_ref, lb_ref, o_ref):
    z = _gelu_tanh(ys_ref[...] + d_ref[...] * u_ref[...])
    ya = z * _sigmoid(_dot(z.astype(BF16), gw_ref[...]) + gb_ref[...])
    y = _dot(ya.astype(BF16), wa_ref[...]) + _dot(yb_ref[...].astype(BF16), wb_ref[...])
    o_ref[...] = _residual_ln(x_ref[...], y, mod_ref, 2, lg_ref, lb_ref)


def _ml_out_body(y_ref, x_ref, mod_ref, w_ref, lg_ref, lb_ref, o_ref):
    y = _dot(y_ref[...].astype(BF16), w_ref[...])
    o_ref[...] = _residual_ln(x_ref[...], y, mod_ref, 2, lg_ref, lb_ref)


def _out_call(body, name, row_inputs, x, mod6, consts, ln_g, ln_b, seq):
    t = x.shape[0]
    tm = ROW_TILE
    per_b = seq // tm
    row = lambda a: pl.BlockSpec((tm, a.shape[1]), lambda i: (i, 0))
    full = lambda a: pl.BlockSpec(a.shape, lambda i: (0,) * a.ndim)
    lg, lb = ln_g.reshape(1, D_MODEL), ln_b.reshape(1, D_MODEL)
    return pl.pallas_call(
        body,
        grid=(t // tm,),
        in_specs=([row(a) for a in row_inputs] + [row(x), pl.BlockSpec((1, 6, D_MODEL), lambda i: (i // per_b, 0, 0))]
                  + [full(a) for a in consts] + [full(lg), full(lb)]),
        out_specs=row(x),
        out_shape=jax.ShapeDtypeStruct((t, D_MODEL), F32),
        compiler_params=_cparams("parallel"),
        name=name,
    )(*row_inputs, x, mod6, *consts, lg, lb)


def _ml_in_body(x_ref, mod_ref, w_ref, gb_ref, qk_ref, v_ref, o_ref, gt_ref):
    x = x_ref[...]
    h = (x * (1.0 + mod_ref[0, 1:2, :]) + mod_ref[0, 0:1, :]).astype(BF16)
    qk_ref[...] = _dot(h, w_ref[:, 0:2 * ML_W])
    v_ref[...] = _dot(h, w_ref[:, 2 * ML_W:3 * ML_W])
    o_ref[...] = _dot(h, w_ref[:, 3 * ML_W:4 * ML_W])
    gt_ref[...] = _dot(h, w_ref[:, 4 * ML_W:ML_IN_PAD]) + gb_ref[...]


def _ml_in(x, mod6, w_in, igate_b, fgate_b, seq):
    t = x.shape[0]
    tm = ROW_TILE
    per_b = seq // tm
    w = jnp.zeros((D_MODEL, ML_IN_PAD), BF16).at[:, :ML_IN].set(w_in.astype(BF16))
    gb = jnp.zeros((1, ML_IN_PAD - 4 * ML_W), F32).at[0, :2 * ML_HEADS].set(jnp.concatenate([igate_b, fgate_b]))
    row = lambda n: pl.BlockSpec((tm, n), lambda i: (i, 0))
    full = lambda a: pl.BlockSpec(a.shape, lambda i: (0,) * a.ndim)
    outs = [2 * ML_W, ML_W, ML_W, ML_IN_PAD - 4 * ML_W]
    return pl.pallas_call(
        _ml_in_body,
        grid=(t // tm,),
        in_specs=[row(D_MODEL), pl.BlockSpec((1, 6, D_MODEL), lambda i: (i // per_b, 0, 0)), full(w), full(gb)],
        out_specs=[row(n) for n in outs],
        out_shape=[jax.ShapeDtypeStruct((t, n), F32) for n in outs],
        compiler_params=_cparams("parallel"),
        name="ml_in",
    )(x, mod6, w, gb)


def _mlstm_body(q_ref, k_ref, v_ref, og_ref, ic_ref, fc_ref, ir_ref, fr_ref, wq_ref, wk_ref, bq_ref, bk_ref, ng_ref,
                y_ref, s_ref, n_ref, m_ref, cq_ref, ck_ref, qs_ref, ks_ref, h_ref):
    tl = q_ref.shape[0]

    @pl.when(pl.program_id(2) == 0)
    def _():
        s_ref[...] = jnp.zeros_like(s_ref)
        n_ref[...] = jnp.zeros_like(n_ref)
        m_ref[...] = jnp.zeros_like(m_ref)
        cq_ref[...] = jnp.zeros_like(cq_ref)
        ck_ref[...] = jnp.zeros_like(ck_ref)

    def conv(x_ref, carry_ref, w_ref, b_ref):
        x = x_ref[...]
        xin = jnp.concatenate([carry_ref[...], x], axis=0)
        acc = jnp.zeros((tl, ML_DH), F32) + b_ref[...]
        for i in range(CONV_W):
            sh = CONV_W - 1 - i
            xs = xin if sh == 0 else pltpu.roll(xin, sh, 0)
            acc = acc + w_ref[i:i + 1, :] * xs[8:8 + tl]
        carry_ref[...] = x[tl - 8:tl]
        return _silu(acc)

    qs_ref[...] = conv(q_ref, cq_ref, wq_ref, bq_ref)
    ks_ref[...] = conv(k_ref, ck_ref, wk_ref, bk_ref) * (ML_DH ** -0.5)

    row = lax.broadcasted_iota(jnp.int32, (CHUNK, CHUNK), 0)
    col = lax.broadcasted_iota(jnp.int32, (CHUNK, CHUNK), 1)
    causal = col <= row

    def chunk(c, carry):
        r0 = pl.multiple_of(c * CHUNK, CHUNK)
        qc = qs_ref[pl.ds(r0, CHUNK), :]
        kc = ks_ref[pl.ds(r0, CHUNK), :]
        vb = v_ref[pl.ds(r0, CHUNK), :].astype(BF16)
        i_c, i_r = ic_ref[c], ir_ref[c]
        lf_c, lf_r = _log_sigmoid(fc_ref[c]), _log_sigmoid(fr_ref[c])
        b_c = jnp.sum(jnp.where(causal, lf_r, 0.0), axis=1, keepdims=True)
        b_r = jnp.sum(jnp.where(row <= col, lf_c, 0.0), axis=0, keepdims=True)
        m = m_ref[...]
        dmat = jnp.where(causal, b_c - b_r + i_r, -jnp.inf)
        inter_log = b_c + m
        m_t = jnp.maximum(inter_log, jnp.max(dmat, axis=1, keepdims=True))
        w_intra = jnp.exp(dmat - m_t)
        w_inter = jnp.exp(inter_log - m_t)
        qb = qc.astype(BF16)
        s = _dot_nt(qb, kc.astype(BF16)) * w_intra
        state, nrm = s_ref[...], n_ref[...]
        num = w_inter * _dot(qb, state.astype(BF16)) + _dot(s.astype(BF16), vb)
        den = w_inter * jnp.sum(qc * nrm, axis=1, keepdims=True) + jnp.sum(s, axis=1, keepdims=True)
        h_ref[pl.ds(r0, CHUNK), :] = num / jnp.maximum(jnp.abs(den), jnp.exp(-m_t))
        b_last = b_c[CHUNK - 1:CHUNK, :]
        gs_c = b_last - b_c + i_c
        gs_r = b_last - b_r + i_r
        m_new = jnp.maximum(b_last + m, jnp.max(gs_r, axis=1, keepdims=True))
        dec = jnp.exp(b_last + m - m_new)
        kw = kc * jnp.exp(gs_c - m_new)
        s_ref[...] = dec * state + _dot_tn(kw.astype(BF16), vb)
        n_ref[...] = dec * nrm + jnp.sum(kw, axis=0, keepdims=True)
        m_ref[...] = m_new
        return carry

    lax.fori_loop(0, tl // CHUNK, chunk, 0)
    hh = h_ref[...]
    hh = hh * lax.rsqrt(jnp.mean(hh * hh, axis=-1, keepdims=True) + RMS_EPS)
    y_ref[...] = _sigmoid(og_ref[...]) * (hh * ng_ref[...])


def _mlstm(qk, v, og, gates, conv_w, conv_b, norm_g, bsz, seq):
    tl = SEQ_TILE
    nl = seq // tl
    nc = tl // CHUNK
    gt = gates[:, :2 * ML_HEADS].reshape(bsz, seq // CHUNK, CHUNK, 2, ML_HEADS)
    gt = jnp.transpose(gt, (3, 0, 4, 1, 2)).reshape(2, bsz * ML_HEADS * (seq // CHUNK), CHUNK)
    i_col, f_col = gt[0][:, :, None], gt[1][:, :, None]
    i_row, f_row = gt[0][:, None, :], gt[1][:, None, :]
    hcol = lambda off: pl.BlockSpec((tl, ML_DH), lambda b, h, l: (b * nl + l, h + off))
    gcol = pl.BlockSpec((nc, CHUNK, 1), lambda b, h, l: ((b * ML_HEADS + h) * nl + l, 0, 0))
    grow = pl.BlockSpec((nc, 1, CHUNK), lambda b, h, l: ((b * ML_HEADS + h) * nl + l, 0, 0))
    par = lambda rows, off: pl.BlockSpec((rows, ML_DH), lambda b, h, l: (0, h + off))
    return pl.pallas_call(
        _mlstm_body,
        grid=(bsz, ML_HEADS, nl),
        in_specs=[hcol(0), hcol(ML_HEADS), hcol(0), hcol(0), gcol, gcol, grow, grow,
                  par(CONV_W, 0), par(CONV_W, ML_HEADS), par(1, 0), par(1, ML_HEADS), par(1, 0)],
        out_specs=hcol(0),
        out_shape=jax.ShapeDtypeStruct((bsz * seq, ML_W), F32),
        scratch_shapes=[pltpu.VMEM((ML_DH, ML_DH), F32), pltpu.VMEM((1, ML_DH), F32), pltpu.VMEM((1, 1), F32),
                        pltpu.VMEM((8, ML_DH), F32), pltpu.VMEM((8, ML_DH), F32),
                        pltpu.VMEM((tl, ML_DH), F32), pltpu.VMEM((tl, ML_DH), F32), pltpu.VMEM((tl, ML_DH), F32)],
        compiler_params=_cparams("parallel", "parallel", "arbitrary"),
        name="mlstm",
    )(qk, qk, v, og, i_col, f_col, i_row, f_row, conv_w, conv_w, conv_b.reshape(1, -1), conv_b.reshape(1, -1),
      norm_g.reshape(1, ML_W))


def _router_body(x_ref, mod_ref, rw_ref, rb_ref, info_ref, cnt_ref, carry_ref):
    tr = x_ref.shape[0]

    @pl.when(pl.program_id(0) == 0)
    def _():
        carry_ref[...] = jnp.zeros_like(carry_ref)

    h = x_ref[...] * (1.0 + mod_ref[0, 4:5, :]) + mod_ref[0, 3:4, :]
    logits = lax.dot_general(rw_ref[...], h, (((1,), (1,)), ((), ())), precision=lax.Precision.HIGHEST,
                             preferred_element_type=F32)
    aff = _sigmoid(logits)
    sel = aff + rb_ref[...]
    s = [sel[e:e + 1, :] for e in range(N_EXPERTS)]
    a = [aff[e:e + 1, :] for e in range(N_EXPERTS)]

    def top2_sum(v):
        best = v[0] + v[1]
        for lo_i, hi_i in zip(PAIR_LO[1:], PAIR_HI[1:]):
            best = jnp.maximum(best, v[lo_i] + v[hi_i])
        return best

    score = [top2_sum(s[EPG * g:EPG * (g + 1)]) for g in range(N_GROUPS)]
    g_idx = jnp.zeros((1, tr), jnp.int32)
    best = score[0]
    for g in range(1, N_GROUPS):
        upd = score[g] > best
        g_idx = jnp.where(upd, g, g_idx)
        best = jnp.where(upd, score[g], best)

    def pick_group(rows, j):
        out = rows[j]
        for g in range(1, N_GROUPS):
            out = jnp.where(g_idx == g, rows[EPG * g + j], out)
        return out

    sg = [pick_group(s, j) for j in range(EPG)]
    ag = [pick_group(a, j) for j in range(EPG)]
    first = jnp.zeros((1, tr), jnp.int32)
    best = sg[0]
    for j in range(1, EPG):
        upd = sg[j] > best
        first = jnp.where(upd, j, first)
        best = jnp.where(upd, sg[j], best)
    second = jnp.zeros((1, tr), jnp.int32)
    best2 = jnp.full((1, tr), -jnp.inf, F32)
    for j in range(EPG):
        upd = (first != j) & (sg[j] > best2)
        second = jnp.where(upd, j, second)
        best2 = jnp.where(upd, sg[j], best2)
    lo = jnp.minimum(first, second)
    hi = jnp.maximum(first, second)
    pair = jnp.where(lo == 0, hi - 1, jnp.where(lo == 1, hi + 1, N_PAIRS - 1))
    cls = g_idx * N_PAIRS + pair

    def pick_local(rows, idx):
        out = rows[0]
        for j in range(1, EPG):
            out = jnp.where(idx == j, rows[j], out)
        return out

    a_lo, a_hi = pick_local(ag, lo), pick_local(ag, hi)
    tot = a_lo + a_hi
    onehot = (lax.broadcasted_iota(jnp.int32, (32, tr), 0) == cls).astype(F32)
    before = (lax.broadcasted_iota(jnp.int32, (tr, tr), 0) < lax.broadcasted_iota(jnp.int32, (tr, tr), 1)).astype(BF16)
    cum = _dot(onehot.astype(BF16), before) + carry_ref[:, 0:1]
    rank = jnp.sum(onehot * cum, axis=0, keepdims=True)
    carry_ref[...] = carry_ref[...] + jnp.sum(onehot, axis=1, keepdims=True)
    cnt_ref[...] = carry_ref[...]
    zero = jnp.zeros((1, tr), F32)
    info_ref[...] = jnp.concatenate([cls.astype(F32), rank, a_lo / tot, a_hi / tot, zero, zero, zero, zero], axis=0)


def _router(x, mod6, router_w, router_bias, seq):
    t = x.shape[0]
    tr = ROUTE_TILE
    per_b = seq // tr
    return pl.pallas_call(
        _router_body,
        grid=(t // tr,),
        in_specs=[pl.BlockSpec((tr, D_MODEL), lambda i: (i, 0)),
                  pl.BlockSpec((1, 6, D_MODEL), lambda i: (i // per_b, 0, 0)),
                  pl.BlockSpec((N_EXPERTS, D_MODEL), lambda i: (0, 0)),
                  pl.BlockSpec((N_EXPERTS, 1), lambda i: (0, 0))],
        out_specs=[pl.BlockSpec((8, tr), lambda i: (0, i)), pl.BlockSpec((32, 128), lambda i: (0, 0))],
        out_shape=[jax.ShapeDtypeStruct((8, t), F32), jax.ShapeDtypeStruct((32, 128), F32)],
        scratch_shapes=[pltpu.VMEM((32, 128), F32)],
        compiler_params=_cparams("arbitrary"),
        name="router",
    )(x, mod6, router_w.T.astype(F32), router_bias.reshape(N_EXPERTS, 1).astype(F32))


def _permute_body(idx_ref, src_ref, dst_in_ref, dst_ref, sems, *, n_rows, gather):
    del dst_in_ref
    n_batches = n_rows // DMA_BATCH

    def copy(t, slot):
        j = idx_ref[t]
        if gather:
            return pltpu.make_async_copy(src_ref.at[j], dst_ref.at[t], sems.at[slot])
        return pltpu.make_async_copy(src_ref.at[t], dst_ref.at[j], sems.at[slot])

    def issue(b):
        def one(i, c):
            copy(b * DMA_BATCH + i, b % 2).start()
            return c
        lax.fori_loop(0, DMA_BATCH, one, 0)

    def drain(b):
        def one(i, c):
            copy(b * DMA_BATCH + i, b % 2).wait()
            return c
        lax.fori_loop(0, DMA_BATCH, one, 0)

    issue(0)

    def step(b, c):
        issue(b)
        drain(b - 1)
        return c

    lax.fori_loop(1, n_batches, step, 0)
    drain(n_batches - 1)


def _permute_rows(idx, src, dst_init, gather):
    n_rows = idx.shape[0]
    assert n_rows % DMA_BATCH == 0
    return pl.pallas_call(
        functools.partial(_permute_body, n_rows=n_rows, gather=gather),
        grid_spec=pltpu.PrefetchScalarGridSpec(
            num_scalar_prefetch=1,
            grid=(1,),
            in_specs=[pl.BlockSpec(memory_space=pl.ANY), pl.BlockSpec(memory_space=pl.ANY)],
            out_specs=pl.BlockSpec(memory_space=pl.ANY),
            scratch_shapes=[pltpu.SemaphoreType.DMA((2,))]),
        out_shape=jax.ShapeDtypeStruct(dst_init.shape, dst_init.dtype),
        input_output_aliases={2: 0},
        compiler_params=_cparams("arbitrary"),
        name="gather_rows" if gather else "scatter_rows",
    )(idx, src, dst_init)


def _moe_body(e_lo_ref, e_hi_ref, blk_ref, used_ref, x_ref, meta_ref, mod_ref, wg_lo, wu_lo, wd_lo, wg_hi, wu_hi, wd_hi,
              lg_ref, lb_ref, o_ref):
    del e_lo_ref, e_hi_ref, blk_ref

    @pl.when(pl.program_id(0) >= used_ref[0])
    def _():
        o_ref[...] = jnp.zeros_like(o_ref)

    @pl.when(pl.program_id(0) < used_ref[0])
    def _():
        x = x_ref[...]
        meta = meta_ref[...]
        bid, g_lo, g_hi = meta[:, 0:1], meta[:, 1:2], meta[:, 2:3]
        nb = mod_ref.shape[0]

        def per_row(r):
            out = mod_ref[0, r:r + 1, :]
            for b in range(1, nb):
                out = jnp.where(bid == float(b), mod_ref[b, r:r + 1, :], out)
            return out

        h = (x * (1.0 + per_row(4)) + per_row(3)).astype(BF16)

        def expert(wg, wu, wd):
            he = _silu(_dot(h, wg[0])) * _dot(h, wu[0])
            return _dot(he.astype(BF16), wd[0])

        y = g_lo * expert(wg_lo, wu_lo, wd_lo) + g_hi * expert(wg_hi, wu_hi, wd_hi)
        o_ref[...] = _layer_norm(ALPHA * x + (1.0 + per_row(5)) * y, lg_ref[...], lb_ref[...])


def _moe_tiles(e_lo, e_hi, blk, used, xs, meta, mod6, w_gate, w_up, w_down, ln_g, ln_b):
    tp = xs.shape[0]
    tm = ROW_TILE
    row = lambda n: pl.BlockSpec((tm, n), lambda i, lo, hi, bk, us: (bk[i], 0))
    wspec = lambda shape, which: pl.BlockSpec(
        (1,) + shape, (lambda i, lo, hi, bk, us: (lo[i], 0, 0)) if which == 0 else (lambda i, lo, hi, bk, us: (hi[i], 0, 0)))
    full = lambda a: pl.BlockSpec(a.shape, lambda i, lo, hi, bk, us: (0,) * a.ndim)
    lg, lb = ln_g.reshape(1, D_MODEL), ln_b.reshape(1, D_MODEL)
    gu, dn = (D_MODEL, D_EXPERT), (D_EXPERT, D_MODEL)
    return pl.pallas_call(
        _moe_body,
        grid_spec=pltpu.PrefetchScalarGridSpec(
            num_scalar_prefetch=4,
            grid=(tp // tm,),
            in_specs=[row(D_MODEL), row(8), full(mod6),
                      wspec(gu, 0), wspec(gu, 0), wspec(dn, 0), wspec(gu, 1), wspec(gu, 1), wspec(dn, 1),
                      full(lg), full(lb)],
            out_specs=pl.BlockSpec((tm, D_MODEL), lambda i, lo, hi, bk, us: (i, 0))),
        out_shape=jax.ShapeDtypeStruct((tp, D_MODEL), F32),
        compiler_params=_cparams("arbitrary"),
        name="moe_tiles",
    )(e_lo, e_hi, blk, used, xs, meta, mod6, w_gate, w_up, w_down, w_gate, w_up, w_down, lg, lb)


def _moe_layer(x, mod6, router_w, router_bias, w_gate, w_up, w_down, ln_g, ln_b, bsz, seq):
    t = x.shape[0]
    tm = ROW_TILE
    n_tiles = t // tm + N_CLASSES
    tp = n_tiles * tm
    info, counts = _router(x, mod6, router_w, router_bias, seq)
    cls = info[0].astype(jnp.int32)
    rank = info[1].astype(jnp.int32)
    cnt = counts[:N_CLASSES, 0].astype(jnp.int32)
    tiles_c = (cnt + tm - 1) // tm
    tile_end = jnp.cumsum(tiles_c)
    offs = (tile_end - tiles_c) * tm
    onehot = cls[:, None] == jnp.arange(N_CLASSES)[None, :]
    dest = rank + jnp.sum(jnp.where(onehot, offs[None, :], 0), axis=1)
    used = tile_end[-1]
    tile_id = jnp.arange(n_tiles)
    blk = jnp.minimum(tile_id, used - 1)
    tile_cls = jnp.sum(blk[:, None] >= tile_end[None, :], axis=1)
    grp, pair = tile_cls // N_PAIRS, tile_cls % N_PAIRS
    e_lo = (grp * EPG + jnp.asarray(PAIR_LO, jnp.int32)[pair]).astype(jnp.int32)
    e_hi = (grp * EPG + jnp.asarray(PAIR_HI, jnp.int32)[pair]).astype(jnp.int32)
    bid = (jnp.arange(t) // seq).astype(F32)
    meta_nat = jnp.stack([bid, info[2], info[3]] + [jnp.zeros((t,), F32)] * 5, axis=1)
    meta = jnp.zeros((tp, 8), F32).at[dest].set(meta_nat)
    xs = _permute_rows(dest, x.reshape(t, 8, 128), jnp.zeros((tp, 8, 128), F32), gather=False)
    ys = _moe_tiles(e_lo, e_hi, blk.astype(jnp.int32), used.reshape(1).astype(jnp.int32), xs.reshape(tp, D_MODEL), meta,
                    mod6, w_gate, w_up, w_down, ln_g, ln_b)
    out = _permute_rows(dest, ys.reshape(tp, 8, 128), jnp.zeros((t, 8, 128), F32), gather=True)
    return out.reshape(t, D_MODEL)


def kernel(x, c, router_w, router_bias, ada_w, ada_b, ln1_g, ln1_b, ln2_g, ln2_b, moe_w_gate, moe_w_up, moe_w_down,
           ab_w_in, s5_lam_re, s5_lam_im, s5_log_dt, s5_b_re, s5_b_im, s5_c_re, s5_c_im, s5_d, s5_glu_w, s5_glu_b,
           gla_gate_w, gla_gate_b, gla_norm_g, ab_w_out, ml_w_in, ml_conv_w, ml_conv_b, ml_igate_b, ml_fgate_b,
           ml_norm_g, ml_w_out):
    bsz, seq, d = x.shape
    t = bsz * seq
    mod = _ada_mod(c, ada_w, ada_b)
    xt = x.reshape(t, d)
    for layer in range(DEPTH):
        j = layer // 2
        mod6 = mod[layer]
        if layer % 2 == 0:
            u, q, k, v, r, la = _ab_in(xt, mod6, ab_w_in[j], gla_gate_w[j], gla_gate_b[j], seq)
            tables = _s5_prep(s5_lam_re[j], s5_lam_im[j], s5_log_dt[j], s5_b_re[j], s5_b_im[j], s5_c_re[j], s5_c_im[j])
            ys = _s5_conv(u, tables, bsz, seq)
            yb = _gla(q, k, la, v, r, gla_norm_g[j], bsz, seq)
            w_out = ab_w_out[j].astype(BF16)
            consts = [s5_d[j].reshape(1, S5_W), s5_glu_w[j].astype(BF16), s5_glu_b[j].reshape(1, S5_W),
                      w_out[:S5_W], w_out[S5_W:]]
            xt = _out_call(_ab_out_body, "ab_out", [ys, u, yb], xt, mod6, consts, ln1_g[layer], ln1_b[layer], seq)
        else:
            qk, v, og, gates = _ml_in(xt, mod6, ml_w_in[j], ml_igate_b[j], ml_fgate_b[j], seq)
            y = _mlstm(qk, v, og, gates, ml_conv_w[j], ml_conv_b[j], ml_norm_g[j], bsz, seq)
            xt = _out_call(_ml_out_body, "ml_out", [y], xt, mod6, [ml_w_out[j].astype(BF16)],
                           ln1_g[layer], ln1_b[layer], seq)
        xt = _moe_layer(xt, mod6, router_w, router_bias, moe_w_gate[layer].astype(BF16), moe_w_up[layer].astype(BF16),
                        moe_w_down[layer].astype(BF16), ln2_g[layer], ln2_b[layer], bsz, seq)
    return xt.reshape(bsz, seq, d)
```

```python
import functools
import math

import jax
import jax.numpy as jnp
from jax import lax
from jax.experimental import pallas as pl
from jax.experimental.pallas import tpu as pltpu

F32 = jnp.float32
BF16 = jnp.bfloat16

D_MODEL = 1024
DEPTH = 4
S5_W = 512
S5_GROUP = 16
S5_GROUPS = 32
S5_STATE = 64
S5_CHUNK = 16
GLA_HEADS = 4
GLA_DV = 128
GLA_DK = 64
GLA_QK = 256
GLA_V = 512
GLA_RANK = 16
GLA_GATE_NORM = 16.0
GLA_SUB = 16
AB_IN = S5_W + 2 * GLA_QK + 2 * GLA_V + GLA_RANK
AB_IN_PAD = 2176
ML_HEADS = 8
ML_W = 1024
ML_DH = 128
ML_IN = 4 * ML_W + 2 * ML_HEADS
ML_IN_PAD = 4224
CONV_W = 4
CHUNK = 64
N_EXPERTS = 16
N_GROUPS = 4
EPG = 4
D_EXPERT = 256
N_PAIRS = 6
N_CLASSES = N_GROUPS * N_PAIRS
PAIR_LO = (0, 0, 0, 1, 1, 2)
PAIR_HI = (1, 2, 3, 2, 3, 3)
ALPHA = (2.0 * DEPTH) ** 0.25
LN_EPS = 1e-5
RMS_EPS = 1e-6

V7X_VMEM_BYTES = 64 * 1024 * 1024
VMEM_LIMIT = (V7X_VMEM_BYTES * 3) // 4
ROW_TILE = 256
SEQ_TILE = 256
ROUTE_TILE = 512
PERM_TILE = 512


def _cparams(*sem):
    return pltpu.CompilerParams(dimension_semantics=sem, vmem_limit_bytes=VMEM_LIMIT)


def _sigmoid(x):
    return 1.0 / (1.0 + jnp.exp(-x))


def _silu(x):
    return x * _sigmoid(x)


def _log_sigmoid(z):
    return jnp.minimum(z, 0.0) - jnp.log1p(jnp.exp(-jnp.abs(z)))


def _gelu_tanh(x):
    return 0.5 * x * (1.0 + jnp.tanh(math.sqrt(2.0 / math.pi) * (x + 0.044715 * (x * x * x))))


def _layer_norm(x, g, b):
    mu = jnp.mean(x, axis=-1, keepdims=True)
    xc = x - mu
    var = jnp.mean(xc * xc, axis=-1, keepdims=True)
    return xc * lax.rsqrt(var + LN_EPS) * g + b


def _split3_bf16(x):
    hi = x.astype(BF16)
    r = x - hi.astype(F32)
    mid = r.astype(BF16)
    lo = (r - mid.astype(F32)).astype(BF16)
    return hi, mid, lo


def _dot(a, b):
    return jnp.dot(a, b, preferred_element_type=F32)


def _dot_nt(a, b):
    return lax.dot_general(a, b, (((1,), (1,)), ((), ())), preferred_element_type=F32)


def _dot_tn(a, b):
    return lax.dot_general(a, b, (((0,), (0,)), ((), ())), preferred_element_type=F32)


def _ada_body(c_ref, w_ref, b_ref, o_ref):
    c = c_ref[...]
    o_ref[0] = _dot(_silu(c), w_ref[0]) + b_ref[0]


def _ada_mod(c, ada_w, ada_b):
    bsz = c.shape[0]
    tn = 1536
    cp = jnp.zeros((8, D_MODEL), F32).at[:bsz].set(c)
    out = pl.pallas_call(
        _ada_body,
        grid=(DEPTH, 6 * D_MODEL // tn),
        in_specs=[pl.BlockSpec((8, D_MODEL), lambda l, j: (0, 0)),
                  pl.BlockSpec((1, D_MODEL, tn), lambda l, j: (l, 0, j)),
                  pl.BlockSpec((1, 1, tn), lambda l, j: (l, 0, j))],
        out_specs=pl.BlockSpec((1, 8, tn), lambda l, j: (l, 0, j)),
        out_shape=jax.ShapeDtypeStruct((DEPTH, 8, 6 * D_MODEL), F32),
        compiler_params=_cparams("parallel", "parallel"),
        name="ada_mod",
    )(cp, ada_w, ada_b.reshape(DEPTH, 1, 6 * D_MODEL))
    return out[:, :bsz].reshape(DEPTH, bsz, 6, D_MODEL)


def _ab_in_body(x_ref, mod_ref, w_ref, gw_ref, gb_ref, u_ref, q_ref, k_ref, v_ref, r_ref, la_ref):
    x = x_ref[...]
    h = (x * (1.0 + mod_ref[0, 1:2, :]) + mod_ref[0, 0:1, :]).astype(BF16)

    def seg(a, b):
        return _dot(h, w_ref[:, a:b])

    u_ref[...] = seg(0, 512)
    q_ref[...] = seg(512, 768)
    k_ref[...] = seg(768, 1024)
    v_ref[...] = seg(1024, 1536)
    r_ref[...] = seg(1536, 2048)
    a_lr = seg(2048, AB_IN_PAD)
    z = _dot(a_lr.astype(BF16), gw_ref[...]) + gb_ref[...]
    la_ref[...] = _log_sigmoid(z) * (1.0 / GLA_GATE_NORM)


def _ab_in(x, mod6, w_in, gate_w, gate_b, seq):
    t = x.shape[0]
    tm = ROW_TILE
    per_b = seq // tm
    w = jnp.zeros((D_MODEL, AB_IN_PAD), BF16).at[:, :AB_IN].set(w_in.astype(BF16))
    gw = jnp.zeros((AB_IN_PAD - 2048, GLA_QK), BF16).at[:GLA_RANK].set(gate_w.astype(BF16))
    row = lambda n: pl.BlockSpec((tm, n), lambda i: (i, 0))
    full = lambda a: pl.BlockSpec(a.shape, lambda i: (0,) * a.ndim)
    outs = [S5_W, GLA_QK, GLA_QK, GLA_V, GLA_V, GLA_QK]
    return pl.pallas_call(
        _ab_in_body,
        grid=(t // tm,),
        in_specs=[row(D_MODEL),
                  pl.BlockSpec((1, 6, D_MODEL), lambda i: (i // per_b, 0, 0)),
                  full(w), full(gw), pl.BlockSpec((1, GLA_QK), lambda i: (0, 0))],
        out_specs=[row(n) for n in outs],
        out_shape=[jax.ShapeDtypeStruct((t, n), F32) for n in outs],
        compiler_params=_cparams("parallel"),
        name="ab_in",
    )(x, mod6, w, gw, gate_b.reshape(1, GLA_QK))


def _s5_prep(lam_re, lam_im, log_dt, b_re, b_im, c_re, c_im):
    hp = lax.Precision.HIGHEST
    n = S5_CHUNK
    lre, lim = lam_re.astype(F32), lam_im.astype(F32)
    dt = jnp.exp(log_dt.astype(F32))[:, None]
    d = jnp.arange(n + 1, dtype=F32)[:, None, None]
    mag = jnp.exp(lre * dt * d)
    ang = lim * dt * d
    pw_re, pw_im = mag * jnp.cos(ang), mag * jnp.sin(ang)
    den = lre * lre + lim * lim
    nre, nim = pw_re[1] - 1.0, pw_im[1]
    coef_re = (nre * lre + nim * lim) / den
    coef_im = (nim * lre - nre * lim) / den
    bre, bim = b_re.astype(F32), b_im.astype(F32)
    bb_re = coef_re[..., None] * bre - coef_im[..., None] * bim
    bb_im = coef_re[..., None] * bim + coef_im[..., None] * bre
    cre, cim = c_re.astype(F32), c_im.astype(F32)
    cp_re = cre[None] * pw_re[:, :, None, :] - cim[None] * pw_im[:, :, None, :]
    cp_im = cre[None] * pw_im[:, :, None, :] + cim[None] * pw_re[:, :, None, :]
    kern = (jnp.einsum('dgcp,gpe->dgce', cp_re, bb_re, precision=hp)
            - jnp.einsum('dgcp,gpe->dgce', cp_im, bb_im, precision=hp))
    jj = jnp.arange(n)[:, None]
    tt = jnp.arange(n)[None, :]
    lag = tt - jj
    kg = jnp.where((lag >= 0)[:, :, None, None, None], kern[jnp.clip(lag, 0, n)], 0.0)
    m_intra = jnp.transpose(kg, (2, 0, 4, 1, 3)).reshape(S5_GROUPS, n * S5_GROUP, n * S5_GROUP)
    pws_re, pws_im = pw_re[n - 1 - jnp.arange(n)], pw_im[n - 1 - jnp.arange(n)]
    ws_re = pws_re[..., None] * bb_re[None] - pws_im[..., None] * bb_im[None]
    ws_im = pws_re[..., None] * bb_im[None] + pws_im[..., None] * bb_re[None]
    w_state = jnp.transpose(jnp.concatenate([ws_re, ws_im], axis=2), (1, 0, 3, 2))
    w_state = w_state.reshape(S5_GROUPS, n * S5_GROUP, 2 * S5_STATE)
    wi_re = jnp.transpose(cp_re[1:], (1, 3, 0, 2))
    wi_im = -jnp.transpose(cp_im[1:], (1, 3, 0, 2))
    w_inter = jnp.concatenate([wi_re, wi_im], axis=1).reshape(S5_GROUPS, 2 * S5_STATE, n * S5_GROUP)
    a_chunk = jnp.stack([pw_re[n], pw_im[n]], axis=1)
    return m_intra.astype(BF16), w_state.astype(BF16), w_inter.astype(BF16), a_chunk


def _s5_body(u_ref, m_ref, ws_ref, wi_ref, a_ref, y_ref, ere, eim, sre, sim):
    u = u_ref[0]
    e = _dot(u, ws_ref[0])
    ere[...] = e[:, :S5_STATE]
    eim[...] = e[:, S5_STATE:]
    ar = jnp.broadcast_to(a_ref[0, 0:1, :], (8, S5_STATE))
    ai = jnp.broadcast_to(a_ref[0, 1:2, :], (8, S5_STATE))
    lo = lax.broadcasted_iota(jnp.int32, (8, S5_STATE), 0) < 4
    n_tiles = ere.shape[0] // 8

    def tile(i, carry):
        c_r, c_i = carry
        r0 = pl.multiple_of(i * 8, 8)
        e_r, e_i = ere[pl.ds(r0, 8), :], eim[pl.ds(r0, 8), :]
        e_r4, e_i4 = pltpu.roll(e_r, 4, 0), pltpu.roll(e_i, 4, 0)
        t_r = ar * c_r - ai * c_i + e_r4
        t_i = ar * c_i + ai * c_r + e_i4
        sp_r, sp_i = jnp.where(lo, c_r, t_r), jnp.where(lo, c_i, t_i)
        sre[pl.ds(r0, 8), :] = sp_r
        sim[pl.ds(r0, 8), :] = sp_i
        n_r = ar * sp_r - ai * sp_i + e_r
        n_i = ar * sp_i + ai * sp_r + e_i
        return (jnp.where(lo, pltpu.roll(n_r, 4, 0), n_r), jnp.where(lo, pltpu.roll(n_i, 4, 0), n_i))

    z = jnp.zeros((8, S5_STATE), F32)
    lax.fori_loop(0, n_tiles, tile, (z, z))
    y = _dot(u, m_ref[0])
    y = y + _dot(sre[...].astype(BF16), wi_ref[0, :S5_STATE, :])
    y = y + _dot(sim[...].astype(BF16), wi_ref[0, S5_STATE:, :])
    y_ref[0] = y


def _s5_conv(u, tables, bsz, seq):
    assert bsz == 4
    m_intra, w_state, w_inter, a_chunk = tables
    nk = seq // S5_CHUNK
    rows = nk * bsz
    lanes = S5_CHUNK * S5_GROUP
    uf = u.astype(BF16).reshape(bsz, nk, S5_CHUNK, S5_GROUPS, S5_GROUP)
    uf = jnp.transpose(uf, (3, 1, 0, 2, 4)).reshape(S5_GROUPS, rows, lanes)
    grp = lambda shape: pl.BlockSpec((1,) + shape, lambda g: (g, 0, 0))
    yf = pl.pallas_call(
        _s5_body,
        grid=(S5_GROUPS,),
        in_specs=[grp((rows, lanes)), grp((lanes, lanes)), grp((lanes, 2 * S5_STATE)),
                  grp((2 * S5_STATE, lanes)), grp((2, S5_STATE))],
        out_specs=grp((rows, lanes)),
        out_shape=jax.ShapeDtypeStruct((S5_GROUPS, rows, lanes), F32),
        scratch_shapes=[pltpu.VMEM((rows, S5_STATE), F32)] * 4,
        compiler_params=_cparams("parallel"),
        name="s5_conv",
    )(uf, m_intra, w_state, w_inter, a_chunk)
    yf = yf.reshape(S5_GROUPS, nk, bsz, S5_CHUNK, S5_GROUP)
    return jnp.transpose(yf, (2, 1, 3, 0, 4)).reshape(bsz * seq, S5_W)


def _gla_body(q_ref, k_ref, g_ref, v_ref, r_ref, ng_ref, y_ref, s_ref, bc_ref, o_ref):
    tl = q_ref.shape[0]
    scale = GLA_DK ** -0.5

    @pl.when(pl.program_id(1) == 0)
    def _():
        s_ref[...] = jnp.zeros_like(s_ref)

    row = lax.broadcasted_iota(jnp.int32, (CHUNK, CHUNK), 0)
    col = lax.broadcasted_iota(jnp.int32, (CHUNK, CHUNK), 1)
    tri = col <= row
    tref = col < (row // GLA_SUB) * GLA_SUB
    cum_mat = jnp.concatenate([tri, tref], axis=0).astype(BF16)
    lane_head = lax.broadcasted_iota(jnp.int32, (GLA_SUB, GLA_QK), 1) // GLA_DK
    bd_mask = (lax.broadcasted_iota(jnp.int32, (GLA_QK, GLA_V), 0) // GLA_DK
               == lax.broadcasted_iota(jnp.int32, (GLA_QK, GLA_V), 1) // GLA_DV)
    ones_bd = bd_mask.astype(BF16)
    ones_v = jnp.ones((CHUNK, GLA_V), BF16)

    def chunk(c, carry):
        r0 = pl.multiple_of(c * CHUNK, CHUNK)
        qc = q_ref[pl.ds(r0, CHUNK), :] * scale
        kc = k_ref[pl.ds(r0, CHUNK), :]
        g_parts = _split3_bf16(g_ref[pl.ds(r0, CHUNK), :])
        vb = v_ref[pl.ds(r0, CHUNK), :].astype(BF16)
        br = sum(_dot(cum_mat, part) for part in g_parts)
        bc, ref = br[:CHUNK], br[CHUNK:]
        bc_ref[pl.ds(r0, CHUNK), :] = bc
        qt = qc * jnp.exp(bc - ref)
        state = s_ref[...]
        o = _dot((qt * jnp.exp(ref)).astype(BF16), state.astype(BF16))
        blocks = [o[:GLA_SUB]]
        for i in range(1, CHUNK // GLA_SUB):
            ri = bc[GLA_SUB * i - 1:GLA_SUB * i, :]
            kt = (kc * jnp.exp(jnp.minimum(ri - bc, 0.0))).astype(BF16)
            qi = qt[GLA_SUB * i:GLA_SUB * (i + 1)]
            lhs = jnp.concatenate([jnp.where(lane_head == h, qi, 0.0) for h in range(GLA_HEADS)], axis=0)
            att = _dot_nt(lhs.astype(BF16), kt)
            att = jnp.where(col < GLA_SUB * i, att, 0.0)
            ov = _dot(att.astype(BF16), vb)
            oi = jnp.concatenate([ov[GLA_SUB * h:GLA_SUB * (h + 1), GLA_DV * h:GLA_DV * (h + 1)]
                                  for h in range(GLA_HEADS)], axis=1)
            blocks.append(o[GLA_SUB * i:GLA_SUB * (i + 1)] + oi)
        o_ref[pl.ds(r0, CHUNK), :] = jnp.concatenate(blocks, axis=0)
        bl = bc[CHUNK - 1:CHUNK, :]
        kh = (kc * jnp.exp(bl - bc)).astype(BF16)
        upd = _dot_tn(kh, vb)
        dcol = sum(_dot_tn(part, ones_v) for part in g_parts)
        s_ref[...] = jnp.exp(dcol) * state + jnp.where(bd_mask, upd, 0.0)
        return carry

    lax.fori_loop(0, tl // CHUNK, chunk, 0)

    q = q_ref[...] * scale
    k = k_ref[...]
    v = v_ref[...]
    bc = bc_ref[...]
    rmod = lax.broadcasted_iota(jnp.int32, (tl, 1), 0) % GLA_SUB
    od = jnp.zeros((tl, GLA_V), F32)
    for d in range(GLA_SUB):
        kd = k if d == 0 else pltpu.roll(k, d, 0)
        bd = bc if d == 0 else pltpu.roll(bc, d, 0)
        vd = v if d == 0 else pltpu.roll(v, d, 0)
        valid = rmod >= d
        e = jnp.exp(jnp.where(valid, bc - bd, 0.0))
        p = jnp.where(valid, q * kd * e, 0.0).astype(BF16)
        od = od + _dot(p, ones_bd) * vd
    o = o_ref[...] + od
    ng = ng_ref[...]
    outs = []
    for h in range(GLA_HEADS):
        oh = o[:, GLA_DV * h:GLA_DV * (h + 1)]
        outs.append(oh * lax.rsqrt(jnp.mean(oh * oh, axis=-1, keepdims=True) + RMS_EPS))
    y_ref[...] = jnp.concatenate(outs, axis=1) * ng * _silu(r_ref[...])


def _gla(q, k, la, v, r, norm_g, bsz, seq):
    tl = SEQ_TILE
    nl = seq // tl
    blk = lambda n: pl.BlockSpec((tl, n), lambda b, l: (b * nl + l, 0))
    return pl.pallas_call(
        _gla_body,
        grid=(bsz, nl),
        in_specs=[blk(GLA_QK), blk(GLA_QK), blk(GLA_QK), blk(GLA_V), blk(GLA_V),
                  pl.BlockSpec((1, GLA_V), lambda b, l: (0, 0))],
        out_specs=blk(GLA_V),
        out_shape=jax.ShapeDtypeStruct((bsz * seq, GLA_V), F32),
        scratch_shapes=[pltpu.VMEM((GLA_QK, GLA_V), F32), pltpu.VMEM((tl, GLA_QK), F32),
                        pltpu.VMEM((tl, GLA_V), F32)],
        compiler_params=_cparams("parallel", "arbitrary"),
        name="gla",
    )(q, k, la, v, r, norm_g.reshape(1, GLA_V))


def _residual_ln(x, y, mod_ref, gate_row, lg_ref, lb_ref):
    return _layer_norm(ALPHA * x + (1.0 + mod_ref[0, gate_row:gate_row + 1, :]) * y, lg_ref[...], lb_ref[...])


def _ab_out_body(ys_ref, u_ref, yb_ref, x_ref, mod_ref, d_ref, gw_ref, gb_ref, wa_ref, wb_ref, lg_ref, lb_ref, o_ref):
    z = _gelu_tanh(ys_ref[...] + d_ref[...] * u_ref[...])
    ya = z * _sigmoid(_dot(z.astype(BF16), gw_ref[...]) + gb_ref[...])
    y = _dot(ya.astype(BF16), wa_ref[...]) + _dot(yb_ref[...].astype(BF16), wb_ref[...])
    o_ref[...] = _residual_ln(x_ref[...], y, mod_ref, 2, lg_ref, lb_ref)


def _ml_out_body(y_ref, x_ref, mod_ref, w_ref, lg_ref, lb_ref, o_ref):
    y = _dot(y_ref[...].astype(BF16), w_ref[...])
    o_ref[...] = _residual_ln(x_ref[...], y, mod_ref, 2, lg_ref, lb_ref)


def _out_call(body, name, row_inputs, x, mod6, consts, ln_g, ln_b, seq):
    t = x.shape[0]
    tm = ROW_TILE
    per_b = seq // tm
    row = lambda a: pl.BlockSpec((tm, a.shape[1]), lambda i: (i, 0))
    full = lambda a: pl.BlockSpec(a.shape, lambda i: (0,) * a.ndim)
    lg, lb = ln_g.reshape(1, D_MODEL), ln_b.reshape(1, D_MODEL)
    return pl.pallas_call(
        body,
        grid=(t // tm,),
        in_specs=([row(a) for a in row_inputs] + [row(x), pl.BlockSpec((1, 6, D_MODEL), lambda i: (i // per_b, 0, 0))]
                  + [full(a) for a in consts] + [full(lg), full(lb)]),
        out_specs=row(x),
        out_shape=jax.ShapeDtypeStruct((t, D_MODEL), F32),
        compiler_params=_cparams("parallel"),
        name=name,
    )(*row_inputs, x, mod6, *consts, lg, lb)


def _ml_in_body(x_ref, mod_ref, w_ref, gb_ref, qk_ref, v_ref, o_ref, gt_ref):
    x = x_ref[...]
    h = (x * (1.0 + mod_ref[0, 1:2, :]) + mod_ref[0, 0:1, :]).astype(BF16)
    qk_ref[...] = _dot(h, w_ref[:, 0:2 * ML_W])
    v_ref[...] = _dot(h, w_ref[:, 2 * ML_W:3 * ML_W])
    o_ref[...] = _dot(h, w_ref[:, 3 * ML_W:4 * ML_W])
    gt_ref[...] = _dot(h, w_ref[:, 4 * ML_W:ML_IN_PAD]) + gb_ref[...]


def _ml_in(x, mod6, w_in, igate_b, fgate_b, seq):
    t = x.shape[0]
    tm = ROW_TILE
    per_b = seq // tm
    w = jnp.zeros((D_MODEL, ML_IN_PAD), BF16).at[:, :ML_IN].set(w_in.astype(BF16))
    gb = jnp.zeros((1, ML_IN_PAD - 4 * ML_W), F32).at[0, :2 * ML_HEADS].set(jnp.concatenate([igate_b, fgate_b]))
    row = lambda n: pl.BlockSpec((tm, n), lambda i: (i, 0))
    full = lambda a: pl.BlockSpec(a.shape, lambda i: (0,) * a.ndim)
    outs = [2 * ML_W, ML_W, ML_W, ML_IN_PAD - 4 * ML_W]
    return pl.pallas_call(
        _ml_in_body,
        grid=(t // tm,),
        in_specs=[row(D_MODEL), pl.BlockSpec((1, 6, D_MODEL), lambda i: (i // per_b, 0, 0)), full(w), full(gb)],
        out_specs=[row(n) for n in outs],
        out_shape=[jax.ShapeDtypeStruct((t, n), F32) for n in outs],
        compiler_params=_cparams("parallel"),
        name="ml_in",
    )(x, mod6, w, gb)


def _mlstm_body(q_ref, k_ref, v_ref, og_ref, gt_ref, wq_ref, wk_ref, bq_ref, bk_ref, ng_ref,
                y_ref, s_ref, n_ref, m_ref, cq_ref, ck_ref, qs_ref, ks_ref, h_ref):
    tl = q_ref.shape[0]

    @pl.when(pl.program_id(1) == 0)
    def _():
        s_ref[...] = jnp.zeros_like(s_ref)
        n_ref[...] = jnp.zeros_like(n_ref)
        m_ref[...] = jnp.zeros_like(m_ref)
        cq_ref[...] = jnp.zeros_like(cq_ref)
        ck_ref[...] = jnp.zeros_like(ck_ref)

    def conv(x_ref, carry_ref, w_ref, b_ref, out_ref, scale):
        for h in range(ML_HEADS):
            cols = slice(ML_DH * h, ML_DH * (h + 1))
            x = x_ref[:, cols]
            xin = jnp.concatenate([carry_ref[:, cols], x], axis=0)
            acc = jnp.zeros((tl, ML_DH), F32) + b_ref[:, cols]
            for i in range(CONV_W):
                sh = CONV_W - 1 - i
                xs = xin if sh == 0 else pltpu.roll(xin, sh, 0)
                acc = acc + w_ref[i:i + 1, cols] * xs[8:8 + tl]
            carry_ref[:, cols] = x[tl - 8:tl]
            out_ref[:, cols] = _silu(acc) * scale

    conv(q_ref, cq_ref, wq_ref, bq_ref, qs_ref, 1.0)
    conv(k_ref, ck_ref, wk_ref, bk_ref, ks_ref, ML_DH ** -0.5)

    row = lax.broadcasted_iota(jnp.int32, (CHUNK, CHUNK), 0)
    col = lax.broadcasted_iota(jnp.int32, (CHUNK, CHUNK), 1)
    causal = col <= row
    tri = causal.astype(BF16)
    tri_t = (row <= col).astype(BF16)

    def chunk(c, carry):
        r0 = pl.multiple_of(c * CHUNK, CHUNK)
        gt = gt_ref[pl.ds(r0, CHUNK), :]
        gt_t = gt.T
        lf_c = _log_sigmoid(gt)
        lf_r = _log_sigmoid(gt_t[ML_HEADS:2 * ML_HEADS])
        b_c_all = sum(_dot(tri, part) for part in _split3_bf16(lf_c))
        b_r_all = sum(_dot(part, tri_t) for part in _split3_bf16(lf_r))
        for h in range(ML_HEADS):
            cols = slice(ML_DH * h, ML_DH * (h + 1))
            qc = qs_ref[pl.ds(r0, CHUNK), cols]
            kc = ks_ref[pl.ds(r0, CHUNK), cols]
            vb = v_ref[pl.ds(r0, CHUNK), cols].astype(BF16)
            i_c, i_r = gt[:, h:h + 1], gt_t[h:h + 1, :]
            b_c, b_r = b_c_all[:, ML_HEADS + h:ML_HEADS + h + 1], b_r_all[h:h + 1, :]
            m = m_ref[h:h + 1, 0:1]
            dmat = jnp.where(causal, b_c - b_r + i_r, -jnp.inf)
            inter_log = b_c + m
            m_t = jnp.maximum(inter_log, jnp.max(dmat, axis=1, keepdims=True))
            w_intra = jnp.exp(dmat - m_t)
            w_inter = jnp.exp(inter_log - m_t)
            qb = qc.astype(BF16)
            s = _dot_nt(qb, kc.astype(BF16)) * w_intra
            state, nrm = s_ref[h], n_ref[h:h + 1, :]
            num = w_inter * _dot(qb, state.astype(BF16)) + _dot(s.astype(BF16), vb)
            den = w_inter * jnp.sum(qc * nrm, axis=1, keepdims=True) + jnp.sum(s, axis=1, keepdims=True)
            h_ref[pl.ds(r0, CHUNK), cols] = num / jnp.maximum(jnp.abs(den), jnp.exp(-m_t))
            b_last = b_c[CHUNK - 1:CHUNK, :]
            gs_c = b_last - b_c + i_c
            gs_r = b_last - b_r + i_r
            m_new = jnp.maximum(b_last + m, jnp.max(gs_r, axis=1, keepdims=True))
            dec = jnp.exp(b_last + m - m_new)
            kw = kc * jnp.exp(gs_c - m_new)
            s_ref[h] = dec * state + _dot_tn(kw.astype(BF16), vb)
            n_ref[h:h + 1, :] = dec * nrm + jnp.sum(kw, axis=0, keepdims=True)
            m_ref[h:h + 1, :] = jnp.broadcast_to(m_new, (1, ML_DH))
        return carry

    lax.fori_loop(0, tl // CHUNK, chunk, 0)
    for h in range(ML_HEADS):
        cols = slice(ML_DH * h, ML_DH * (h + 1))
        hh = h_ref[:, cols]
        hh = hh * lax.rsqrt(jnp.mean(hh * hh, axis=-1, keepdims=True) + RMS_EPS)
        y_ref[:, cols] = _sigmoid(og_ref[:, cols]) * (hh * ng_ref[:, cols])


def _mlstm(qk, v, og, gates, conv_w, conv_b, norm_g, bsz, seq):
    tl = SEQ_TILE
    nl = seq // tl
    blk = lambda n, off: pl.BlockSpec((tl, n), lambda b, l: (b * nl + l, off))
    par = lambda rows, off: pl.BlockSpec((rows, ML_W), lambda b, l: (0, off))
    return pl.pallas_call(
        _mlstm_body,
        grid=(bsz, nl),
        in_specs=[blk(ML_W, 0), blk(ML_W, 1), blk(ML_W, 0), blk(ML_W, 0), blk(gates.shape[1], 0),
                  par(CONV_W, 0), par(CONV_W, 1), par(1, 0), par(1, 1), par(1, 0)],
        out_specs=blk(ML_W, 0),
        out_shape=jax.ShapeDtypeStruct((bsz * seq, ML_W), F32),
        scratch_shapes=[pltpu.VMEM((ML_HEADS, ML_DH, ML_DH), F32), pltpu.VMEM((ML_HEADS, ML_DH), F32),
                        pltpu.VMEM((ML_HEADS, ML_DH), F32),
                        pltpu.VMEM((8, ML_W), F32), pltpu.VMEM((8, ML_W), F32),
                        pltpu.VMEM((tl, ML_W), F32), pltpu.VMEM((tl, ML_W), F32), pltpu.VMEM((tl, ML_W), F32)],
        compiler_params=_cparams("parallel", "arbitrary"),
        name="mlstm",
    )(qk, qk, v, og, gates, conv_w, conv_w, conv_b.reshape(1, -1), conv_b.reshape(1, -1), norm_g.reshape(1, ML_W))


def _router_body(x_ref, mod_ref, rw_ref, rb_ref, info_ref, cnt_ref, carry_ref):
    tr = x_ref.shape[0]

    @pl.when(pl.program_id(0) == 0)
    def _():
        carry_ref[...] = jnp.zeros_like(carry_ref)

    h = x_ref[...] * (1.0 + mod_ref[0, 4:5, :]) + mod_ref[0, 3:4, :]
    logits = lax.dot_general(rw_ref[...], h, (((1,), (1,)), ((), ())), precision=lax.Precision.HIGHEST,
                             preferred_element_type=F32)
    aff = _sigmoid(logits)
    sel = aff + rb_ref[...]
    s = [sel[e:e + 1, :] for e in range(N_EXPERTS)]
    a = [aff[e:e + 1, :] for e in range(N_EXPERTS)]

    def top2_sum(v):
        best = v[0] + v[1]
        for lo_i, hi_i in zip(PAIR_LO[1:], PAIR_HI[1:]):
            best = jnp.maximum(best, v[lo_i] + v[hi_i])
        return best

    score = [top2_sum(s[EPG * g:EPG * (g + 1)]) for g in range(N_GROUPS)]
    g_idx = jnp.zeros((1, tr), jnp.int32)
    best = score[0]
    for g in range(1, N_GROUPS):
        upd = score[g] > best
        g_idx = jnp.where(upd, g, g_idx)
        best = jnp.where(upd, score[g], best)

    def pick_group(rows, j):
        out = rows[j]
        for g in range(1, N_GROUPS):
            out = jnp.where(g_idx == g, rows[EPG * g + j], out)
        return out

    sg = [pick_group(s, j) for j in range(EPG)]
    ag = [pick_group(a, j) for j in range(EPG)]
    first = jnp.zeros((1, tr), jnp.int32)
    best = sg[0]
    for j in range(1, EPG):
        upd = sg[j] > best
        first = jnp.where(upd, j, first)
        best = jnp.where(upd, sg[j], best)
    second = jnp.zeros((1, tr), jnp.int32)
    best2 = jnp.full((1, tr), -jnp.inf, F32)
    for j in range(EPG):
        upd = (first != j) & (sg[j] > best2)
        second = jnp.where(upd, j, second)
        best2 = jnp.where(upd, sg[j], best2)
    lo = jnp.minimum(first, second)
    hi = jnp.maximum(first, second)
    pair = jnp.where(lo == 0, hi - 1, jnp.where(lo == 1, hi + 1, N_PAIRS - 1))
    cls = g_idx * N_PAIRS + pair

    def pick_local(rows, idx):
        out = rows[0]
        for j in range(1, EPG):
            out = jnp.where(idx == j, rows[j], out)
        return out

    a_lo, a_hi = pick_local(ag, lo), pick_local(ag, hi)
    tot = a_lo + a_hi
    onehot = (lax.broadcasted_iota(jnp.int32, (32, tr), 0) == cls).astype(F32)
    before = (lax.broadcasted_iota(jnp.int32, (tr, tr), 0) < lax.broadcasted_iota(jnp.int32, (tr, tr), 1)).astype(BF16)
    cum = _dot(onehot.astype(BF16), before) + carry_ref[:, 0:1]
    rank = jnp.sum(onehot * cum, axis=0, keepdims=True)
    carry_ref[...] = carry_ref[...] + jnp.sum(onehot, axis=1, keepdims=True)
    cnt_ref[...] = carry_ref[...]
    zero = jnp.zeros((1, tr), F32)
    info_ref[...] = jnp.concatenate([cls.astype(F32), rank, a_lo / tot, a_hi / tot, zero, zero, zero, zero], axis=0)


def _router(x, mod6, router_w, router_bias, seq):
    t = x.shape[0]
    tr = ROUTE_TILE
    per_b = seq // tr
    return pl.pallas_call(
        _router_body,
        grid=(t // tr,),
        in_specs=[pl.BlockSpec((tr, D_MODEL), lambda i: (i, 0)),
                  pl.BlockSpec((1, 6, D_MODEL), lambda i: (i // per_b, 0, 0)),
                  pl.BlockSpec((N_EXPERTS, D_MODEL), lambda i: (0, 0)),
                  pl.BlockSpec((N_EXPERTS, 1), lambda i: (0, 0))],
        out_specs=[pl.BlockSpec((8, tr), lambda i: (0, i)), pl.BlockSpec((32, 128), lambda i: (0, 0))],
        out_shape=[jax.ShapeDtypeStruct((8, t), F32), jax.ShapeDtypeStruct((32, 128), F32)],
        scratch_shapes=[pltpu.VMEM((32, 128), F32)],
        compiler_params=_cparams("arbitrary"),
        name="router",
    )(x, mod6, router_w.T.astype(F32), router_bias.reshape(N_EXPERTS, 1).astype(F32))


def _to_row_tiles(x):
    return [x[:, 128 * j:128 * (j + 1)] for j in range(D_MODEL // 128)]


def _row_dma_loop(make_copy, n):
    def start(r, c):
        make_copy(r).start()
        return c

    def wait(r, c):
        make_copy(r).wait()
        return c

    lax.fori_loop(0, n, start, 0)
    lax.fori_loop(0, n, wait, 0)


def _scatter_body(idx_ref, x_ref, dst_in_ref, dst_ref, stage, sem):
    del dst_in_ref
    tm = x_ref.shape[0]
    base = pl.program_id(0) * tm
    for j, part in enumerate(_to_row_tiles(x_ref[...])):
        stage[:, j, :] = part
    _row_dma_loop(lambda r: pltpu.make_async_copy(stage.at[r], dst_ref.at[idx_ref[base + r]], sem.at[0]), tm)


def _scatter_rows(idx, x, n_dst):
    t = x.shape[0]
    tm = PERM_TILE
    dst0 = jnp.zeros((n_dst, 8, 128), F32)
    return pl.pallas_call(
        _scatter_body,
        grid_spec=pltpu.PrefetchScalarGridSpec(
            num_scalar_prefetch=1,
            grid=(t // tm,),
            in_specs=[pl.BlockSpec((tm, D_MODEL), lambda i, idx: (i, 0)), pl.BlockSpec(memory_space=pl.ANY)],
            out_specs=pl.BlockSpec(memory_space=pl.ANY),
            scratch_shapes=[pltpu.VMEM((tm, 8, 128), F32), pltpu.SemaphoreType.DMA((1,))]),
        out_shape=jax.ShapeDtypeStruct(dst0.shape, F32),
        input_output_aliases={2: 0},
        compiler_params=_cparams("arbitrary"),
        name="scatter_rows",
    )(idx, x, dst0)


def _gather_body(idx_ref, src_ref, o_ref, stage, sem):
    tm = o_ref.shape[0]
    base = pl.program_id(0) * tm
    _row_dma_loop(lambda r: pltpu.make_async_copy(src_ref.at[idx_ref[base + r]], stage.at[r], sem.at[0]), tm)
    o_ref[...] = jnp.concatenate([stage[:, j, :] for j in range(D_MODEL // 128)], axis=1)


def _gather_rows(idx, src):
    t = idx.shape[0]
    tm = PERM_TILE
    return pl.pallas_call(
        _gather_body,
        grid_spec=pltpu.PrefetchScalarGridSpec(
            num_scalar_prefetch=1,
            grid=(t // tm,),
            in_specs=[pl.BlockSpec(memory_space=pl.ANY)],
            out_specs=pl.BlockSpec((tm, D_MODEL), lambda i, idx: (i, 0)),
            scratch_shapes=[pltpu.VMEM((tm, 8, 128), F32), pltpu.SemaphoreType.DMA((1,))]),
        out_shape=jax.ShapeDtypeStruct((t, D_MODEL), F32),
        compiler_params=_cparams("arbitrary"),
        name="gather_rows",
    )(idx, src)


def _moe_body(e_lo_ref, e_hi_ref, blk_ref, new_ref, used_ref, x_ref, meta_ref, mod_ref,
              wg_lo32, wu_lo32, wd_lo32, wg_hi32, wu_hi32, wd_hi32, lg_ref, lb_ref, o_ref,
              wg_lo, wu_lo, wd_lo, wg_hi, wu_hi, wd_hi):
    del e_lo_ref, e_hi_ref, blk_ref
    i = pl.program_id(0)

    @pl.when(new_ref[i] == 1)
    def _():
        for src, dst in ((wg_lo32, wg_lo), (wu_lo32, wu_lo), (wd_lo32, wd_lo),
                         (wg_hi32, wg_hi), (wu_hi32, wu_hi), (wd_hi32, wd_hi)):
            dst[...] = src[...].astype(BF16)

    @pl.when(i >= used_ref[0])
    def _():
        o_ref[...] = jnp.zeros_like(o_ref)

    @pl.when(i < used_ref[0])
    def _():
        x = jnp.concatenate([x_ref[:, j, :] for j in range(D_MODEL // 128)], axis=1)
        meta = meta_ref[...]
        bid, g_lo, g_hi = meta[:, 0:1], meta[:, 1:2], meta[:, 2:3]
        nb = mod_ref.shape[0]

        def per_row(r):
            out = mod_ref[0, r:r + 1, :]
            for b in range(1, nb):
                out = jnp.where(bid == float(b), mod_ref[b, r:r + 1, :], out)
            return out

        h = (x * (1.0 + per_row(4)) + per_row(3)).astype(BF16)

        def expert(wg, wu, wd):
            he = _silu(_dot(h, wg[...])) * _dot(h, wu[...])
            return _dot(he.astype(BF16), wd[...])

        y = g_lo * expert(wg_lo, wu_lo, wd_lo) + g_hi * expert(wg_hi, wu_hi, wd_hi)
        out = _layer_norm(ALPHA * x + (1.0 + per_row(5)) * y, lg_ref[...], lb_ref[...])
        for j, part in enumerate(_to_row_tiles(out)):
            o_ref[:, j, :] = part


def _moe_tiles(e_lo, e_hi, blk, new, used, xs, meta, mod6, layer, w_gate, w_up, w_down, ln_g, ln_b):
    tp = xs.shape[0]
    tm = ROW_TILE
    row = lambda n: pl.BlockSpec((tm, n), lambda i, *s: (s[2][i], 0))
    tiles = lambda index: pl.BlockSpec((tm, 8, 128), index)
    wspec = lambda shape, which: pl.BlockSpec((None, None) + shape, lambda i, *s: (layer, s[which][i], 0, 0))
    full = lambda a: pl.BlockSpec(a.shape, lambda i, *s: (0,) * a.ndim)
    lg, lb = ln_g.reshape(1, D_MODEL), ln_b.reshape(1, D_MODEL)
    gu, dn = (D_MODEL, D_EXPERT), (D_EXPERT, D_MODEL)
    return pl.pallas_call(
        _moe_body,
        grid_spec=pltpu.PrefetchScalarGridSpec(
            num_scalar_prefetch=5,
            grid=(tp // tm,),
            in_specs=[tiles(lambda i, *s: (s[2][i], 0, 0)), row(8), full(mod6),
                      wspec(gu, 0), wspec(gu, 0), wspec(dn, 0), wspec(gu, 1), wspec(gu, 1), wspec(dn, 1),
                      full(lg), full(lb)],
            out_specs=tiles(lambda i, *s: (i, 0, 0)),
            scratch_shapes=[pltpu.VMEM(gu, BF16), pltpu.VMEM(gu, BF16), pltpu.VMEM(dn, BF16)] * 2),
        out_shape=jax.ShapeDtypeStruct((tp, 8, 128), F32),
        compiler_params=_cparams("arbitrary"),
        name="moe_tiles",
    )(e_lo, e_hi, blk, new, used, xs, meta, mod6, w_gate, w_up, w_down, w_gate, w_up, w_down, lg, lb)


def _moe_layer(x, mod6, router_w, router_bias, layer, w_gate, w_up, w_down, ln_g, ln_b, bsz, seq):
    t = x.shape[0]
    tm = ROW_TILE
    n_tiles = t // tm + N_CLASSES
    tp = n_tiles * tm
    info, counts = _router(x, mod6, router_w, router_bias, seq)
    cls = info[0].astype(jnp.int32)
    rank = info[1].astype(jnp.int32)
    cnt = counts[:N_CLASSES, 0].astype(jnp.int32)
    tiles_c = (cnt + tm - 1) // tm
    tile_end = jnp.cumsum(tiles_c)
    offs = (tile_end - tiles_c) * tm
    onehot = cls[:, None] == jnp.arange(N_CLASSES)[None, :]
    dest = rank + jnp.sum(jnp.where(onehot, offs[None, :], 0), axis=1)
    used = tile_end[-1]
    tile_id = jnp.arange(n_tiles)
    blk = jnp.minimum(tile_id, used - 1)
    tile_cls = jnp.sum(blk[:, None] >= tile_end[None, :], axis=1)
    grp, pair = tile_cls // N_PAIRS, tile_cls % N_PAIRS
    e_lo = (grp * EPG + jnp.asarray(PAIR_LO, jnp.int32)[pair]).astype(jnp.int32)
    e_hi = (grp * EPG + jnp.asarray(PAIR_HI, jnp.int32)[pair]).astype(jnp.int32)
    new = jnp.concatenate([jnp.ones((1,), jnp.int32), (tile_cls[1:] != tile_cls[:-1]).astype(jnp.int32)])
    bid = (jnp.arange(t) // seq).astype(F32)
    meta_nat = jnp.stack([bid, info[2], info[3]] + [jnp.zeros((t,), F32)] * 5, axis=1)
    meta = jnp.zeros((tp, 8), F32).at[dest].set(meta_nat)
    xs = _scatter_rows(dest, x, tp)
    ys = _moe_tiles(e_lo, e_hi, blk.astype(jnp.int32), new, used.reshape(1).astype(jnp.int32), xs, meta,
                    mod6, layer, w_gate, w_up, w_down, ln_g, ln_b)
    return _gather_rows(dest, ys)


def kernel(x, c, router_w, router_bias, ada_w, ada_b, ln1_g, ln1_b, ln2_g, ln2_b, moe_w_gate, moe_w_up, moe_w_down,
           ab_w_in, s5_lam_re, s5_lam_im, s5_log_dt, s5_b_re, s5_b_im, s5_c_re, s5_c_im, s5_d, s5_glu_w, s5_glu_b,
           gla_gate_w, gla_gate_b, gla_norm_g, ab_w_out, ml_w_in, ml_conv_w, ml_conv_b, ml_igate_b, ml_fgate_b,
           ml_norm_g, ml_w_out):
    bsz, seq, d = x.shape
    t = bsz * seq
    mod = _ada_mod(c, ada_w, ada_b)
    xt = x.reshape(t, d)
    for layer in range(DEPTH):
        j = layer // 2
        mod6 = mod[layer]
        if layer % 2 == 0:
            u, q, k, v, r, la = _ab_in(xt, mod6, ab_w_in[j], gla_gate_w[j], gla_gate_b[j], seq)
            tables = _s5_prep(s5_lam_re[j], s5_lam_im[j], s5_log_dt[j], s5_b_re[j], s5_b_im[j], s5_c_re[j], s5_c_im[j])
            ys = _s5_conv(u, tables, bsz, seq)
            yb = _gla(q, k, la, v, r, gla_norm_g[j], bsz, seq)
            w_out = ab_w_out[j].astype(BF16)
            consts = [s5_d[j].reshape(1, S5_W), s5_glu_w[j].astype(BF16), s5_glu_b[j].reshape(1, S5_W),
                      w_out[:S5_W], w_out[S5_W:]]
            xt = _out_call(_ab_out_body, "ab_out", [ys, u, yb], xt, mod6, consts, ln1_g[layer], ln1_b[layer], seq)
        else:
            qk, v, og, gates = _ml_in(xt, mod6, ml_w_in[j], ml_igate_b[j], ml_fgate_b[j], seq)
            y = _mlstm(qk, v, og, gates, ml_conv_w[j], ml_conv_b[j], ml_norm_g[j], bsz, seq)
            xt = _out_call(_ml_out_body, "ml_out", [y], xt, mod6, [ml_w_out[j].astype(BF16)],
                           ln1_g[layer], ln1_b[layer], seq)
        xt = _moe_layer(xt, mod6, router_w, router_bias, layer, moe_w_gate, moe_w_up, moe_w_down,
                        ln2_g[layer], ln2_b[layer], bsz, seq)
    return xt.reshape(bsz, seq, d)
```

```python
import functools
import math

import jax
import jax.numpy as jnp
from jax import lax
from jax.experimental import pallas as pl
from jax.experimental.pallas import tpu as pltpu

F32 = jnp.float32
BF16 = jnp.bfloat16

D_MODEL = 1024
DEPTH = 4
S5_W = 512
S5_GROUP = 16
S5_GROUPS = 32
S5_STATE = 64
S5_CHUNK = 16
GLA_HEADS = 4
GLA_DV = 128
GLA_DK = 64
GLA_QK = 256
GLA_V = 512
GLA_RANK = 16
GLA_GATE_NORM = 16.0
GLA_SUB = 16
AB_IN = S5_W + 2 * GLA_QK + 2 * GLA_V + GLA_RANK
AB_IN_PAD = 2176
ML_HEADS = 8
ML_W = 1024
ML_DH = 128
ML_IN = 4 * ML_W + 2 * ML_HEADS
ML_IN_PAD = 4224
CONV_W = 4
CHUNK = 64
ML_CHUNK = 128
N_EXPERTS = 16
N_GROUPS = 4
EPG = 4
D_EXPERT = 256
N_PAIRS = 6
N_CLASSES = N_GROUPS * N_PAIRS
PAIR_LO = (0, 0, 0, 1, 1, 2)
PAIR_HI = (1, 2, 3, 2, 3, 3)
ALPHA = (2.0 * DEPTH) ** 0.25
LN_EPS = 1e-5
RMS_EPS = 1e-6

V7X_VMEM_BYTES = 64 * 1024 * 1024
VMEM_LIMIT = (V7X_VMEM_BYTES * 3) // 4
ROW_TILE = 256
SEQ_TILE = 256
ROUTE_TILE = 512
PERM_TILE = 512
N_LANE_GROUPS = D_MODEL // 128


def _cparams(*sem):
    return pltpu.CompilerParams(dimension_semantics=sem, vmem_limit_bytes=VMEM_LIMIT)


def _sigmoid(x):
    return 1.0 / (1.0 + jnp.exp(-x))


def _silu(x):
    return x * _sigmoid(x)


def _log_sigmoid(z):
    return jnp.minimum(z, 0.0) - jnp.log1p(jnp.exp(-jnp.abs(z)))


def _gelu_tanh(x):
    return 0.5 * x * (1.0 + jnp.tanh(math.sqrt(2.0 / math.pi) * (x + 0.044715 * (x * x * x))))


def _layer_norm(x, g, b):
    mu = jnp.mean(x, axis=-1, keepdims=True)
    xc = x - mu
    var = jnp.mean(xc * xc, axis=-1, keepdims=True)
    return xc * lax.rsqrt(var + LN_EPS) * g + b


def _split3_bf16(x):
    hi = x.astype(BF16)
    r = x - hi.astype(F32)
    mid = r.astype(BF16)
    lo = (r - mid.astype(F32)).astype(BF16)
    return hi, mid, lo


def _dot(a, b):
    return jnp.dot(a, b, preferred_element_type=F32)


def _dot_nt(a, b):
    return lax.dot_general(a, b, (((1,), (1,)), ((), ())), preferred_element_type=F32)


def _dot_tn(a, b):
    return lax.dot_general(a, b, (((0,), (0,)), ((), ())), preferred_element_type=F32)


def _ada_body(c_ref, w_ref, b_ref, o_ref):
    c = c_ref[...]
    o_ref[0] = _dot(_silu(c), w_ref[0]) + b_ref[0]


def _ada_mod(c, ada_w, ada_b):
    bsz = c.shape[0]
    tn = 1536
    cp = jnp.zeros((8, D_MODEL), F32).at[:bsz].set(c)
    out = pl.pallas_call(
        _ada_body,
        grid=(DEPTH, 6 * D_MODEL // tn),
        in_specs=[pl.BlockSpec((8, D_MODEL), lambda l, j: (0, 0)),
                  pl.BlockSpec((1, D_MODEL, tn), lambda l, j: (l, 0, j)),
                  pl.BlockSpec((1, 1, tn), lambda l, j: (l, 0, j))],
        out_specs=pl.BlockSpec((1, 8, tn), lambda l, j: (l, 0, j)),
        out_shape=jax.ShapeDtypeStruct((DEPTH, 8, 6 * D_MODEL), F32),
        compiler_params=_cparams("parallel", "parallel"),
        name="ada_mod",
    )(cp, ada_w, ada_b.reshape(DEPTH, 1, 6 * D_MODEL))
    return out[:, :bsz].reshape(DEPTH, bsz, 6, D_MODEL)


def _ab_in_body(x_ref, mod_ref, w_ref, gw_ref, gb_ref, u_ref, q_ref, k_ref, v_ref, r_ref, la_ref):
    x = x_ref[...]
    h = (x * (1.0 + mod_ref[0, 1:2, :]) + mod_ref[0, 0:1, :]).astype(BF16)

    def seg(a, b):
        return _dot(h, w_ref[:, a:b])

    u_ref[...] = seg(0, 512)
    q_ref[...] = seg(512, 768)
    k_ref[...] = seg(768, 1024)
    v_ref[...] = seg(1024, 1536)
    r_ref[...] = seg(1536, 2048)
    a_lr = seg(2048, AB_IN_PAD)
    z = _dot(a_lr.astype(BF16), gw_ref[...]) + gb_ref[...]
    la_ref[...] = _log_sigmoid(z) * (1.0 / GLA_GATE_NORM)


def _ab_in(x, mod6, w_in, gate_w, gate_b, seq):
    t = x.shape[0]
    tm = ROW_TILE
    per_b = seq // tm
    w = jnp.zeros((D_MODEL, AB_IN_PAD), BF16).at[:, :AB_IN].set(w_in.astype(BF16))
    gw = jnp.zeros((AB_IN_PAD - 2048, GLA_QK), BF16).at[:GLA_RANK].set(gate_w.astype(BF16))
    row = lambda n: pl.BlockSpec((tm, n), lambda i: (i, 0))
    full = lambda a: pl.BlockSpec(a.shape, lambda i: (0,) * a.ndim)
    outs = [S5_W, GLA_QK, GLA_QK, GLA_V, GLA_V, GLA_QK]
    return pl.pallas_call(
        _ab_in_body,
        grid=(t // tm,),
        in_specs=[row(D_MODEL),
                  pl.BlockSpec((1, 6, D_MODEL), lambda i: (i // per_b, 0, 0)),
                  full(w), full(gw), pl.BlockSpec((1, GLA_QK), lambda i: (0, 0))],
        out_specs=[row(n) for n in outs],
        out_shape=[jax.ShapeDtypeStruct((t, n), F32) for n in outs],
        compiler_params=_cparams("parallel"),
        name="ab_in",
    )(x, mod6, w, gw, gate_b.reshape(1, GLA_QK))


def _s5_prep(lam_re, lam_im, log_dt, b_re, b_im, c_re, c_im):
    hp = lax.Precision.HIGHEST
    n = S5_CHUNK
    lre, lim = lam_re.astype(F32), lam_im.astype(F32)
    dt = jnp.exp(log_dt.astype(F32))[:, None]
    d = jnp.arange(n + 1, dtype=F32)[:, None, None]
    mag = jnp.exp(lre * dt * d)
    ang = lim * dt * d
    pw_re, pw_im = mag * jnp.cos(ang), mag * jnp.sin(ang)
    den = lre * lre + lim * lim
    nre, nim = pw_re[1] - 1.0, pw_im[1]
    coef_re = (nre * lre + nim * lim) / den
    coef_im = (nim * lre - nre * lim) / den
    bre, bim = b_re.astype(F32), b_im.astype(F32)
    bb_re = coef_re[..., None] * bre - coef_im[..., None] * bim
    bb_im = coef_re[..., None] * bim + coef_im[..., None] * bre
    cre, cim = c_re.astype(F32), c_im.astype(F32)
    cp_re = cre[None] * pw_re[:, :, None, :] - cim[None] * pw_im[:, :, None, :]
    cp_im = cre[None] * pw_im[:, :, None, :] + cim[None] * pw_re[:, :, None, :]
    kern = (jnp.einsum('dgcp,gpe->dgce', cp_re, bb_re, precision=hp)
            - jnp.einsum('dgcp,gpe->dgce', cp_im, bb_im, precision=hp))
    jj = jnp.arange(n)[:, None]
    tt = jnp.arange(n)[None, :]
    lag = tt - jj
    kg = jnp.where((lag >= 0)[:, :, None, None, None], kern[jnp.clip(lag, 0, n)], 0.0)
    m_intra = jnp.transpose(kg, (2, 0, 4, 1, 3)).reshape(S5_GROUPS, n * S5_GROUP, n * S5_GROUP)
    pws_re, pws_im = pw_re[n - 1 - jnp.arange(n)], pw_im[n - 1 - jnp.arange(n)]
    ws_re = pws_re[..., None] * bb_re[None] - pws_im[..., None] * bb_im[None]
    ws_im = pws_re[..., None] * bb_im[None] + pws_im[..., None] * bb_re[None]
    w_state = jnp.transpose(jnp.concatenate([ws_re, ws_im], axis=2), (1, 0, 3, 2))
    w_state = w_state.reshape(S5_GROUPS, n * S5_GROUP, 2 * S5_STATE)
    wi_re = jnp.transpose(cp_re[1:], (1, 3, 0, 2))
    wi_im = -jnp.transpose(cp_im[1:], (1, 3, 0, 2))
    w_inter = jnp.concatenate([wi_re, wi_im], axis=1).reshape(S5_GROUPS, 2 * S5_STATE, n * S5_GROUP)
    a_chunk = jnp.stack([pw_re[n], pw_im[n]], axis=1)
    return m_intra.astype(BF16), w_state.astype(BF16), w_inter.astype(BF16), a_chunk


def _s5_body(u_ref, m_ref, ws_ref, wi_ref, a_ref, y_ref, ere, eim, sre, sim):
    u = u_ref[0]
    e = _dot(u, ws_ref[0])
    ere[...] = e[:, :S5_STATE]
    eim[...] = e[:, S5_STATE:]
    ar = jnp.broadcast_to(a_ref[0, 0:1, :], (8, S5_STATE))
    ai = jnp.broadcast_to(a_ref[0, 1:2, :], (8, S5_STATE))
    lo = lax.broadcasted_iota(jnp.int32, (8, S5_STATE), 0) < 4
    n_tiles = ere.shape[0] // 8

    def tile(i, carry):
        c_r, c_i = carry
        r0 = pl.multiple_of(i * 8, 8)
        e_r, e_i = ere[pl.ds(r0, 8), :], eim[pl.ds(r0, 8), :]
        e_r4, e_i4 = pltpu.roll(e_r, 4, 0), pltpu.roll(e_i, 4, 0)
        t_r = ar * c_r - ai * c_i + e_r4
        t_i = ar * c_i + ai * c_r + e_i4
        sp_r, sp_i = jnp.where(lo, c_r, t_r), jnp.where(lo, c_i, t_i)
        sre[pl.ds(r0, 8), :] = sp_r
        sim[pl.ds(r0, 8), :] = sp_i
        n_r = ar * sp_r - ai * sp_i + e_r
        n_i = ar * sp_i + ai * sp_r + e_i
        return (jnp.where(lo, pltpu.roll(n_r, 4, 0), n_r), jnp.where(lo, pltpu.roll(n_i, 4, 0), n_i))

    z = jnp.zeros((8, S5_STATE), F32)
    lax.fori_loop(0, n_tiles, tile, (z, z))
    y = _dot(u, m_ref[0])
    y = y + _dot(sre[...].astype(BF16), wi_ref[0, :S5_STATE, :])
    y = y + _dot(sim[...].astype(BF16), wi_ref[0, S5_STATE:, :])
    y_ref[0] = y


def _s5_conv(u, tables, bsz, seq):
    assert bsz == 4
    m_intra, w_state, w_inter, a_chunk = tables
    nk = seq // S5_CHUNK
    rows = nk * bsz
    lanes = S5_CHUNK * S5_GROUP
    uf = u.astype(BF16).reshape(bsz, nk, S5_CHUNK, S5_GROUPS, S5_GROUP)
    uf = jnp.transpose(uf, (3, 1, 0, 2, 4)).reshape(S5_GROUPS, rows, lanes)
    grp = lambda shape: pl.BlockSpec((1,) + shape, lambda g: (g, 0, 0))
    yf = pl.pallas_call(
        _s5_body,
        grid=(S5_GROUPS,),
        in_specs=[grp((rows, lanes)), grp((lanes, lanes)), grp((lanes, 2 * S5_STATE)),
                  grp((2 * S5_STATE, lanes)), grp((2, S5_STATE))],
        out_specs=grp((rows, lanes)),
        out_shape=jax.ShapeDtypeStruct((S5_GROUPS, rows, lanes), F32),
        scratch_shapes=[pltpu.VMEM((rows, S5_STATE), F32)] * 4,
        compiler_params=_cparams("parallel"),
        name="s5_conv",
    )(uf, m_intra, w_state, w_inter, a_chunk)
    yf = yf.reshape(S5_GROUPS, nk, bsz, S5_CHUNK, S5_GROUP)
    return jnp.transpose(yf, (2, 1, 3, 0, 4)).reshape(bsz * seq, S5_W)


def _gla_body(q_ref, k_ref, g_ref, v_ref, r_ref, ng_ref, y_ref, s_ref, bc_ref, o_ref):
    tl = q_ref.shape[0]
    scale = GLA_DK ** -0.5

    @pl.when(pl.program_id(1) == 0)
    def _():
        s_ref[...] = jnp.zeros_like(s_ref)

    row = lax.broadcasted_iota(jnp.int32, (CHUNK, CHUNK), 0)
    col = lax.broadcasted_iota(jnp.int32, (CHUNK, CHUNK), 1)
    tri = col <= row
    tref = col < (row // GLA_SUB) * GLA_SUB
    cum_mat = jnp.concatenate([tri, tref], axis=0).astype(BF16)
    lane_head = lax.broadcasted_iota(jnp.int32, (GLA_SUB, GLA_QK), 1) // GLA_DK
    bd_mask = (lax.broadcasted_iota(jnp.int32, (GLA_QK, GLA_V), 0) // GLA_DK
               == lax.broadcasted_iota(jnp.int32, (GLA_QK, GLA_V), 1) // GLA_DV)
    ones_bd = bd_mask.astype(BF16)
    ones_v = jnp.ones((CHUNK, GLA_V), BF16)

    def chunk(c, carry):
        r0 = pl.multiple_of(c * CHUNK, CHUNK)
        qc = q_ref[pl.ds(r0, CHUNK), :] * scale
        kc = k_ref[pl.ds(r0, CHUNK), :]
        g_parts = _split3_bf16(g_ref[pl.ds(r0, CHUNK), :])
        vb = v_ref[pl.ds(r0, CHUNK), :].astype(BF16)
        br = sum(_dot(cum_mat, part) for part in g_parts)
        bc, ref = br[:CHUNK], br[CHUNK:]
        bc_ref[pl.ds(r0, CHUNK), :] = bc
        qt = qc * jnp.exp(bc - ref)
        state = s_ref[...]
        o = _dot((qt * jnp.exp(ref)).astype(BF16), state.astype(BF16))
        blocks = [o[:GLA_SUB]]
        for i in range(1, CHUNK // GLA_SUB):
            ri = bc[GLA_SUB * i - 1:GLA_SUB * i, :]
            kt = (kc * jnp.exp(jnp.minimum(ri - bc, 0.0))).astype(BF16)
            qi = qt[GLA_SUB * i:GLA_SUB * (i + 1)]
            lhs = jnp.concatenate([jnp.where(lane_head == h, qi, 0.0) for h in range(GLA_HEADS)], axis=0)
            att = _dot_nt(lhs.astype(BF16), kt)
            att = jnp.where(col < GLA_SUB * i, att, 0.0)
            ov = _dot(att.astype(BF16), vb)
            oi = jnp.concatenate([ov[GLA_SUB * h:GLA_SUB * (h + 1), GLA_DV * h:GLA_DV * (h + 1)]
                                  for h in range(GLA_HEADS)], axis=1)
            blocks.append(o[GLA_SUB * i:GLA_SUB * (i + 1)] + oi)
        o_ref[pl.ds(r0, CHUNK), :] = jnp.concatenate(blocks, axis=0)
        bl = bc[CHUNK - 1:CHUNK, :]
        kh = (kc * jnp.exp(bl - bc)).astype(BF16)
        upd = _dot_tn(kh, vb)
        dcol = sum(_dot_tn(part, ones_v) for part in g_parts)
        s_ref[...] = jnp.exp(dcol) * state + jnp.where(bd_mask, upd, 0.0)
        return carry

    lax.fori_loop(0, tl // CHUNK, chunk, 0)

    q = q_ref[...] * scale
    k = k_ref[...]
    v = v_ref[...]
    bc = bc_ref[...]
    rmod = lax.broadcasted_iota(jnp.int32, (tl, 1), 0) % GLA_SUB
    od = jnp.zeros((tl, GLA_V), F32)
    for d in range(GLA_SUB):
        kd = k if d == 0 else pltpu.roll(k, d, 0)
        bd = bc if d == 0 else pltpu.roll(bc, d, 0)
        vd = v if d == 0 else pltpu.roll(v, d, 0)
        valid = rmod >= d
        e = jnp.exp(jnp.where(valid, bc - bd, 0.0))
        p = jnp.where(valid, q * kd * e, 0.0).astype(BF16)
        od = od + _dot(p, ones_bd) * vd
    o = o_ref[...] + od
    ng = ng_ref[...]
    outs = []
    for h in range(GLA_HEADS):
        oh = o[:, GLA_DV * h:GLA_DV * (h + 1)]
        outs.append(oh * lax.rsqrt(jnp.mean(oh * oh, axis=-1, keepdims=True) + RMS_EPS))
    y_ref[...] = jnp.concatenate(outs, axis=1) * ng * _silu(r_ref[...])


def _gla(q, k, la, v, r, norm_g, bsz, seq):
    tl = SEQ_TILE
    nl = seq // tl
    blk = lambda n: pl.BlockSpec((tl, n), lambda b, l: (b * nl + l, 0))
    return pl.pallas_call(
        _gla_body,
        grid=(bsz, nl),
        in_specs=[blk(GLA_QK), blk(GLA_QK), blk(GLA_QK), blk(GLA_V), blk(GLA_V),
                  pl.BlockSpec((1, GLA_V), lambda b, l: (0, 0))],
        out_specs=blk(GLA_V),
        out_shape=jax.ShapeDtypeStruct((bsz * seq, GLA_V), F32),
        scratch_shapes=[pltpu.VMEM((GLA_QK, GLA_V), F32), pltpu.VMEM((tl, GLA_QK), F32),
                        pltpu.VMEM((tl, GLA_V), F32)],
        compiler_params=_cparams("parallel", "arbitrary"),
        name="gla",
    )(q, k, la, v, r, norm_g.reshape(1, GLA_V))


def _residual_ln(x, y, mod_ref, gate_row, lg_ref, lb_ref):
    return _layer_norm(ALPHA * x + (1.0 + mod_ref[0, gate_row:gate_row + 1, :]) * y, lg_ref[...], lb_ref[...])


def _store_both_layouts(o_ref, o3_ref, x):
    o_ref[...] = x
    _store_row_tiles(o3_ref, x)


def _ab_out_body(ys_ref, u_ref, yb_ref, x_ref, mod_ref, d_ref, gw_ref, gb_ref, wa_ref, wb_ref, lg_ref, lb_ref, o_ref, o3_ref):
    z = _gelu_tanh(ys_ref[...] + d_ref[...] * u_ref[...])
    ya = z * _sigmoid(_dot(z.astype(BF16), gw_ref[...]) + gb_ref[...])
    y = _dot(ya.astype(BF16), wa_ref[...]) + _dot(yb_ref[...].astype(BF16), wb_ref[...])
    _store_both_layouts(o_ref, o3_ref, _residual_ln(x_ref[...], y, mod_ref, 2, lg_ref, lb_ref))


def _ml_out_body(y_ref, x_ref, mod_ref, w_ref, lg_ref, lb_ref, o_ref, o3_ref):
    y = _dot(y_ref[...].astype(BF16), w_ref[...])
    _store_both_layouts(o_ref, o3_ref, _residual_ln(x_ref[...], y, mod_ref, 2, lg_ref, lb_ref))


def _out_call(body, name, row_inputs, x, mod6, consts, ln_g, ln_b, seq):
    t = x.shape[0]
    tm = ROW_TILE
    per_b = seq // tm
    row = lambda a: pl.BlockSpec((tm, a.shape[1]), lambda i: (i, 0))
    full = lambda a: pl.BlockSpec(a.shape, lambda i: (0,) * a.ndim)
    lg, lb = ln_g.reshape(1, D_MODEL), ln_b.reshape(1, D_MODEL)
    return pl.pallas_call(
        body,
        grid=(t // tm,),
        in_specs=([row(a) for a in row_inputs] + [row(x), pl.BlockSpec((1, 6, D_MODEL), lambda i: (i // per_b, 0, 0))]
                  + [full(a) for a in consts] + [full(lg), full(lb)]),
        out_specs=[row(x), pl.BlockSpec((tm, N_LANE_GROUPS, 128), lambda i: (i, 0, 0))],
        out_shape=[jax.ShapeDtypeStruct((t, D_MODEL), F32), jax.ShapeDtypeStruct((t, N_LANE_GROUPS, 128), F32)],
        compiler_params=_cparams("parallel"),
        name=name,
    )(*row_inputs, x, mod6, *consts, lg, lb)


def _ml_in_body(x_ref, mod_ref, w_ref, wt_ref, gb_ref, gbt_ref, qk_ref, v_ref, o_ref, gt_ref, vt_ref, gtt_ref):
    x = x_ref[...]
    h = (x * (1.0 + mod_ref[0, 1:2, :]) + mod_ref[0, 0:1, :]).astype(BF16)
    qk_ref[...] = _dot(h, w_ref[:, 0:2 * ML_W])
    v_ref[...] = _dot(h, w_ref[:, 2 * ML_W:3 * ML_W])
    o_ref[...] = _dot(h, w_ref[:, 3 * ML_W:4 * ML_W])
    gt_ref[...] = _dot(h, w_ref[:, 4 * ML_W:ML_IN_PAD]) + gb_ref[...]
    vt_ref[...] = _dot_nt(wt_ref[0:ML_W, :], h)
    gtt_ref[...] = _dot_nt(wt_ref[ML_W:, :], h) + gbt_ref[...]


def _ml_in(x, mod6, w_in, igate_b, fgate_b, seq):
    t = x.shape[0]
    tm = ROW_TILE
    per_b = seq // tm
    n_gate = ML_IN_PAD - 4 * ML_W
    w = jnp.zeros((D_MODEL, ML_IN_PAD), BF16).at[:, :ML_IN].set(w_in.astype(BF16))
    wt = w[:, 2 * ML_W:3 * ML_W].T
    wt = jnp.concatenate([wt, w[:, 4 * ML_W:].T], axis=0)
    gb = jnp.zeros((1, n_gate), F32).at[0, :2 * ML_HEADS].set(jnp.concatenate([igate_b, fgate_b]))
    row = lambda n: pl.BlockSpec((tm, n), lambda i: (i, 0))
    colblk = lambda n: pl.BlockSpec((n, tm), lambda i: (0, i))
    full = lambda a: pl.BlockSpec(a.shape, lambda i: (0,) * a.ndim)
    outs = [2 * ML_W, ML_W, ML_W, n_gate]
    return pl.pallas_call(
        _ml_in_body,
        grid=(t // tm,),
        in_specs=[row(D_MODEL), pl.BlockSpec((1, 6, D_MODEL), lambda i: (i // per_b, 0, 0)), full(w), full(wt),
                  full(gb), pl.BlockSpec((n_gate, 1), lambda i: (0, 0))],
        out_specs=[row(n) for n in outs] + [colblk(ML_W), colblk(n_gate)],
        out_shape=([jax.ShapeDtypeStruct((t, n), F32) for n in outs]
                   + [jax.ShapeDtypeStruct((ML_W, t), F32), jax.ShapeDtypeStruct((n_gate, t), F32)]),
        compiler_params=_cparams("parallel"),
        name="ml_in",
    )(x, mod6, w, wt, gb, gb.reshape(n_gate, 1))


def _mlstm_body(q_ref, k_ref, v_ref, og_ref, gt_ref, vt_ref, gtt_ref, wq_ref, wk_ref, bq_ref, bk_ref, ng_ref,
                y_ref, cq_ref, ck_ref, qs_ref, ks_ref, s_ref, n_ref, m_ref):
    tl = q_ref.shape[0]

    @pl.when(pl.program_id(1) == 0)
    def _():
        for ref in (s_ref, n_ref, m_ref, cq_ref, ck_ref):
            ref[...] = jnp.zeros_like(ref)

    def conv(x_ref, carry_ref, w_ref, b_ref, out_ref, scale):
        for h in range(ML_HEADS):
            cols = slice(ML_DH * h, ML_DH * (h + 1))
            x = x_ref[:, cols]
            xin = jnp.concatenate([carry_ref[:, cols], x], axis=0)
            acc = jnp.zeros((tl, ML_DH), F32) + b_ref[:, cols]
            for i in range(CONV_W):
                sh = CONV_W - 1 - i
                xs = xin if sh == 0 else pltpu.roll(xin, sh, 0)
                acc = acc + w_ref[i:i + 1, cols] * xs[8:8 + tl]
            carry_ref[:, cols] = x[tl - 8:tl]
            out_ref[:, cols] = _silu(acc) * scale

    conv(q_ref, cq_ref, wq_ref, bq_ref, qs_ref, 1.0)
    conv(k_ref, ck_ref, wk_ref, bk_ref, ks_ref, ML_DH ** -0.5)

    cs = ML_CHUNK
    row = lax.broadcasted_iota(jnp.int32, (cs, cs), 0)
    col = lax.broadcasted_iota(jnp.int32, (cs, cs), 1)
    causal = col <= row
    tri = causal.astype(BF16)
    tri_t = (row <= col).astype(BF16)

    for c in range(tl // cs):
        rows = slice(cs * c, cs * (c + 1))
        gt = gt_ref[rows, :]
        gt_t = gtt_ref[:, rows]
        lf_c = _log_sigmoid(gt)
        lf_r = _log_sigmoid(gt_t[ML_HEADS:2 * ML_HEADS])
        b_c_all = sum(_dot(tri, part) for part in _split3_bf16(lf_c))
        b_r_all = sum(_dot(part, tri_t) for part in _split3_bf16(lf_r))
        per_head = lambda f: jnp.stack([f(h) for h in range(ML_HEADS)])
        head_cols = lambda h: slice(ML_DH * h, ML_DH * (h + 1))
        q3 = per_head(lambda h: qs_ref[rows, head_cols(h)])
        k3 = per_head(lambda h: ks_ref[rows, head_cols(h)])
        vb = per_head(lambda h: v_ref[rows, head_cols(h)]).astype(BF16)
        vtb = per_head(lambda h: vt_ref[head_cols(h), rows]).astype(BF16)
        i_c = per_head(lambda h: gt[:, h:h + 1])
        b_c = per_head(lambda h: b_c_all[:, ML_HEADS + h:ML_HEADS + h + 1])
        i_r = per_head(lambda h: gt_t[h:h + 1, :])
        b_r = per_head(lambda h: b_r_all[h:h + 1, :])
        m = m_ref[:, :, 0:1]
        dmat = jnp.where(causal[None], b_c - b_r + i_r, -jnp.inf)
        inter_log = b_c + m
        m_t = jnp.maximum(inter_log, jnp.max(dmat, axis=2, keepdims=True))
        w_intra = jnp.exp(dmat - m_t)
        w_inter = jnp.exp(inter_log - m_t)
        qb = q3.astype(BF16)
        s = jnp.einsum('htd,hsd->hts', qb, k3.astype(BF16), preferred_element_type=F32) * w_intra
        state, nrm = s_ref[...], n_ref[...]
        num = (w_inter * jnp.einsum('htd,hvd->htv', qb, state.astype(BF16), preferred_element_type=F32)
               + jnp.einsum('hts,hsv->htv', s.astype(BF16), vb, preferred_element_type=F32))
        den = w_inter * jnp.sum(q3 * nrm, axis=2, keepdims=True) + jnp.sum(s, axis=2, keepdims=True)
        hc = num / jnp.maximum(jnp.abs(den), jnp.exp(-m_t))
        hc = hc * lax.rsqrt(jnp.mean(hc * hc, axis=2, keepdims=True) + RMS_EPS)
        for h in range(ML_HEADS):
            cols = head_cols(h)
            y_ref[rows, cols] = _sigmoid(og_ref[rows, cols]) * (hc[h] * ng_ref[:, cols])
        b_last = b_c[:, cs - 1:cs, :]
        gs_c = b_last - b_c + i_c
        gs_r = b_last - b_r + i_r
        m_new = jnp.maximum(b_last + m, jnp.max(gs_r, axis=2, keepdims=True))
        dec = jnp.exp(b_last + m - m_new)
        kw = k3 * jnp.exp(gs_c - m_new)
        s_ref[...] = dec * state + jnp.einsum('hvt,htd->hvd', vtb, kw.astype(BF16), preferred_element_type=F32)
        n_ref[...] = dec * nrm + jnp.sum(kw, axis=1, keepdims=True)
        m_ref[...] = jnp.broadcast_to(m_new, m_ref.shape)


def _mlstm(qk, v, og, gates, v_t, gates_t, conv_w, conv_b, norm_g, bsz, seq):
    tl = SEQ_TILE
    nl = seq // tl
    blk = lambda n, off: pl.BlockSpec((tl, n), lambda b, l: (b * nl + l, off))
    blk_t = lambda n: pl.BlockSpec((n, tl), lambda b, l: (0, b * nl + l))
    par = lambda rows, off: pl.BlockSpec((rows, ML_W), lambda b, l: (0, off))
    state = [pltpu.VMEM((ML_HEADS, ML_DH, ML_DH), F32), pltpu.VMEM((ML_HEADS, 1, ML_DH), F32),
             pltpu.VMEM((ML_HEADS, 1, ML_DH), F32)]
    return pl.pallas_call(
        _mlstm_body,
        grid=(bsz, nl),
        in_specs=[blk(ML_W, 0), blk(ML_W, 1), blk(ML_W, 0), blk(ML_W, 0), blk(gates.shape[1], 0),
                  blk_t(ML_W), blk_t(gates_t.shape[0]),
                  par(CONV_W, 0), par(CONV_W, 1), par(1, 0), par(1, 1), par(1, 0)],
        out_specs=blk(ML_W, 0),
        out_shape=jax.ShapeDtypeStruct((bsz * seq, ML_W), F32),
        scratch_shapes=[pltpu.VMEM((8, ML_W), F32), pltpu.VMEM((8, ML_W), F32),
                        pltpu.VMEM((tl, ML_W), F32), pltpu.VMEM((tl, ML_W), F32)] + state,
        compiler_params=_cparams("parallel", "arbitrary"),
        name="mlstm",
    )(qk, qk, v, og, gates, v_t, gates_t, conv_w, conv_w, conv_b.reshape(1, -1), conv_b.reshape(1, -1),
      norm_g.reshape(1, ML_W))


def _router_body(x_ref, mod_ref, rw_ref, rb_ref, info_ref, cnt_ref, carry_ref):
    tr = x_ref.shape[0]

    @pl.when(pl.program_id(0) == 0)
    def _():
        carry_ref[...] = jnp.zeros_like(carry_ref)

    h = x_ref[...] * (1.0 + mod_ref[0, 4:5, :]) + mod_ref[0, 3:4, :]
    logits = lax.dot_general(rw_ref[...], h, (((1,), (1,)), ((), ())), precision=lax.Precision.HIGHEST,
                             preferred_element_type=F32)
    aff = _sigmoid(logits)
    sel = aff + rb_ref[...]
    s = [sel[e:e + 1, :] for e in range(N_EXPERTS)]
    a = [aff[e:e + 1, :] for e in range(N_EXPERTS)]

    def top2_sum(v):
        best = v[0] + v[1]
        for lo_i, hi_i in zip(PAIR_LO[1:], PAIR_HI[1:]):
            best = jnp.maximum(best, v[lo_i] + v[hi_i])
        return best

    score = [top2_sum(s[EPG * g:EPG * (g + 1)]) for g in range(N_GROUPS)]
    g_idx = jnp.zeros((1, tr), jnp.int32)
    best = score[0]
    for g in range(1, N_GROUPS):
        upd = score[g] > best
        g_idx = jnp.where(upd, g, g_idx)
        best = jnp.where(upd, score[g], best)

    def pick_group(rows, j):
        out = rows[j]
        for g in range(1, N_GROUPS):
            out = jnp.where(g_idx == g, rows[EPG * g + j], out)
        return out

    sg = [pick_group(s, j) for j in range(EPG)]
    ag = [pick_group(a, j) for j in range(EPG)]
    first = jnp.zeros((1, tr), jnp.int32)
    best = sg[0]
    for j in range(1, EPG):
        upd = sg[j] > best
        first = jnp.where(upd, j, first)
        best = jnp.where(upd, sg[j], best)
    second = jnp.zeros((1, tr), jnp.int32)
    best2 = jnp.full((1, tr), -jnp.inf, F32)
    for j in range(EPG):
        upd = (first != j) & (sg[j] > best2)
        second = jnp.where(upd, j, second)
        best2 = jnp.where(upd, sg[j], best2)
    lo = jnp.minimum(first, second)
    hi = jnp.maximum(first, second)
    pair = jnp.where(lo == 0, hi - 1, jnp.where(lo == 1, hi + 1, N_PAIRS - 1))
    cls = g_idx * N_PAIRS + pair

    def pick_local(rows, idx):
        out = rows[0]
        for j in range(1, EPG):
            out = jnp.where(idx == j, rows[j], out)
        return out

    a_lo, a_hi = pick_local(ag, lo), pick_local(ag, hi)
    tot = a_lo + a_hi
    onehot = (lax.broadcasted_iota(jnp.int32, (32, tr), 0) == cls).astype(F32)
    before = (lax.broadcasted_iota(jnp.int32, (tr, tr), 0) < lax.broadcasted_iota(jnp.int32, (tr, tr), 1)).astype(BF16)
    cum = _dot(onehot.astype(BF16), before) + carry_ref[:, 0:1]
    rank = jnp.sum(onehot * cum, axis=0, keepdims=True)
    carry_ref[...] = carry_ref[...] + jnp.sum(onehot, axis=1, keepdims=True)
    cnt_ref[...] = carry_ref[...]
    zero = jnp.zeros((1, tr), F32)
    info_ref[...] = jnp.concatenate([cls.astype(F32), rank, a_lo / tot, a_hi / tot, zero, zero, zero, zero], axis=0)


def _router(x, mod6, router_w, router_bias, seq):
    t = x.shape[0]
    tr = ROUTE_TILE
    per_b = seq // tr
    return pl.pallas_call(
        _router_body,
        grid=(t // tr,),
        in_specs=[pl.BlockSpec((tr, D_MODEL), lambda i: (i, 0)),
                  pl.BlockSpec((1, 6, D_MODEL), lambda i: (i // per_b, 0, 0)),
                  pl.BlockSpec((N_EXPERTS, D_MODEL), lambda i: (0, 0)),
                  pl.BlockSpec((N_EXPERTS, 1), lambda i: (0, 0))],
        out_specs=[pl.BlockSpec((8, tr), lambda i: (0, i)), pl.BlockSpec((32, 128), lambda i: (0, 0))],
        out_shape=[jax.ShapeDtypeStruct((8, t), F32), jax.ShapeDtypeStruct((32, 128), F32)],
        scratch_shapes=[pltpu.VMEM((32, 128), F32)],
        compiler_params=_cparams("arbitrary"),
        name="router",
    )(x, mod6, router_w.T.astype(F32), router_bias.reshape(N_EXPERTS, 1).astype(F32))


def _from_row_tiles(ref):
    return jnp.concatenate([ref[:, j, :] for j in range(N_LANE_GROUPS)], axis=1)


def _store_row_tiles(ref, x):
    for j in range(N_LANE_GROUPS):
        ref[:, j, :] = x[:, 128 * j:128 * (j + 1)]


def _retile_body(x_ref, o_ref):
    o_ref[...] = _from_row_tiles(x_ref)


def _retile(x3):
    t = x3.shape[0]
    tm = PERM_TILE
    return pl.pallas_call(
        _retile_body,
        grid=(t // tm,),
        in_specs=[pl.BlockSpec((tm, N_LANE_GROUPS, 128), lambda i: (i, 0, 0))],
        out_specs=pl.BlockSpec((tm, D_MODEL), lambda i: (i, 0)),
        out_shape=jax.ShapeDtypeStruct((t, D_MODEL), F32),
        compiler_params=_cparams("parallel"),
        name="retile",
    )(x3)


def _moe_body(e_lo_ref, e_hi_ref, blk_ref, new_ref, nv_ref, used_ref, src_ref,
              x_hbm, meta_ref, mod_ref, wg_lo32, wu_lo32, wd_lo32, wg_hi32, wu_hi32, wd_hi32, lg_ref, lb_ref,
              out_hbm, wg_lo, wu_lo, wd_lo, wg_hi, wu_hi, wd_hi, xbuf, obuf, sem_in, sem_out):
    del e_lo_ref, e_hi_ref, blk_ref
    i = pl.program_id(0)
    used = used_ref[0]
    tm = xbuf.shape[1]
    slot = i % 2

    def fetch_row(tile, sl, r):
        return pltpu.make_async_copy(x_hbm.at[src_ref[tile * tm + r]], xbuf.at[sl, r], sem_in.at[sl])

    def send_row(tile, sl, r):
        return pltpu.make_async_copy(obuf.at[sl, r], out_hbm.at[src_ref[tile * tm + r]], sem_out.at[sl])

    def start_fetch(tile, sl):
        def one(r, c):
            fetch_row(tile, sl, r).start()
            return c
        lax.fori_loop(0, tm, one, 0)

    def wait_fetch(tile, sl):
        def one(r, c):
            fetch_row(tile, sl, r).wait()
            return c
        lax.fori_loop(0, tm, one, 0)

    def start_send(tile, sl):
        def one(r, c):
            send_row(tile, sl, r).start()
            return c
        lax.fori_loop(0, nv_ref[tile], one, 0)

    def wait_send(tile, sl):
        def one(r, c):
            send_row(tile, sl, r).wait()
            return c
        lax.fori_loop(0, nv_ref[tile], one, 0)

    @pl.when(new_ref[i] == 1)
    def _():
        for src, dst in ((wg_lo32, wg_lo), (wu_lo32, wu_lo), (wd_lo32, wd_lo),
                         (wg_hi32, wg_hi), (wu_hi32, wu_hi), (wd_hi32, wd_hi)):
            dst[...] = src[...].astype(BF16)

    @pl.when(i == 0)
    def _():
        start_fetch(0, 0)

    @pl.when(i < used)
    def _():
        wait_fetch(i, slot)

        @pl.when(i + 1 < used)
        def _():
            start_fetch(i + 1, 1 - slot)

        x = _from_row_tiles(xbuf.at[slot])
        meta = meta_ref[...]
        bid, g_lo, g_hi = meta[:, 0:1], meta[:, 1:2], meta[:, 2:3]
        nb = mod_ref.shape[0]

        def per_row(r):
            out = mod_ref[0, r:r + 1, :]
            for b in range(1, nb):
                out = jnp.where(bid == float(b), mod_ref[b, r:r + 1, :], out)
            return out

        h = (x * (1.0 + per_row(4)) + per_row(3)).astype(BF16)

        def expert(wg, wu, wd):
            he = _silu(_dot(h, wg[...])) * _dot(h, wu[...])
            return _dot(he.astype(BF16), wd[...])

        y = g_lo * expert(wg_lo, wu_lo, wd_lo) + g_hi * expert(wg_hi, wu_hi, wd_hi)
        out = _layer_norm(ALPHA * x + (1.0 + per_row(5)) * y, lg_ref[...], lb_ref[...])
        _store_row_tiles(obuf.at[slot], out)
        start_send(i, slot)

        @pl.when(i > 0)
        def _():
            wait_send(i - 1, 1 - slot)

        @pl.when(i == used - 1)
        def _():
            wait_send(i, slot)


def _moe_tiles(e_lo, e_hi, blk, new, n_valid, used, src, x3, meta, mod6, layer, w_gate, w_up, w_down, ln_g, ln_b):
    t = x3.shape[0]
    tm = ROW_TILE
    n_tiles = e_lo.shape[0]
    wspec = lambda shape, which: pl.BlockSpec((None, None) + shape, lambda i, *s: (layer, s[which][i], 0, 0))
    full = lambda a: pl.BlockSpec(a.shape, lambda i, *s: (0,) * a.ndim)
    lg, lb = ln_g.reshape(1, D_MODEL), ln_b.reshape(1, D_MODEL)
    gu, dn = (D_MODEL, D_EXPERT), (D_EXPERT, D_MODEL)
    return pl.pallas_call(
        _moe_body,
        grid_spec=pltpu.PrefetchScalarGridSpec(
            num_scalar_prefetch=7,
            grid=(n_tiles,),
            in_specs=[pl.BlockSpec(memory_space=pl.ANY), pl.BlockSpec((tm, 8), lambda i, *s: (s[2][i], 0)), full(mod6),
                      wspec(gu, 0), wspec(gu, 0), wspec(dn, 0), wspec(gu, 1), wspec(gu, 1), wspec(dn, 1),
                      full(lg), full(lb)],
            out_specs=pl.BlockSpec(memory_space=pl.ANY),
            scratch_shapes=([pltpu.VMEM(gu, BF16), pltpu.VMEM(gu, BF16), pltpu.VMEM(dn, BF16)] * 2
                            + [pltpu.VMEM((2, tm, N_LANE_GROUPS, 128), F32)] * 2
                            + [pltpu.SemaphoreType.DMA((2,))] * 2)),
        out_shape=jax.ShapeDtypeStruct((t, N_LANE_GROUPS, 128), F32),
        compiler_params=_cparams("arbitrary"),
        name="moe_tiles",
    )(e_lo, e_hi, blk, new, n_valid, used, src, x3, meta, mod6, w_gate, w_up, w_down, w_gate, w_up, w_down, lg, lb)


def _moe_layer(x, x3, mod6, router_w, router_bias, layer, w_gate, w_up, w_down, ln_g, ln_b, bsz, seq):
    t = x.shape[0]
    tm = ROW_TILE
    n_tiles = t // tm + N_CLASSES
    tp = n_tiles * tm
    info, counts = _router(x, mod6, router_w, router_bias, seq)
    cls = info[0].astype(jnp.int32)
    rank = info[1].astype(jnp.int32)
    cnt = counts[:N_CLASSES, 0].astype(jnp.int32)
    tiles_c = (cnt + tm - 1) // tm
    tile_end = jnp.cumsum(tiles_c)
    offs = (tile_end - tiles_c) * tm
    onehot = cls[:, None] == jnp.arange(N_CLASSES)[None, :]
    dest = rank + jnp.sum(jnp.where(onehot, offs[None, :], 0), axis=1)
    used = tile_end[-1]
    tile_id = jnp.arange(n_tiles)
    blk = jnp.maximum(jnp.minimum(tile_id, used - 1), 0)
    tile_cls = jnp.minimum(jnp.sum(blk[:, None] >= tile_end[None, :], axis=1), N_CLASSES - 1)
    grp, pair = tile_cls // N_PAIRS, tile_cls % N_PAIRS
    e_lo = (grp * EPG + jnp.asarray(PAIR_LO, jnp.int32)[pair]).astype(jnp.int32)
    e_hi = (grp * EPG + jnp.asarray(PAIR_HI, jnp.int32)[pair]).astype(jnp.int32)
    new = jnp.concatenate([jnp.ones((1,), jnp.int32), (tile_cls[1:] != tile_cls[:-1]).astype(jnp.int32)])
    tile_in_cls = blk - (tile_end - tiles_c)[tile_cls]
    n_valid = jnp.where(tile_id < used, jnp.clip(cnt[tile_cls] - tile_in_cls * tm, 0, tm), 0).astype(jnp.int32)
    bid = (jnp.arange(t) // seq).astype(F32)
    meta_nat = jnp.stack([bid, info[2], info[3], jnp.arange(t, dtype=F32)] + [jnp.zeros((t,), F32)] * 4, axis=1)
    meta = jnp.zeros((tp, 8), F32).at[dest].set(meta_nat)
    src = meta[:, 3].astype(jnp.int32)
    out3 = _moe_tiles(e_lo, e_hi, blk.astype(jnp.int32), new, n_valid, used.reshape(1).astype(jnp.int32), src,
                      x3, meta, mod6, layer, w_gate, w_up, w_down, ln_g, ln_b)
    return _retile(out3)


def kernel(x, c, router_w, router_bias, ada_w, ada_b, ln1_g, ln1_b, ln2_g, ln2_b, moe_w_gate, moe_w_up, moe_w_down,
           ab_w_in, s5_lam_re, s5_lam_im, s5_log_dt, s5_b_re, s5_b_im, s5_c_re, s5_c_im, s5_d, s5_glu_w, s5_glu_b,
           gla_gate_w, gla_gate_b, gla_norm_g, ab_w_out, ml_w_in, ml_conv_w, ml_conv_b, ml_igate_b, ml_fgate_b,
           ml_norm_g, ml_w_out):
    bsz, seq, d = x.shape
    t = bsz * seq
    mod = _ada_mod(c, ada_w, ada_b)
    xt = x.reshape(t, d)
    for layer in range(DEPTH):
        j = layer // 2
        mod6 = mod[layer]
        if layer % 2 == 0:
            u, q, k, v, r, la = _ab_in(xt, mod6, ab_w_in[j], gla_gate_w[j], gla_gate_b[j], seq)
            tables = _s5_prep(s5_lam_re[j], s5_lam_im[j], s5_log_dt[j], s5_b_re[j], s5_b_im[j], s5_c_re[j], s5_c_im[j])
            ys = _s5_conv(u, tables, bsz, seq)
            yb = _gla(q, k, la, v, r, gla_norm_g[j], bsz, seq)
            w_out = ab_w_out[j].astype(BF16)
            consts = [s5_d[j].reshape(1, S5_W), s5_glu_w[j].astype(BF16), s5_glu_b[j].reshape(1, S5_W),
                      w_out[:S5_W], w_out[S5_W:]]
            xt, xt3 = _out_call(_ab_out_body, "ab_out", [ys, u, yb], xt, mod6, consts, ln1_g[layer], ln1_b[layer], seq)
        else:
            qk, v, og, gates, v_t, gates_t = _ml_in(xt, mod6, ml_w_in[j], ml_igate_b[j], ml_fgate_b[j], seq)
            y = _mlstm(qk, v, og, gates, v_t, gates_t, ml_conv_w[j], ml_conv_b[j], ml_norm_g[j], bsz, seq)
            xt, xt3 = _out_call(_ml_out_body, "ml_out", [y], xt, mod6, [ml_w_out[j].astype(BF16)],
                                ln1_g[layer], ln1_b[layer], seq)
        xt = _moe_layer(xt, xt3, mod6, router_w, router_bias, layer, moe_w_gate, moe_w_up, moe_w_down,
                        ln2_g[layer], ln2_b[layer], bsz, seq)
    return xt.reshape(bsz, seq, d)
```

```python
import functools
import math

import jax
import jax.numpy as jnp
from jax import lax
from jax.experimental import pallas as pl
from jax.experimental.pallas import tpu as pltpu

F32 = jnp.float32
BF16 = jnp.bfloat16

D_MODEL = 1024
DEPTH = 4
S5_W = 512
S5_GROUP = 16
S5_GROUPS = 32
S5_STATE = 64
S5_CHUNK = 16
S5_SCAN_ROWS = 8
GLA_HEADS = 4
GLA_DV = 128
GLA_DK = 64
GLA_QK = 256
GLA_V = 512
GLA_RANK = 16
GLA_GATE_NORM = 16.0
GLA_SUB = 16
AB_IN = S5_W + 2 * GLA_QK + 2 * GLA_V + GLA_RANK
AB_IN_PAD = 2176
ML_HEADS = 8
ML_W = 1024
ML_DH = 128
ML_IN = 4 * ML_W + 2 * ML_HEADS
ML_IN_PAD = 4224
CONV_W = 4
CHUNK = 64
ML_CHUNK = 128
N_EXPERTS = 16
N_GROUPS = 4
EPG = 4
D_EXPERT = 256
N_PAIRS = 6
N_CLASSES = N_GROUPS * N_PAIRS
PAIR_LO = (0, 0, 0, 1, 1, 2)
PAIR_HI = (1, 2, 3, 2, 3, 3)
ALPHA = (2.0 * DEPTH) ** 0.25
LN_EPS = 1e-5
RMS_EPS = 1e-6

V7X_VMEM_BYTES = 64 * 1024 * 1024
VMEM_LIMIT = (V7X_VMEM_BYTES * 3) // 4
ROW_TILE = 256
SEQ_TILE = 256
ROUTE_TILE = 512
PERM_TILE = 512
N_LANE_GROUPS = D_MODEL // 128
MOE_DMA_PHASES = 8


def _cparams(*sem):
    return pltpu.CompilerParams(dimension_semantics=sem, vmem_limit_bytes=VMEM_LIMIT)


def _sigmoid(x):
    return 1.0 / (1.0 + jnp.exp(-x))


def _silu(x):
    return x * _sigmoid(x)


def _log_sigmoid(z):
    return jnp.minimum(z, 0.0) - jnp.log1p(jnp.exp(-jnp.abs(z)))


def _gelu_tanh(x):
    return 0.5 * x * (1.0 + jnp.tanh(math.sqrt(2.0 / math.pi) * (x + 0.044715 * (x * x * x))))


def _layer_norm(x, g, b):
    mu = jnp.mean(x, axis=-1, keepdims=True)
    xc = x - mu
    var = jnp.mean(xc * xc, axis=-1, keepdims=True)
    return xc * lax.rsqrt(var + LN_EPS) * g + b


def _split3_bf16(x):
    hi = x.astype(BF16)
    r = x - hi.astype(F32)
    mid = r.astype(BF16)
    lo = (r - mid.astype(F32)).astype(BF16)
    return hi, mid, lo


def _dot(a, b):
    return jnp.dot(a, b, preferred_element_type=F32)


def _dot_nt(a, b):
    return lax.dot_general(a, b, (((1,), (1,)), ((), ())), preferred_element_type=F32)


def _dot_tn(a, b):
    return lax.dot_general(a, b, (((0,), (0,)), ((), ())), preferred_element_type=F32)


def _ada_body(c_ref, w_ref, b_ref, o_ref):
    c = c_ref[...]
    o_ref[0] = _dot(_silu(c), w_ref[0]) + b_ref[0]


def _ada_mod(c, ada_w, ada_b):
    bsz = c.shape[0]
    tn = 1536
    cp = jnp.zeros((8, D_MODEL), F32).at[:bsz].set(c)
    out = pl.pallas_call(
        _ada_body,
        grid=(DEPTH, 6 * D_MODEL // tn),
        in_specs=[pl.BlockSpec((8, D_MODEL), lambda l, j: (0, 0)),
                  pl.BlockSpec((1, D_MODEL, tn), lambda l, j: (l, 0, j)),
                  pl.BlockSpec((1, 1, tn), lambda l, j: (l, 0, j))],
        out_specs=pl.BlockSpec((1, 8, tn), lambda l, j: (l, 0, j)),
        out_shape=jax.ShapeDtypeStruct((DEPTH, 8, 6 * D_MODEL), F32),
        compiler_params=_cparams("parallel", "parallel"),
        name="ada_mod",
    )(cp, ada_w, ada_b.reshape(DEPTH, 1, 6 * D_MODEL))
    return out[:, :bsz].reshape(DEPTH, bsz, 6, D_MODEL)


def _ab_in_body(x_ref, mod_ref, w_ref, gw_ref, gb_ref, u_ref, q_ref, k_ref, v_ref, r_ref, la_ref):
    x = x_ref[...]
    h = (x * (1.0 + mod_ref[0, 1:2, :]) + mod_ref[0, 0:1, :]).astype(BF16)

    def seg(a, b):
        return _dot(h, w_ref[:, a:b])

    u_ref[...] = seg(0, 512)
    q_ref[...] = seg(512, 768)
    k_ref[...] = seg(768, 1024)
    v_ref[...] = seg(1024, 1536)
    r_ref[...] = seg(1536, 2048)
    a_lr = seg(2048, AB_IN_PAD)
    z = _dot(a_lr.astype(BF16), gw_ref[...]) + gb_ref[...]
    la_ref[...] = _log_sigmoid(z) * (1.0 / GLA_GATE_NORM)


def _ab_in(x, mod6, w_in, gate_w, gate_b, seq):
    t = x.shape[0]
    tm = ROW_TILE
    per_b = seq // tm
    w = jnp.zeros((D_MODEL, AB_IN_PAD), BF16).at[:, :AB_IN].set(w_in.astype(BF16))
    gw = jnp.zeros((AB_IN_PAD - 2048, GLA_QK), BF16).at[:GLA_RANK].set(gate_w.astype(BF16))
    row = lambda n: pl.BlockSpec((tm, n), lambda i: (i, 0))
    full = lambda a: pl.BlockSpec(a.shape, lambda i: (0,) * a.ndim)
    outs = [S5_W, GLA_QK, GLA_QK, GLA_V, GLA_V, GLA_QK]
    return pl.pallas_call(
        _ab_in_body,
        grid=(t // tm,),
        in_specs=[row(D_MODEL),
                  pl.BlockSpec((1, 6, D_MODEL), lambda i: (i // per_b, 0, 0)),
                  full(w), full(gw), pl.BlockSpec((1, GLA_QK), lambda i: (0, 0))],
        out_specs=[row(n) for n in outs],
        out_shape=[jax.ShapeDtypeStruct((t, n), F32) for n in outs],
        compiler_params=_cparams("parallel"),
        name="ab_in",
    )(x, mod6, w, gw, gate_b.reshape(1, GLA_QK))


def _s5_prep(lam_re, lam_im, log_dt, b_re, b_im, c_re, c_im):
    hp = lax.Precision.HIGHEST
    n = S5_CHUNK
    lre, lim = lam_re.astype(F32), lam_im.astype(F32)
    dt = jnp.exp(log_dt.astype(F32))[:, None]
    d = jnp.arange(n + 1, dtype=F32)[:, None, None]
    mag = jnp.exp(lre * dt * d)
    ang = lim * dt * d
    pw_re, pw_im = mag * jnp.cos(ang), mag * jnp.sin(ang)
    den = lre * lre + lim * lim
    nre, nim = pw_re[1] - 1.0, pw_im[1]
    coef_re = (nre * lre + nim * lim) / den
    coef_im = (nim * lre - nre * lim) / den
    bre, bim = b_re.astype(F32), b_im.astype(F32)
    bb_re = coef_re[..., None] * bre - coef_im[..., None] * bim
    bb_im = coef_re[..., None] * bim + coef_im[..., None] * bre
    cre, cim = c_re.astype(F32), c_im.astype(F32)
    cp_re = cre[None] * pw_re[:, :, None, :] - cim[None] * pw_im[:, :, None, :]
    cp_im = cre[None] * pw_im[:, :, None, :] + cim[None] * pw_re[:, :, None, :]
    kern = (jnp.einsum('dgcp,gpe->dgce', cp_re, bb_re, precision=hp)
            - jnp.einsum('dgcp,gpe->dgce', cp_im, bb_im, precision=hp))
    jj = jnp.arange(n)[:, None]
    tt = jnp.arange(n)[None, :]
    lag = tt - jj
    kg = jnp.where((lag >= 0)[:, :, None, None, None], kern[jnp.clip(lag, 0, n)], 0.0)
    m_intra = jnp.transpose(kg, (2, 0, 4, 1, 3)).reshape(S5_GROUPS, n * S5_GROUP, n * S5_GROUP)
    pws_re, pws_im = pw_re[n - 1 - jnp.arange(n)], pw_im[n - 1 - jnp.arange(n)]
    ws_re = pws_re[..., None] * bb_re[None] - pws_im[..., None] * bb_im[None]
    ws_im = pws_re[..., None] * bb_im[None] + pws_im[..., None] * bb_re[None]
    w_state = jnp.transpose(jnp.concatenate([ws_re, ws_im], axis=2), (1, 0, 3, 2))
    w_state = w_state.reshape(S5_GROUPS, n * S5_GROUP, 2 * S5_STATE)
    wi_re = jnp.transpose(cp_re[1:], (1, 3, 0, 2))
    wi_im = -jnp.transpose(cp_im[1:], (1, 3, 0, 2))
    w_inter = jnp.concatenate([wi_re, wi_im], axis=1).reshape(S5_GROUPS, 2 * S5_STATE, n * S5_GROUP)
    r = (n * jnp.arange(S5_SCAN_ROWS + 1, dtype=F32))[None, :, None]
    cmag, cang = jnp.exp(lre[:, None, :] * dt[:, None, :] * r), lim[:, None, :] * dt[:, None, :] * r
    a_pows = jnp.concatenate([cmag * jnp.cos(cang), cmag * jnp.sin(cang)], axis=1)
    return m_intra.astype(BF16), w_state.astype(BF16), w_inter.astype(BF16), a_pows


def _s5_body(u_ref, m_ref, ws_ref, wi_ref, a_ref, y_ref, ere, eim, sre, sim, *, bsz):
    u = u_ref[0]
    e = _dot(u, ws_ref[0])
    ere[...] = e[:, :S5_STATE]
    eim[...] = e[:, S5_STATE:]
    n_pow = S5_SCAN_ROWS + 1
    pr, pi = a_ref[0, 0:n_pow, :], a_ref[0, n_pow:2 * n_pow, :]
    tab_r, tab_i = pr[0:S5_SCAN_ROWS], pi[0:S5_SCAN_ROWS]
    rowi = lax.broadcasted_iota(jnp.int32, (S5_SCAN_ROWS, S5_STATE), 0)
    nk = ere.shape[0] // bsz

    def cmul(a_r, a_i, x_r, x_i):
        return a_r * x_r - a_i * x_i, a_r * x_i + a_i * x_r

    def shifted(x, s):
        return jnp.where(rowi >= s, pltpu.roll(x, s, 0), 0.0)

    def tile(m, carry):
        out = []
        for b in range(bsz):
            c_r, c_i = carry[2 * b], carry[2 * b + 1]
            r0 = pl.multiple_of(b * nk + m * S5_SCAN_ROWS, S5_SCAN_ROWS)
            x_r, x_i = ere[pl.ds(r0, S5_SCAN_ROWS), :], eim[pl.ds(r0, S5_SCAN_ROWS), :]
            for s in (1, 2, 4):
                d_r, d_i = cmul(pr[s:s + 1], pi[s:s + 1], shifted(x_r, s), shifted(x_i, s))
                x_r, x_i = x_r + d_r, x_i + d_i
            t_r, t_i = cmul(tab_r, tab_i, c_r, c_i)
            sre[pl.ds(r0, S5_SCAN_ROWS), :] = shifted(x_r, 1) + t_r
            sim[pl.ds(r0, S5_SCAN_ROWS), :] = shifted(x_i, 1) + t_i
            n_r, n_i = cmul(pr[S5_SCAN_ROWS:n_pow], pi[S5_SCAN_ROWS:n_pow], c_r, c_i)
            last = slice(S5_SCAN_ROWS - 1, S5_SCAN_ROWS)
            out += [n_r + jnp.broadcast_to(x_r[last], c_r.shape), n_i + jnp.broadcast_to(x_i[last], c_i.shape)]
        return tuple(out)

    z = jnp.zeros((S5_SCAN_ROWS, S5_STATE), F32)
    lax.fori_loop(0, nk // S5_SCAN_ROWS, tile, (z,) * (2 * bsz))
    y = _dot(u, m_ref[0])
    y = y + _dot(sre[...].astype(BF16), wi_ref[0, :S5_STATE, :])
    y = y + _dot(sim[...].astype(BF16), wi_ref[0, S5_STATE:, :])
    y_ref[0] = y


def _s5_pack_body(u_ref, o_ref):
    nk = u_ref.shape[0] // S5_CHUNK
    by_step = [u_ref[pl.ds(t, nk, stride=S5_CHUNK), :] for t in range(S5_CHUNK)]
    for g in range(128 // S5_GROUP):
        cols = slice(S5_GROUP * g, S5_GROUP * (g + 1))
        o_ref[g] = jnp.concatenate([x[:, cols] for x in by_step], axis=1).astype(o_ref.dtype)


def _s5_unpack_body(y_ref, o_ref):
    nk = y_ref.shape[1]
    groups = [y_ref[g] for g in range(128 // S5_GROUP)]
    for t in range(S5_CHUNK):
        cols = slice(S5_GROUP * t, S5_GROUP * (t + 1))
        o_ref[pl.ds(t, nk, stride=S5_CHUNK), :] = jnp.concatenate([y[:, cols] for y in groups], axis=1)


def _s5_conv(u, tables, bsz, seq):
    m_intra, w_state, w_inter, a_pows = tables
    t = bsz * seq
    rows = t // S5_CHUNK
    lanes = S5_CHUNK * S5_GROUP
    tm = PERM_TILE
    gpb = 128 // S5_GROUP
    tok = pl.BlockSpec((tm, 128), lambda i, j: (i, j))
    chunked = pl.BlockSpec((gpb, tm // S5_CHUNK, lanes), lambda i, j: (j, i, 0))
    uf = pl.pallas_call(
        _s5_pack_body,
        grid=(t // tm, S5_W // 128),
        in_specs=[tok],
        out_specs=chunked,
        out_shape=jax.ShapeDtypeStruct((S5_GROUPS, rows, lanes), BF16),
        compiler_params=_cparams("parallel", "parallel"),
        name="s5_pack",
    )(u)
    grp = lambda shape: pl.BlockSpec((1,) + shape, lambda g: (g, 0, 0))
    yf = pl.pallas_call(
        functools.partial(_s5_body, bsz=bsz),
        grid=(S5_GROUPS,),
        in_specs=[grp((rows, lanes)), grp((lanes, lanes)), grp((lanes, 2 * S5_STATE)),
                  grp((2 * S5_STATE, lanes)), grp(a_pows.shape[1:])],
        out_specs=grp((rows, lanes)),
        out_shape=jax.ShapeDtypeStruct((S5_GROUPS, rows, lanes), F32),
        scratch_shapes=[pltpu.VMEM((rows, S5_STATE), F32)] * 4,
        compiler_params=_cparams("parallel"),
        name="s5_conv",
    )(uf, m_intra, w_state, w_inter, a_pows)
    return pl.pallas_call(
        _s5_unpack_body,
        grid=(t // tm, S5_W // 128),
        in_specs=[chunked],
        out_specs=tok,
        out_shape=jax.ShapeDtypeStruct((t, S5_W), F32),
        compiler_params=_cparams("parallel", "parallel"),
        name="s5_unpack",
    )(yf)


def _gla_body(q_ref, k_ref, g_ref, v_ref, r_ref, ng_ref, y_ref, s_ref, bc_ref, o_ref):
    tl = q_ref.shape[0]
    scale = GLA_DK ** -0.5

    @pl.when(pl.program_id(1) == 0)
    def _():
        s_ref[...] = jnp.zeros_like(s_ref)

    row = lax.broadcasted_iota(jnp.int32, (CHUNK, CHUNK), 0)
    col = lax.broadcasted_iota(jnp.int32, (CHUNK, CHUNK), 1)
    tri = col <= row
    tref = col < (row // GLA_SUB) * GLA_SUB
    cum_mat = jnp.concatenate([tri, tref], axis=0).astype(BF16)
    lane_head = lax.broadcasted_iota(jnp.int32, (GLA_SUB, GLA_QK), 1) // GLA_DK
    bd_mask = (lax.broadcasted_iota(jnp.int32, (GLA_QK, GLA_V), 0) // GLA_DK
               == lax.broadcasted_iota(jnp.int32, (GLA_QK, GLA_V), 1) // GLA_DV)
    ones_bd = bd_mask.astype(BF16)
    ones_v = jnp.ones((CHUNK, GLA_V), BF16)

    for c in range(tl // CHUNK):
        r0 = c * CHUNK
        qc = q_ref[pl.ds(r0, CHUNK), :] * scale
        kc = k_ref[pl.ds(r0, CHUNK), :]
        g_parts = _split3_bf16(g_ref[pl.ds(r0, CHUNK), :])
        vb = v_ref[pl.ds(r0, CHUNK), :].astype(BF16)
        br = sum(_dot(cum_mat, part) for part in g_parts)
        bc, ref = br[:CHUNK], br[CHUNK:]
        bc_ref[pl.ds(r0, CHUNK), :] = bc
        qt = qc * jnp.exp(bc - ref)
        state = s_ref[...]
        o = _dot((qt * jnp.exp(ref)).astype(BF16), state.astype(BF16))
        blocks = [o[:GLA_SUB]]
        for i in range(1, CHUNK // GLA_SUB):
            ri = bc[GLA_SUB * i - 1:GLA_SUB * i, :]
            kt = (kc * jnp.exp(jnp.minimum(ri - bc, 0.0))).astype(BF16)
            qi = qt[GLA_SUB * i:GLA_SUB * (i + 1)]
            lhs = jnp.concatenate([jnp.where(lane_head == h, qi, 0.0) for h in range(GLA_HEADS)], axis=0)
            att = _dot_nt(lhs.astype(BF16), kt)
            att = jnp.where(col < GLA_SUB * i, att, 0.0)
            ov = _dot(att.astype(BF16), vb)
            oi = jnp.concatenate([ov[GLA_SUB * h:GLA_SUB * (h + 1), GLA_DV * h:GLA_DV * (h + 1)]
                                  for h in range(GLA_HEADS)], axis=1)
            blocks.append(o[GLA_SUB * i:GLA_SUB * (i + 1)] + oi)
        o_ref[pl.ds(r0, CHUNK), :] = jnp.concatenate(blocks, axis=0)
        bl = bc[CHUNK - 1:CHUNK, :]
        kh = (kc * jnp.exp(bl - bc)).astype(BF16)
        upd = _dot_tn(kh, vb)
        dcol = sum(_dot_tn(part, ones_v) for part in g_parts)
        s_ref[...] = jnp.exp(dcol) * state + jnp.where(bd_mask, upd, 0.0)

    q = q_ref[...] * scale
    k = k_ref[...]
    v = v_ref[...]
    bc = bc_ref[...]
    rmod = lax.broadcasted_iota(jnp.int32, (tl, 1), 0) % GLA_SUB
    od = jnp.zeros((tl, GLA_V), F32)
    for d in range(GLA_SUB):
        kd = k if d == 0 else pltpu.roll(k, d, 0)
        bd = bc if d == 0 else pltpu.roll(bc, d, 0)
        vd = v if d == 0 else pltpu.roll(v, d, 0)
        valid = rmod >= d
        e = jnp.exp(jnp.where(valid, bc - bd, 0.0))
        p = jnp.where(valid, q * kd * e, 0.0).astype(BF16)
        od = od + _dot(p, ones_bd) * vd
    o = o_ref[...] + od
    ng = ng_ref[...]
    outs = []
    for h in range(GLA_HEADS):
        oh = o[:, GLA_DV * h:GLA_DV * (h + 1)]
        outs.append(oh * lax.rsqrt(jnp.mean(oh * oh, axis=-1, keepdims=True) + RMS_EPS))
    y_ref[...] = jnp.concatenate(outs, axis=1) * ng * _silu(r_ref[...])


def _gla(q, k, la, v, r, norm_g, bsz, seq):
    tl = SEQ_TILE
    nl = seq // tl
    blk = lambda n: pl.BlockSpec((tl, n), lambda b, l: (b * nl + l, 0))
    return pl.pallas_call(
        _gla_body,
        grid=(bsz, nl),
        in_specs=[blk(GLA_QK), blk(GLA_QK), blk(GLA_QK), blk(GLA_V), blk(GLA_V),
                  pl.BlockSpec((1, GLA_V), lambda b, l: (0, 0))],
        out_specs=blk(GLA_V),
        out_shape=jax.ShapeDtypeStruct((bsz * seq, GLA_V), F32),
        scratch_shapes=[pltpu.VMEM((GLA_QK, GLA_V), F32), pltpu.VMEM((tl, GLA_QK), F32),
                        pltpu.VMEM((tl, GLA_V), F32)],
        compiler_params=_cparams("parallel", "arbitrary"),
        name="gla",
    )(q, k, la, v, r, norm_g.reshape(1, GLA_V))


def _residual_ln(x, y, mod_ref, gate_row, lg_ref, lb_ref):
    return _layer_norm(ALPHA * x + (1.0 + mod_ref[0, gate_row:gate_row + 1, :]) * y, lg_ref[...], lb_ref[...])


def _store_both_layouts(o_ref, o3_ref, x):
    o_ref[...] = x
    _store_row_tiles(o3_ref, x)


def _ab_out_body(ys_ref, u_ref, yb_ref, x_ref, mod_ref, d_ref, gw_ref, gb_ref, wa_ref, wb_ref, lg_ref, lb_ref, o_ref, o3_ref):
    z = _gelu_tanh(ys_ref[...] + d_ref[...] * u_ref[...])
    ya = z * _sigmoid(_dot(z.astype(BF16), gw_ref[...]) + gb_ref[...])
    y = _dot(ya.astype(BF16), wa_ref[...]) + _dot(yb_ref[...].astype(BF16), wb_ref[...])
    _store_both_layouts(o_ref, o3_ref, _residual_ln(x_ref[...], y, mod_ref, 2, lg_ref, lb_ref))


def _ml_out_body(y_ref, x_ref, mod_ref, w_ref, lg_ref, lb_ref, o_ref, o3_ref):
    y = _dot(y_ref[...].astype(BF16), w_ref[...])
    _store_both_layouts(o_ref, o3_ref, _residual_ln(x_ref[...], y, mod_ref, 2, lg_ref, lb_ref))


def _out_call(body, name, row_inputs, x, mod6, consts, ln_g, ln_b, seq):
    t = x.shape[0]
    tm = ROW_TILE
    per_b = seq // tm
    row = lambda a: pl.BlockSpec((tm, a.shape[1]), lambda i: (i, 0))
    full = lambda a: pl.BlockSpec(a.shape, lambda i: (0,) * a.ndim)
    lg, lb = ln_g.reshape(1, D_MODEL), ln_b.reshape(1, D_MODEL)
    return pl.pallas_call(
        body,
        grid=(t // tm,),
        in_specs=([row(a) for a in row_inputs] + [row(x), pl.BlockSpec((1, 6, D_MODEL), lambda i: (i // per_b, 0, 0))]
                  + [full(a) for a in consts] + [full(lg), full(lb)]),
        out_specs=[row(x), pl.BlockSpec((tm, N_LANE_GROUPS, 128), lambda i: (i, 0, 0))],
        out_shape=[jax.ShapeDtypeStruct((t, D_MODEL), F32), jax.ShapeDtypeStruct((t, N_LANE_GROUPS, 128), F32)],
        compiler_params=_cparams("parallel"),
        name=name,
    )(*row_inputs, x, mod6, *consts, lg, lb)


def _ml_in_body(x_ref, mod_ref, w_ref, wt_ref, gb_ref, gbt_ref, qk_ref, v_ref, o_ref, gt_ref, vt_ref, gtt_ref):
    x = x_ref[...]
    h = (x * (1.0 + mod_ref[0, 1:2, :]) + mod_ref[0, 0:1, :]).astype(BF16)
    qk_ref[...] = _dot(h, w_ref[:, 0:2 * ML_W])
    v_ref[...] = _dot(h, w_ref[:, 2 * ML_W:3 * ML_W])
    o_ref[...] = _dot(h, w_ref[:, 3 * ML_W:4 * ML_W])
    gt_ref[...] = _dot(h, w_ref[:, 4 * ML_W:ML_IN_PAD]) + gb_ref[...]
    vt_ref[...] = _dot_nt(wt_ref[0:ML_W, :], h)
    gtt_ref[...] = _dot_nt(wt_ref[ML_W:, :], h) + gbt_ref[...]


def _ml_in(x, mod6, w_in, igate_b, fgate_b, seq):
    t = x.shape[0]
    tm = ROW_TILE
    per_b = seq // tm
    n_gate = ML_IN_PAD - 4 * ML_W
    w = jnp.zeros((D_MODEL, ML_IN_PAD), BF16).at[:, :ML_IN].set(w_in.astype(BF16))
    wt = w[:, 2 * ML_W:3 * ML_W].T
    wt = jnp.concatenate([wt, w[:, 4 * ML_W:].T], axis=0)
    gb = jnp.zeros((1, n_gate), F32).at[0, :2 * ML_HEADS].set(jnp.concatenate([igate_b, fgate_b]))
    row = lambda n: pl.BlockSpec((tm, n), lambda i: (i, 0))
    colblk = lambda n: pl.BlockSpec((n, tm), lambda i: (0, i))
    full = lambda a: pl.BlockSpec(a.shape, lambda i: (0,) * a.ndim)
    outs = [2 * ML_W, ML_W, ML_W, n_gate]
    return pl.pallas_call(
        _ml_in_body,
        grid=(t // tm,),
        in_specs=[row(D_MODEL), pl.BlockSpec((1, 6, D_MODEL), lambda i: (i // per_b, 0, 0)), full(w), full(wt),
                  full(gb), pl.BlockSpec((n_gate, 1), lambda i: (0, 0))],
        out_specs=[row(n) for n in outs] + [colblk(ML_W), colblk(n_gate)],
        out_shape=([jax.ShapeDtypeStruct((t, n), F32) for n in outs]
                   + [jax.ShapeDtypeStruct((ML_W, t), F32), jax.ShapeDtypeStruct((n_gate, t), F32)]),
        compiler_params=_cparams("parallel"),
        name="ml_in",
    )(x, mod6, w, wt, gb, gb.reshape(n_gate, 1))


def _mlstm_body(q_ref, k_ref, v_ref, og_ref, gt_ref, vt_ref, gtt_ref, wq_ref, wk_ref, bq_ref, bk_ref, ng_ref,
                y_ref, cq_ref, ck_ref, qs_ref, ks_ref, s_ref, n_ref, m_ref):
    tl = q_ref.shape[0]

    @pl.when(pl.program_id(1) == 0)
    def _():
        for ref in (s_ref, n_ref, m_ref, cq_ref, ck_ref):
            ref[...] = jnp.zeros_like(ref)

    def conv(x_ref, carry_ref, w_ref, b_ref, out_ref, scale):
        for h in range(ML_HEADS):
            cols = slice(ML_DH * h, ML_DH * (h + 1))
            x = x_ref[:, cols]
            xin = jnp.concatenate([carry_ref[:, cols], x], axis=0)
            acc = jnp.zeros((tl, ML_DH), F32) + b_ref[:, cols]
            for i in range(CONV_W):
                sh = CONV_W - 1 - i
                xs = xin if sh == 0 else pltpu.roll(xin, sh, 0)
                acc = acc + w_ref[i:i + 1, cols] * xs[8:8 + tl]
            carry_ref[:, cols] = x[tl - 8:tl]
            out_ref[:, cols] = _silu(acc) * scale

    conv(q_ref, cq_ref, wq_ref, bq_ref, qs_ref, 1.0)
    conv(k_ref, ck_ref, wk_ref, bk_ref, ks_ref, ML_DH ** -0.5)

    cs = ML_CHUNK
    row = lax.broadcasted_iota(jnp.int32, (cs, cs), 0)
    col = lax.broadcasted_iota(jnp.int32, (cs, cs), 1)
    causal = col <= row
    tri = causal.astype(BF16)
    tri_t = (row <= col).astype(BF16)

    for c in range(tl // cs):
        rows = slice(cs * c, cs * (c + 1))
        gt = gt_ref[rows, :]
        gt_t = gtt_ref[:, rows]
        lf_c = _log_sigmoid(gt)
        lf_r = _log_sigmoid(gt_t[ML_HEADS:2 * ML_HEADS])
        b_c_all = sum(_dot(tri, part) for part in _split3_bf16(lf_c))
        b_r_all = sum(_dot(part, tri_t) for part in _split3_bf16(lf_r))
        per_head = lambda f: jnp.stack([f(h) for h in range(ML_HEADS)])
        head_cols = lambda h: slice(ML_DH * h, ML_DH * (h + 1))
        q3 = per_head(lambda h: qs_ref[rows, head_cols(h)])
        k3 = per_head(lambda h: ks_ref[rows, head_cols(h)])
        vb = per_head(lambda h: v_ref[rows, head_cols(h)]).astype(BF16)
        vtb = per_head(lambda h: vt_ref[head_cols(h), rows]).astype(BF16)
        i_c = per_head(lambda h: gt[:, h:h + 1])
        b_c = per_head(lambda h: b_c_all[:, ML_HEADS + h:ML_HEADS + h + 1])
        i_r = per_head(lambda h: gt_t[h:h + 1, :])
        b_r = per_head(lambda h: b_r_all[h:h + 1, :])
        m = m_ref[:, :, 0:1]
        dmat = jnp.where(causal[None], b_c - b_r + i_r, -jnp.inf)
        inter_log = b_c + m
        m_t = jnp.maximum(inter_log, jnp.max(dmat, axis=2, keepdims=True))
        w_intra = jnp.exp(dmat - m_t)
        w_inter = jnp.exp(inter_log - m_t)
        qb = q3.astype(BF16)
        s = jnp.einsum('htd,hsd->hts', qb, k3.astype(BF16), preferred_element_type=F32) * w_intra
        state, nrm = s_ref[...], n_ref[...]
        num = (w_inter * jnp.einsum('htd,hvd->htv', qb, state.astype(BF16), preferred_element_type=F32)
               + jnp.einsum('hts,hsv->htv', s.astype(BF16), vb, preferred_element_type=F32))
        den = w_inter * jnp.sum(q3 * nrm, axis=2, keepdims=True) + jnp.sum(s, axis=2, keepdims=True)
        hc = num / jnp.maximum(jnp.abs(den), jnp.exp(-m_t))
        hc = hc * lax.rsqrt(jnp.mean(hc * hc, axis=2, keepdims=True) + RMS_EPS)
        for h in range(ML_HEADS):
            cols = head_cols(h)
            y_ref[rows, cols] = _sigmoid(og_ref[rows, cols]) * (hc[h] * ng_ref[:, cols])
        b_last = b_c[:, cs - 1:cs, :]
        gs_c = b_last - b_c + i_c
        gs_r = b_last - b_r + i_r
        m_new = jnp.maximum(b_last + m, jnp.max(gs_r, axis=2, keepdims=True))
        dec = jnp.exp(b_last + m - m_new)
        kw = k3 * jnp.exp(gs_c - m_new)
        s_ref[...] = dec * state + jnp.einsum('hvt,htd->hvd', vtb, kw.astype(BF16), preferred_element_type=F32)
        n_ref[...] = dec * nrm + jnp.sum(kw, axis=1, keepdims=True)
        m_ref[...] = jnp.broadcast_to(m_new, m_ref.shape)


def _mlstm(qk, v, og, gates, v_t, gates_t, conv_w, conv_b, norm_g, bsz, seq):
    tl = SEQ_TILE
    nl = seq // tl
    blk = lambda n, off: pl.BlockSpec((tl, n), lambda b, l: (b * nl + l, off))
    blk_t = lambda n: pl.BlockSpec((n, tl), lambda b, l: (0, b * nl + l))
    par = lambda rows, off: pl.BlockSpec((rows, ML_W), lambda b, l: (0, off))
    state = [pltpu.VMEM((ML_HEADS, ML_DH, ML_DH), F32), pltpu.VMEM((ML_HEADS, 1, ML_DH), F32),
             pltpu.VMEM((ML_HEADS, 1, ML_DH), F32)]
    return pl.pallas_call(
        _mlstm_body,
        grid=(bsz, nl),
        in_specs=[blk(ML_W, 0), blk(ML_W, 1), blk(ML_W, 0), blk(ML_W, 0), blk(gates.shape[1], 0),
                  blk_t(ML_W), blk_t(gates_t.shape[0]),
                  par(CONV_W, 0), par(CONV_W, 1), par(1, 0), par(1, 1), par(1, 0)],
        out_specs=blk(ML_W, 0),
        out_shape=jax.ShapeDtypeStruct((bsz * seq, ML_W), F32),
        scratch_shapes=[pltpu.VMEM((8, ML_W), F32), pltpu.VMEM((8, ML_W), F32),
                        pltpu.VMEM((tl, ML_W), F32), pltpu.VMEM((tl, ML_W), F32)] + state,
        compiler_params=_cparams("parallel", "arbitrary"),
        name="mlstm",
    )(qk, qk, v, og, gates, v_t, gates_t, conv_w, conv_w, conv_b.reshape(1, -1), conv_b.reshape(1, -1),
      norm_g.reshape(1, ML_W))


def _router_body(x_ref, mod_ref, rw_ref, rb_ref, info_ref, cnt_ref, carry_ref):
    tr = x_ref.shape[0]

    @pl.when(pl.program_id(0) == 0)
    def _():
        carry_ref[...] = jnp.zeros_like(carry_ref)

    h = x_ref[...] * (1.0 + mod_ref[0, 4:5, :]) + mod_ref[0, 3:4, :]
    logits = lax.dot_general(rw_ref[...], h, (((1,), (1,)), ((), ())), precision=lax.Precision.HIGHEST,
                             preferred_element_type=F32)
    aff = _sigmoid(logits)
    sel = aff + rb_ref[...]
    s = [sel[e:e + 1, :] for e in range(N_EXPERTS)]
    a = [aff[e:e + 1, :] for e in range(N_EXPERTS)]

    def top2_sum(v):
        best = v[0] + v[1]
        for lo_i, hi_i in zip(PAIR_LO[1:], PAIR_HI[1:]):
            best = jnp.maximum(best, v[lo_i] + v[hi_i])
        return best

    score = [top2_sum(s[EPG * g:EPG * (g + 1)]) for g in range(N_GROUPS)]
    g_idx = jnp.zeros((1, tr), jnp.int32)
    best = score[0]
    for g in range(1, N_GROUPS):
        upd = score[g] > best
        g_idx = jnp.where(upd, g, g_idx)
        best = jnp.where(upd, score[g], best)

    def pick_group(rows, j):
        out = rows[j]
        for g in range(1, N_GROUPS):
            out = jnp.where(g_idx == g, rows[EPG * g + j], out)
        return out

    sg = [pick_group(s, j) for j in range(EPG)]
    ag = [pick_group(a, j) for j in range(EPG)]
    first = jnp.zeros((1, tr), jnp.int32)
    best = sg[0]
    for j in range(1, EPG):
        upd = sg[j] > best
        first = jnp.where(upd, j, first)
        best = jnp.where(upd, sg[j], best)
    second = jnp.zeros((1, tr), jnp.int32)
    best2 = jnp.full((1, tr), -jnp.inf, F32)
    for j in range(EPG):
        upd = (first != j) & (sg[j] > best2)
        second = jnp.where(upd, j, second)
        best2 = jnp.where(upd, sg[j], best2)
    lo = jnp.minimum(first, second)
    hi = jnp.maximum(first, second)
    pair = jnp.where(lo == 0, hi - 1, jnp.where(lo == 1, hi + 1, N_PAIRS - 1))
    cls = g_idx * N_PAIRS + pair

    def pick_local(rows, idx):
        out = rows[0]
        for j in range(1, EPG):
            out = jnp.where(idx == j, rows[j], out)
        return out

    a_lo, a_hi = pick_local(ag, lo), pick_local(ag, hi)
    tot = a_lo + a_hi
    onehot = (lax.broadcasted_iota(jnp.int32, (32, tr), 0) == cls).astype(F32)
    before = (lax.broadcasted_iota(jnp.int32, (tr, tr), 0) < lax.broadcasted_iota(jnp.int32, (tr, tr), 1)).astype(BF16)
    cum = _dot(onehot.astype(BF16), before) + carry_ref[:, 0:1]
    rank = jnp.sum(onehot * cum, axis=0, keepdims=True)
    carry_ref[...] = carry_ref[...] + jnp.sum(onehot, axis=1, keepdims=True)
    cnt_ref[...] = carry_ref[...]
    zero = jnp.zeros((1, tr), F32)
    info_ref[...] = jnp.concatenate([cls.astype(F32), rank, a_lo / tot, a_hi / tot, zero, zero, zero, zero], axis=0)


def _router(x, mod6, router_w, router_bias, seq):
    t = x.shape[0]
    tr = ROUTE_TILE
    per_b = seq // tr
    return pl.pallas_call(
        _router_body,
        grid=(t // tr,),
        in_specs=[pl.BlockSpec((tr, D_MODEL), lambda i: (i, 0)),
                  pl.BlockSpec((1, 6, D_MODEL), lambda i: (i // per_b, 0, 0)),
                  pl.BlockSpec((N_EXPERTS, D_MODEL), lambda i: (0, 0)),
                  pl.BlockSpec((N_EXPERTS, 1), lambda i: (0, 0))],
        out_specs=[pl.BlockSpec((8, tr), lambda i: (0, i)), pl.BlockSpec((32, 128), lambda i: (0, 0))],
        out_shape=[jax.ShapeDtypeStruct((8, t), F32), jax.ShapeDtypeStruct((32, 128), F32)],
        scratch_shapes=[pltpu.VMEM((32, 128), F32)],
        compiler_params=_cparams("arbitrary"),
        name="router",
    )(x, mod6, router_w.T.astype(F32), router_bias.reshape(N_EXPERTS, 1).astype(F32))


def _from_row_tiles(ref):
    return jnp.concatenate([ref[:, j, :] for j in range(N_LANE_GROUPS)], axis=1)


def _store_row_tiles(ref, x):
    for j in range(N_LANE_GROUPS):
        ref[:, j, :] = x[:, 128 * j:128 * (j + 1)]


def _retile_body(x_ref, o_ref):
    o_ref[...] = _from_row_tiles(x_ref)


def _retile(x3):
    t = x3.shape[0]
    tm = PERM_TILE
    return pl.pallas_call(
        _retile_body,
        grid=(t // tm,),
        in_specs=[pl.BlockSpec((tm, N_LANE_GROUPS, 128), lambda i: (i, 0, 0))],
        out_specs=pl.BlockSpec((tm, D_MODEL), lambda i: (i, 0)),
        out_shape=jax.ShapeDtypeStruct((t, D_MODEL), F32),
        compiler_params=_cparams("parallel"),
        name="retile",
    )(x3)


def _moe_body(e_lo_ref, e_hi_ref, blk_ref, new_ref, nv_ref, used_ref, src_ref,
              x_hbm, meta_ref, mod_ref, wg_lo32, wu_lo32, wd_lo32, wg_hi32, wu_hi32, wd_hi32, lg_ref, lb_ref,
              out_hbm, wg_lo, wu_lo, wd_lo, wg_hi, wu_hi, wd_hi, xbuf, obuf, sem_in, sem_out):
    del e_lo_ref, e_hi_ref, blk_ref
    i = pl.program_id(0)
    used = used_ref[0]
    tm = xbuf.shape[1]
    slot = i % 2
    per_phase = tm // MOE_DMA_PHASES

    def fetch_row(tile, sl, r):
        return pltpu.make_async_copy(x_hbm.at[src_ref[tile * tm + r]], xbuf.at[sl, r], sem_in.at[sl])

    def send_row(tile, sl, r):
        return pltpu.make_async_copy(obuf.at[sl, r], out_hbm.at[src_ref[tile * tm + r]], sem_out.at[sl])

    def rows_loop(make, action, lo, hi):
        def one(r, c):
            getattr(make(r), action)()
            return c
        lax.fori_loop(lo, hi, one, 0)

    def start_fetch(tile, sl, lo, hi):
        rows_loop(lambda r: fetch_row(tile, sl, r), "start", lo, hi)

    def wait_fetch(tile, sl):
        rows_loop(lambda r: fetch_row(tile, sl, r), "wait", 0, tm)

    def start_send(tile, sl, lo, hi):
        rows_loop(lambda r: send_row(tile, sl, r), "start", lo, jnp.minimum(hi, nv_ref[tile]))

    def wait_send(tile, sl):
        rows_loop(lambda r: send_row(tile, sl, r), "wait", 0, nv_ref[tile])

    def dma_phase(p):
        lo, hi = p * per_phase, (p + 1) * per_phase

        @pl.when(i + 1 < used)
        def _():
            start_fetch(i + 1, 1 - slot, lo, hi)

        @pl.when(i > 0)
        def _():
            start_send(i - 1, 1 - slot, lo, hi)

    @pl.when(new_ref[i] == 1)
    def _():
        for src, dst in ((wg_lo32, wg_lo), (wu_lo32, wu_lo), (wd_lo32, wd_lo),
                         (wg_hi32, wg_hi), (wu_hi32, wu_hi), (wd_hi32, wd_hi)):
            dst[...] = src[...].astype(BF16)

    @pl.when(i == 0)
    def _():
        start_fetch(0, 0, 0, tm)

    @pl.when(i < used)
    def _():
        wait_fetch(i, slot)

        @pl.when(i > 1)
        def _():
            wait_send(i - 2, slot)

        x = _from_row_tiles(xbuf.at[slot])
        meta = meta_ref[...]
        bid, g_lo, g_hi = meta[:, 0:1], meta[:, 1:2], meta[:, 2:3]
        nb = mod_ref.shape[0]

        def per_row(r):
            out = mod_ref[0, r:r + 1, :]
            for b in range(1, nb):
                out = jnp.where(bid == float(b), mod_ref[b, r:r + 1, :], out)
            return out

        dma_phase(0)
        h = (x * (1.0 + per_row(4)) + per_row(3)).astype(BF16)
        dma_phase(1)
        he_lo = (_silu(_dot(h, wg_lo[...])) * _dot(h, wu_lo[...])).astype(BF16)
        dma_phase(2)
        y_lo = _dot(he_lo, wd_lo[...])
        dma_phase(3)
        he_hi = (_silu(_dot(h, wg_hi[...])) * _dot(h, wu_hi[...])).astype(BF16)
        dma_phase(4)
        y_hi = _dot(he_hi, wd_hi[...])
        dma_phase(5)
        y = g_lo * y_lo + g_hi * y_hi
        out = _layer_norm(ALPHA * x + (1.0 + per_row(5)) * y, lg_ref[...], lb_ref[...])
        dma_phase(6)
        _store_row_tiles(obuf.at[slot], out)
        dma_phase(7)

        @pl.when(i == used - 1)
        def _():
            @pl.when(i > 0)
            def _():
                wait_send(i - 1, 1 - slot)

            start_send(i, slot, 0, tm)
            wait_send(i, slot)


def _moe_tiles(e_lo, e_hi, blk, new, n_valid, used, src, x3, meta, mod6, layer, w_gate, w_up, w_down, ln_g, ln_b):
    t = x3.shape[0]
    tm = ROW_TILE
    n_tiles = e_lo.shape[0]
    wspec = lambda shape, which: pl.BlockSpec((None, None) + shape, lambda i, *s: (layer, s[which][i], 0, 0))
    full = lambda a: pl.BlockSpec(a.shape, lambda i, *s: (0,) * a.ndim)
    lg, lb = ln_g.reshape(1, D_MODEL), ln_b.reshape(1, D_MODEL)
    gu, dn = (D_MODEL, D_EXPERT), (D_EXPERT, D_MODEL)
    return pl.pallas_call(
        _moe_body,
        grid_spec=pltpu.PrefetchScalarGridSpec(
            num_scalar_prefetch=7,
            grid=(n_tiles,),
            in_specs=[pl.BlockSpec(memory_space=pl.ANY), pl.BlockSpec((tm, 8), lambda i, *s: (s[2][i], 0)), full(mod6),
                      wspec(gu, 0), wspec(gu, 0), wspec(dn, 0), wspec(gu, 1), wspec(gu, 1), wspec(dn, 1),
                      full(lg), full(lb)],
            out_specs=pl.BlockSpec(memory_space=pl.ANY),
            scratch_shapes=([pltpu.VMEM(gu, BF16), pltpu.VMEM(gu, BF16), pltpu.VMEM(dn, BF16)] * 2
                            + [pltpu.VMEM((2, tm, N_LANE_GROUPS, 128), F32)] * 2
                            + [pltpu.SemaphoreType.DMA((2,))] * 2)),
        out_shape=jax.ShapeDtypeStruct((t, N_LANE_GROUPS, 128), F32),
        compiler_params=_cparams("arbitrary"),
        name="moe_tiles",
    )(e_lo, e_hi, blk, new, n_valid, used, src, x3, meta, mod6, w_gate, w_up, w_down, w_gate, w_up, w_down, lg, lb)


def _moe_layer(x, x3, mod6, router_w, router_bias, layer, w_gate, w_up, w_down, ln_g, ln_b, bsz, seq):
    t = x.shape[0]
    tm = ROW_TILE
    n_tiles = t // tm + N_CLASSES
    tp = n_tiles * tm
    info, counts = _router(x, mod6, router_w, router_bias, seq)
    cls = info[0].astype(jnp.int32)
    rank = info[1].astype(jnp.int32)
    cnt = counts[:N_CLASSES, 0].astype(jnp.int32)
    tiles_c = (cnt + tm - 1) // tm
    tile_end = jnp.cumsum(tiles_c)
    offs = (tile_end - tiles_c) * tm
    onehot = cls[:, None] == jnp.arange(N_CLASSES)[None, :]
    dest = rank + jnp.sum(jnp.where(onehot, offs[None, :], 0), axis=1)
    used = tile_end[-1]
    tile_id = jnp.arange(n_tiles)
    blk = jnp.maximum(jnp.minimum(tile_id, used - 1), 0)
    tile_cls = jnp.minimum(jnp.sum(blk[:, None] >= tile_end[None, :], axis=1), N_CLASSES - 1)
    grp, pair = tile_cls // N_PAIRS, tile_cls % N_PAIRS
    e_lo = (grp * EPG + jnp.asarray(PAIR_LO, jnp.int32)[pair]).astype(jnp.int32)
    e_hi = (grp * EPG + jnp.asarray(PAIR_HI, jnp.int32)[pair]).astype(jnp.int32)
    new = jnp.concatenate([jnp.ones((1,), jnp.int32), (tile_cls[1:] != tile_cls[:-1]).astype(jnp.int32)])
    tile_in_cls = blk - (tile_end - tiles_c)[tile_cls]
    n_valid = jnp.where(tile_id < used, jnp.clip(cnt[tile_cls] - tile_in_cls * tm, 0, tm), 0).astype(jnp.int32)
    bid = (jnp.arange(t) // seq).astype(F32)
    meta_nat = jnp.stack([bid, info[2], info[3], jnp.arange(t, dtype=F32)] + [jnp.zeros((t,), F32)] * 4, axis=1)
    meta = jnp.zeros((tp, 8), F32).at[dest].set(meta_nat)
    src = meta[:, 3].astype(jnp.int32)
    out3 = _moe_tiles(e_lo, e_hi, blk.astype(jnp.int32), new, n_valid, used.reshape(1).astype(jnp.int32), src,
                      x3, meta, mod6, layer, w_gate, w_up, w_down, ln_g, ln_b)
    return _retile(out3)


def kernel(x, c, router_w, router_bias, ada_w, ada_b, ln1_g, ln1_b, ln2_g, ln2_b, moe_w_gate, moe_w_up, moe_w_down,
           ab_w_in, s5_lam_re, s5_lam_im, s5_log_dt, s5_b_re, s5_b_im, s5_c_re, s5_c_im, s5_d, s5_glu_w, s5_glu_b,
           gla_gate_w, gla_gate_b, gla_norm_g, ab_w_out, ml_w_in, ml_conv_w, ml_conv_b, ml_igate_b, ml_fgate_b,
           ml_norm_g, ml_w_out):
    bsz, seq, d = x.shape
    t = bsz * seq
    mod = _ada_mod(c, ada_w, ada_b)
    xt = x.reshape(t, d)
    for layer in range(DEPTH):
        j = layer // 2
        mod6 = mod[layer]
        if layer % 2 == 0:
            u, q, k, v, r, la = _ab_in(xt, mod6, ab_w_in[j], gla_gate_w[j], gla_gate_b[j], seq)
            tables = _s5_prep(s5_lam_re[j], s5_lam_im[j], s5_log_dt[j], s5_b_re[j], s5_b_im[j], s5_c_re[j], s5_c_im[j])
            ys = _s5_conv(u, tables, bsz, seq)
            yb = _gla(q, k, la, v, r, gla_norm_g[j], bsz, seq)
            w_out = ab_w_out[j].astype(BF16)
            consts = [s5_d[j].reshape(1, S5_W), s5_glu_w[j].astype(BF16), s5_glu_b[j].reshape(1, S5_W),
                      w_out[:S5_W], w_out[S5_W:]]
            xt, xt3 = _out_call(_ab_out_body, "ab_out", [ys, u, yb], xt, mod6, consts, ln1_g[layer], ln1_b[layer], seq)
        else:
            qk, v, og, gates, v_t, gates_t = _ml_in(xt, mod6, ml_w_in[j], ml_igate_b[j], ml_fgate_b[j], seq)
            y = _mlstm(qk, v, og, gates, v_t, gates_t, ml_conv_w[j], ml_conv_b[j], ml_norm_g[j], bsz, seq)
            xt, xt3 = _out_call(_ml_out_body, "ml_out", [y], xt, mod6, [ml_w_out[j].astype(BF16)],
                                ln1_g[layer], ln1_b[layer], seq)
        xt = _moe_layer(xt, xt3, mod6, router_w, router_bias, layer, moe_w_gate, moe_w_up, moe_w_down,
                        ln2_g[layer], ln2_b[layer], bsz, seq)
    return xt.reshape(bsz, seq, d)
```

```python
import functools
import math

import jax
import jax.numpy as jnp
from jax import lax
from jax.experimental import pallas as pl
from jax.experimental.pallas import tpu as pltpu

F32 = jnp.float32
BF16 = jnp.bfloat16

D_MODEL = 1024
DEPTH = 4
S5_W = 512
S5_GROUP = 16
S5_GROUPS = 32
S5_STATE = 64
S5_CHUNK = 16
S5_SCAN_ROWS = 8
GLA_HEADS = 4
GLA_DV = 128
GLA_DK = 64
GLA_QK = 256
GLA_V = 512
GLA_RANK = 16
GLA_GATE_NORM = 16.0
GLA_SUB = 16
AB_IN = S5_W + 2 * GLA_QK + 2 * GLA_V + GLA_RANK
AB_IN_PAD = 2176
ML_HEADS = 8
ML_W = 1024
ML_DH = 128
ML_IN = 4 * ML_W + 2 * ML_HEADS
ML_IN_PAD = 4224
CONV_W = 4
CHUNK = 64
ML_CHUNK = 128
N_EXPERTS = 16
N_GROUPS = 4
EPG = 4
D_EXPERT = 256
N_PAIRS = 6
N_CLASSES = N_GROUPS * N_PAIRS
PAIR_LO = (0, 0, 0, 1, 1, 2)
PAIR_HI = (1, 2, 3, 2, 3, 3)
ALPHA = (2.0 * DEPTH) ** 0.25
LN_EPS = 1e-5
RMS_EPS = 1e-6

V7X_VMEM_BYTES = 64 * 1024 * 1024
VMEM_LIMIT = (V7X_VMEM_BYTES * 3) // 4
ROW_TILE = 256
SEQ_TILE = 256
ROUTE_TILE = 512
PERM_TILE = 512
assert ROUTE_TILE == PERM_TILE
N_LANE_GROUPS = D_MODEL // 128


def _cparams(*sem):
    return pltpu.CompilerParams(dimension_semantics=sem, vmem_limit_bytes=VMEM_LIMIT)


def _sigmoid(x):
    return 1.0 / (1.0 + jnp.exp(-x))


def _silu(x):
    return x * _sigmoid(x)


def _log_sigmoid(z):
    return jnp.minimum(z, 0.0) - jnp.log1p(jnp.exp(-jnp.abs(z)))


def _gelu_tanh(x):
    return 0.5 * x * (1.0 + jnp.tanh(math.sqrt(2.0 / math.pi) * (x + 0.044715 * (x * x * x))))


def _layer_norm(x, g, b):
    mu = jnp.mean(x, axis=-1, keepdims=True)
    xc = x - mu
    var = jnp.mean(xc * xc, axis=-1, keepdims=True)
    return xc * lax.rsqrt(var + LN_EPS) * g + b


def _split3_bf16(x):
    hi = x.astype(BF16)
    r = x - hi.astype(F32)
    mid = r.astype(BF16)
    lo = (r - mid.astype(F32)).astype(BF16)
    return hi, mid, lo


def _dot(a, b):
    return jnp.dot(a, b, preferred_element_type=F32)


def _dot_nt(a, b):
    return lax.dot_general(a, b, (((1,), (1,)), ((), ())), preferred_element_type=F32)


def _dot_tn(a, b):
    return lax.dot_general(a, b, (((0,), (0,)), ((), ())), preferred_element_type=F32)


def _ada_body(c_ref, w_ref, b_ref, o_ref):
    c = c_ref[...]
    o_ref[0] = _dot(_silu(c), w_ref[0]) + b_ref[0]


def _ada_mod(c, ada_w, ada_b):
    bsz = c.shape[0]
    tn = 1536
    cp = jnp.zeros((8, D_MODEL), F32).at[:bsz].set(c)
    out = pl.pallas_call(
        _ada_body,
        grid=(DEPTH, 6 * D_MODEL // tn),
        in_specs=[pl.BlockSpec((8, D_MODEL), lambda l, j: (0, 0)),
                  pl.BlockSpec((1, D_MODEL, tn), lambda l, j: (l, 0, j)),
                  pl.BlockSpec((1, 1, tn), lambda l, j: (l, 0, j))],
        out_specs=pl.BlockSpec((1, 8, tn), lambda l, j: (l, 0, j)),
        out_shape=jax.ShapeDtypeStruct((DEPTH, 8, 6 * D_MODEL), F32),
        compiler_params=_cparams("parallel", "parallel"),
        name="ada_mod",
    )(cp, ada_w, ada_b.reshape(DEPTH, 1, 6 * D_MODEL))
    return out[:, :bsz].reshape(DEPTH, bsz, 6, D_MODEL)


def _ab_in_body(x_ref, mod_ref, w_ref, gw_ref, gb_ref, u_ref, q_ref, k_ref, v_ref, r_ref, la_ref):
    x = x_ref[...]
    h = (x * (1.0 + mod_ref[0, 1:2, :]) + mod_ref[0, 0:1, :]).astype(BF16)

    def seg(a, b):
        return _dot(h, w_ref[:, a:b])

    u_ref[...] = seg(0, 512)
    q_ref[...] = seg(512, 768)
    k_ref[...] = seg(768, 1024)
    v_ref[...] = seg(1024, 1536)
    r_ref[...] = seg(1536, 2048)
    a_lr = seg(2048, AB_IN_PAD)
    z = _dot(a_lr.astype(BF16), gw_ref[...]) + gb_ref[...]
    la_ref[...] = _log_sigmoid(z) * (1.0 / GLA_GATE_NORM)


def _ab_in(x, mod6, w_in, gate_w, gate_b, seq):
    t = x.shape[0]
    tm = ROW_TILE
    per_b = seq // tm
    w = jnp.zeros((D_MODEL, AB_IN_PAD), BF16).at[:, :AB_IN].set(w_in.astype(BF16))
    gw = jnp.zeros((AB_IN_PAD - 2048, GLA_QK), BF16).at[:GLA_RANK].set(gate_w.astype(BF16))
    row = lambda n: pl.BlockSpec((tm, n), lambda i: (i, 0))
    full = lambda a: pl.BlockSpec(a.shape, lambda i: (0,) * a.ndim)
    outs = [S5_W, GLA_QK, GLA_QK, GLA_V, GLA_V, GLA_QK]
    return pl.pallas_call(
        _ab_in_body,
        grid=(t // tm,),
        in_specs=[row(D_MODEL),
                  pl.BlockSpec((1, 6, D_MODEL), lambda i: (i // per_b, 0, 0)),
                  full(w), full(gw), pl.BlockSpec((1, GLA_QK), lambda i: (0, 0))],
        out_specs=[row(n) for n in outs],
        out_shape=[jax.ShapeDtypeStruct((t, n), F32) for n in outs],
        compiler_params=_cparams("parallel"),
        name="ab_in",
    )(x, mod6, w, gw, gate_b.reshape(1, GLA_QK))


def _s5_prep(lam_re, lam_im, log_dt, b_re, b_im, c_re, c_im):
    hp = lax.Precision.HIGHEST
    n = S5_CHUNK
    lre, lim = lam_re.astype(F32), lam_im.astype(F32)
    dt = jnp.exp(log_dt.astype(F32))[:, None]
    d = jnp.arange(n + 1, dtype=F32)[:, None, None]
    mag = jnp.exp(lre * dt * d)
    ang = lim * dt * d
    pw_re, pw_im = mag * jnp.cos(ang), mag * jnp.sin(ang)
    den = lre * lre + lim * lim
    nre, nim = pw_re[1] - 1.0, pw_im[1]
    coef_re = (nre * lre + nim * lim) / den
    coef_im = (nim * lre - nre * lim) / den
    bre, bim = b_re.astype(F32), b_im.astype(F32)
    bb_re = coef_re[..., None] * bre - coef_im[..., None] * bim
    bb_im = coef_re[..., None] * bim + coef_im[..., None] * bre
    cre, cim = c_re.astype(F32), c_im.astype(F32)
    cp_re = cre[None] * pw_re[:, :, None, :] - cim[None] * pw_im[:, :, None, :]
    cp_im = cre[None] * pw_im[:, :, None, :] + cim[None] * pw_re[:, :, None, :]
    kern = (jnp.einsum('dgcp,gpe->dgce', cp_re, bb_re, precision=hp)
            - jnp.einsum('dgcp,gpe->dgce', cp_im, bb_im, precision=hp))
    jj = jnp.arange(n)[:, None]
    tt = jnp.arange(n)[None, :]
    lag = tt - jj
    kg = jnp.where((lag >= 0)[:, :, None, None, None], kern[jnp.clip(lag, 0, n)], 0.0)
    m_intra = jnp.transpose(kg, (2, 0, 4, 1, 3)).reshape(S5_GROUPS, n * S5_GROUP, n * S5_GROUP)
    pws_re, pws_im = pw_re[n - 1 - jnp.arange(n)], pw_im[n - 1 - jnp.arange(n)]
    ws_re = pws_re[..., None] * bb_re[None] - pws_im[..., None] * bb_im[None]
    ws_im = pws_re[..., None] * bb_im[None] + pws_im[..., None] * bb_re[None]
    w_state = jnp.transpose(jnp.concatenate([ws_re, ws_im], axis=2), (1, 0, 3, 2))
    w_state = w_state.reshape(S5_GROUPS, n * S5_GROUP, 2 * S5_STATE)
    wi_re = jnp.transpose(cp_re[1:], (1, 3, 0, 2))
    wi_im = -jnp.transpose(cp_im[1:], (1, 3, 0, 2))
    w_inter = jnp.concatenate([wi_re, wi_im], axis=1).reshape(S5_GROUPS, 2 * S5_STATE, n * S5_GROUP)
    r = (n * jnp.arange(S5_SCAN_ROWS + 1, dtype=F32))[None, :, None]
    cmag, cang = jnp.exp(lre[:, None, :] * dt[:, None, :] * r), lim[:, None, :] * dt[:, None, :] * r
    a_pows = jnp.concatenate([cmag * jnp.cos(cang), cmag * jnp.sin(cang)], axis=1)
    return m_intra.astype(BF16), w_state.astype(BF16), w_inter.astype(BF16), a_pows


def _s5_body(u_ref, m_ref, ws_ref, wi_ref, a_ref, y_ref, ere, eim, sre, sim, *, bsz):
    u = u_ref[0]
    e = _dot(u, ws_ref[0])
    ere[...] = e[:, :S5_STATE]
    eim[...] = e[:, S5_STATE:]
    n_pow = S5_SCAN_ROWS + 1
    pr, pi = a_ref[0, 0:n_pow, :], a_ref[0, n_pow:2 * n_pow, :]
    tab_r, tab_i = pr[0:S5_SCAN_ROWS], pi[0:S5_SCAN_ROWS]
    rowi = lax.broadcasted_iota(jnp.int32, (S5_SCAN_ROWS, S5_STATE), 0)
    nk = ere.shape[0] // bsz

    def cmul(a_r, a_i, x_r, x_i):
        return a_r * x_r - a_i * x_i, a_r * x_i + a_i * x_r

    def shifted(x, s):
        return jnp.where(rowi >= s, pltpu.roll(x, s, 0), 0.0)

    def tile(m, carry):
        out = []
        for b in range(bsz):
            c_r, c_i = carry[2 * b], carry[2 * b + 1]
            r0 = pl.multiple_of(b * nk + m * S5_SCAN_ROWS, S5_SCAN_ROWS)
            x_r, x_i = ere[pl.ds(r0, S5_SCAN_ROWS), :], eim[pl.ds(r0, S5_SCAN_ROWS), :]
            for s in (1, 2, 4):
                d_r, d_i = cmul(pr[s:s + 1], pi[s:s + 1], shifted(x_r, s), shifted(x_i, s))
                x_r, x_i = x_r + d_r, x_i + d_i
            t_r, t_i = cmul(tab_r, tab_i, c_r, c_i)
            sre[pl.ds(r0, S5_SCAN_ROWS), :] = shifted(x_r, 1) + t_r
            sim[pl.ds(r0, S5_SCAN_ROWS), :] = shifted(x_i, 1) + t_i
            n_r, n_i = cmul(pr[S5_SCAN_ROWS:n_pow], pi[S5_SCAN_ROWS:n_pow], c_r, c_i)
            last = slice(S5_SCAN_ROWS - 1, S5_SCAN_ROWS)
            out += [n_r + jnp.broadcast_to(x_r[last], c_r.shape), n_i + jnp.broadcast_to(x_i[last], c_i.shape)]
        return tuple(out)

    z = jnp.zeros((S5_SCAN_ROWS, S5_STATE), F32)
    lax.fori_loop(0, nk // S5_SCAN_ROWS, tile, (z,) * (2 * bsz))
    y = _dot(u, m_ref[0])
    y = y + _dot(sre[...].astype(BF16), wi_ref[0, :S5_STATE, :])
    y = y + _dot(sim[...].astype(BF16), wi_ref[0, S5_STATE:, :])
    y_ref[0] = y


def _s5_pack_body(u_ref, o_ref):
    nk = u_ref.shape[0] // S5_CHUNK
    by_step = [u_ref[pl.ds(t, nk, stride=S5_CHUNK), :] for t in range(S5_CHUNK)]
    for g in range(128 // S5_GROUP):
        cols = slice(S5_GROUP * g, S5_GROUP * (g + 1))
        o_ref[g] = jnp.concatenate([x[:, cols] for x in by_step], axis=1).astype(o_ref.dtype)


def _s5_unpack_body(y_ref, o_ref):
    nk = y_ref.shape[1]
    groups = [y_ref[g] for g in range(128 // S5_GROUP)]
    for t in range(S5_CHUNK):
        cols = slice(S5_GROUP * t, S5_GROUP * (t + 1))
        o_ref[pl.ds(t, nk, stride=S5_CHUNK), :] = jnp.concatenate([y[:, cols] for y in groups], axis=1)


def _s5_conv(u, tables, bsz, seq):
    m_intra, w_state, w_inter, a_pows = tables
    t = bsz * seq
    rows = t // S5_CHUNK
    lanes = S5_CHUNK * S5_GROUP
    tm = PERM_TILE
    gpb = 128 // S5_GROUP
    tok = pl.BlockSpec((tm, 128), lambda i, j: (i, j))
    chunked = pl.BlockSpec((gpb, tm // S5_CHUNK, lanes), lambda i, j: (j, i, 0))
    uf = pl.pallas_call(
        _s5_pack_body,
        grid=(t // tm, S5_W // 128),
        in_specs=[tok],
        out_specs=chunked,
        out_shape=jax.ShapeDtypeStruct((S5_GROUPS, rows, lanes), BF16),
        compiler_params=_cparams("parallel", "parallel"),
        name="s5_pack",
    )(u)
    grp = lambda shape: pl.BlockSpec((1,) + shape, lambda g: (g, 0, 0))
    yf = pl.pallas_call(
        functools.partial(_s5_body, bsz=bsz),
        grid=(S5_GROUPS,),
        in_specs=[grp((rows, lanes)), grp((lanes, lanes)), grp((lanes, 2 * S5_STATE)),
                  grp((2 * S5_STATE, lanes)), grp(a_pows.shape[1:])],
        out_specs=grp((rows, lanes)),
        out_shape=jax.ShapeDtypeStruct((S5_GROUPS, rows, lanes), F32),
        scratch_shapes=[pltpu.VMEM((rows, S5_STATE), F32)] * 4,
        compiler_params=_cparams("parallel"),
        name="s5_conv",
    )(uf, m_intra, w_state, w_inter, a_pows)
    return pl.pallas_call(
        _s5_unpack_body,
        grid=(t // tm, S5_W // 128),
        in_specs=[chunked],
        out_specs=tok,
        out_shape=jax.ShapeDtypeStruct((t, S5_W), F32),
        compiler_params=_cparams("parallel", "parallel"),
        name="s5_unpack",
    )(yf)


def _gla_body(q_ref, k_ref, g_ref, v_ref, r_ref, ng_ref, y_ref, s_ref, bc_ref, o_ref):
    tl = q_ref.shape[0]
    scale = GLA_DK ** -0.5

    @pl.when(pl.program_id(1) == 0)
    def _():
        s_ref[...] = jnp.zeros_like(s_ref)

    row = lax.broadcasted_iota(jnp.int32, (CHUNK, CHUNK), 0)
    col = lax.broadcasted_iota(jnp.int32, (CHUNK, CHUNK), 1)
    tri = col <= row
    tref = col < (row // GLA_SUB) * GLA_SUB
    cum_mat = jnp.concatenate([tri, tref], axis=0).astype(BF16)
    lane_head = lax.broadcasted_iota(jnp.int32, (GLA_SUB, GLA_QK), 1) // GLA_DK
    bd_mask = (lax.broadcasted_iota(jnp.int32, (GLA_QK, GLA_V), 0) // GLA_DK
               == lax.broadcasted_iota(jnp.int32, (GLA_QK, GLA_V), 1) // GLA_DV)
    ones_bd = bd_mask.astype(BF16)
    ones_v = jnp.ones((CHUNK, GLA_V), BF16)

    for c in range(tl // CHUNK):
        r0 = c * CHUNK
        qc = q_ref[pl.ds(r0, CHUNK), :] * scale
        kc = k_ref[pl.ds(r0, CHUNK), :]
        g_parts = _split3_bf16(g_ref[pl.ds(r0, CHUNK), :])
        vb = v_ref[pl.ds(r0, CHUNK), :].astype(BF16)
        br = sum(_dot(cum_mat, part) for part in g_parts)
        bc, ref = br[:CHUNK], br[CHUNK:]
        bc_ref[pl.ds(r0, CHUNK), :] = bc
        qt = qc * jnp.exp(bc - ref)
        state = s_ref[...]
        o = _dot((qt * jnp.exp(ref)).astype(BF16), state.astype(BF16))
        blocks = [o[:GLA_SUB]]
        for i in range(1, CHUNK // GLA_SUB):
            ri = bc[GLA_SUB * i - 1:GLA_SUB * i, :]
            kt = (kc * jnp.exp(jnp.minimum(ri - bc, 0.0))).astype(BF16)
            qi = qt[GLA_SUB * i:GLA_SUB * (i + 1)]
            lhs = jnp.concatenate([jnp.where(lane_head == h, qi, 0.0) for h in range(GLA_HEADS)], axis=0)
            att = _dot_nt(lhs.astype(BF16), kt)
            att = jnp.where(col < GLA_SUB * i, att, 0.0)
            ov = _dot(att.astype(BF16), vb)
            oi = jnp.concatenate([ov[GLA_SUB * h:GLA_SUB * (h + 1), GLA_DV * h:GLA_DV * (h + 1)]
                                  for h in range(GLA_HEADS)], axis=1)
            blocks.append(o[GLA_SUB * i:GLA_SUB * (i + 1)] + oi)
        o_ref[pl.ds(r0, CHUNK), :] = jnp.concatenate(blocks, axis=0)
        bl = bc[CHUNK - 1:CHUNK, :]
        kh = (kc * jnp.exp(bl - bc)).astype(BF16)
        upd = _dot_tn(kh, vb)
        dcol = sum(_dot_tn(part, ones_v) for part in g_parts)
        s_ref[...] = jnp.exp(dcol) * state + jnp.where(bd_mask, upd, 0.0)

    q = q_ref[...] * scale
    k = k_ref[...]
    v = v_ref[...]
    bc = bc_ref[...]
    rmod = lax.broadcasted_iota(jnp.int32, (tl, 1), 0) % GLA_SUB
    od = jnp.zeros((tl, GLA_V), F32)
    for d in range(GLA_SUB):
        kd = k if d == 0 else pltpu.roll(k, d, 0)
        bd = bc if d == 0 else pltpu.roll(bc, d, 0)
        vd = v if d == 0 else pltpu.roll(v, d, 0)
        valid = rmod >= d
        e = jnp.exp(jnp.where(valid, bc - bd, 0.0))
        p = jnp.where(valid, q * kd * e, 0.0).astype(BF16)
        od = od + _dot(p, ones_bd) * vd
    o = o_ref[...] + od
    ng = ng_ref[...]
    outs = []
    for h in range(GLA_HEADS):
        oh = o[:, GLA_DV * h:GLA_DV * (h + 1)]
        outs.append(oh * lax.rsqrt(jnp.mean(oh * oh, axis=-1, keepdims=True) + RMS_EPS))
    y_ref[...] = jnp.concatenate(outs, axis=1) * ng * _silu(r_ref[...])


def _gla(q, k, la, v, r, norm_g, bsz, seq):
    tl = SEQ_TILE
    nl = seq // tl
    blk = lambda n: pl.BlockSpec((tl, n), lambda b, l: (b * nl + l, 0))
    return pl.pallas_call(
        _gla_body,
        grid=(bsz, nl),
        in_specs=[blk(GLA_QK), blk(GLA_QK), blk(GLA_QK), blk(GLA_V), blk(GLA_V),
                  pl.BlockSpec((1, GLA_V), lambda b, l: (0, 0))],
        out_specs=blk(GLA_V),
        out_shape=jax.ShapeDtypeStruct((bsz * seq, GLA_V), F32),
        scratch_shapes=[pltpu.VMEM((GLA_QK, GLA_V), F32), pltpu.VMEM((tl, GLA_QK), F32),
                        pltpu.VMEM((tl, GLA_V), F32)],
        compiler_params=_cparams("parallel", "arbitrary"),
        name="gla",
    )(q, k, la, v, r, norm_g.reshape(1, GLA_V))


def _residual_ln(x, y, mod_ref, gate_row, lg_ref, lb_ref):
    return _layer_norm(ALPHA * x + (1.0 + mod_ref[0, gate_row:gate_row + 1, :]) * y, lg_ref[...], lb_ref[...])


def _ab_out_body(ys_ref, u_ref, yb_ref, x_ref, mod_ref, d_ref, gw_ref, gb_ref, wa_ref, wb_ref, lg_ref, lb_ref, o_ref):
    z = _gelu_tanh(ys_ref[...] + d_ref[...] * u_ref[...])
    ya = z * _sigmoid(_dot(z.astype(BF16), gw_ref[...]) + gb_ref[...])
    y = _dot(ya.astype(BF16), wa_ref[...]) + _dot(yb_ref[...].astype(BF16), wb_ref[...])
    o_ref[...] = _residual_ln(x_ref[...], y, mod_ref, 2, lg_ref, lb_ref)


def _ml_out_body(y_ref, x_ref, mod_ref, w_ref, lg_ref, lb_ref, o_ref):
    y = _dot(y_ref[...].astype(BF16), w_ref[...])
    o_ref[...] = _residual_ln(x_ref[...], y, mod_ref, 2, lg_ref, lb_ref)


def _out_call(body, name, row_inputs, x, mod6, consts, ln_g, ln_b, seq):
    t = x.shape[0]
    tm = ROW_TILE
    per_b = seq // tm
    row = lambda a: pl.BlockSpec((tm, a.shape[1]), lambda i: (i, 0))
    full = lambda a: pl.BlockSpec(a.shape, lambda i: (0,) * a.ndim)
    lg, lb = ln_g.reshape(1, D_MODEL), ln_b.reshape(1, D_MODEL)
    return pl.pallas_call(
        body,
        grid=(t // tm,),
        in_specs=([row(a) for a in row_inputs] + [row(x), pl.BlockSpec((1, 6, D_MODEL), lambda i: (i // per_b, 0, 0))]
                  + [full(a) for a in consts] + [full(lg), full(lb)]),
        out_specs=row(x),
        out_shape=jax.ShapeDtypeStruct((t, D_MODEL), F32),
        compiler_params=_cparams("parallel"),
        name=name,
    )(*row_inputs, x, mod6, *consts, lg, lb)


def _ml_in_body(x_ref, mod_ref, w_ref, wt_ref, gb_ref, gbt_ref, qk_ref, v_ref, o_ref, gt_ref, vt_ref, gtt_ref):
    x = x_ref[...]
    h = (x * (1.0 + mod_ref[0, 1:2, :]) + mod_ref[0, 0:1, :]).astype(BF16)
    qk_ref[...] = _dot(h, w_ref[:, 0:2 * ML_W])
    v_ref[...] = _dot(h, w_ref[:, 2 * ML_W:3 * ML_W])
    o_ref[...] = _dot(h, w_ref[:, 3 * ML_W:4 * ML_W])
    gt_ref[...] = _dot(h, w_ref[:, 4 * ML_W:ML_IN_PAD]) + gb_ref[...]
    vt_ref[...] = _dot_nt(wt_ref[0:ML_W, :], h)
    gtt_ref[...] = _dot_nt(wt_ref[ML_W:, :], h) + gbt_ref[...]


def _ml_in(x, mod6, w_in, igate_b, fgate_b, seq):
    t = x.shape[0]
    tm = ROW_TILE
    per_b = seq // tm
    n_gate = ML_IN_PAD - 4 * ML_W
    w = jnp.zeros((D_MODEL, ML_IN_PAD), BF16).at[:, :ML_IN].set(w_in.astype(BF16))
    wt = w[:, 2 * ML_W:3 * ML_W].T
    wt = jnp.concatenate([wt, w[:, 4 * ML_W:].T], axis=0)
    gb = jnp.zeros((1, n_gate), F32).at[0, :2 * ML_HEADS].set(jnp.concatenate([igate_b, fgate_b]))
    row = lambda n: pl.BlockSpec((tm, n), lambda i: (i, 0))
    colblk = lambda n: pl.BlockSpec((n, tm), lambda i: (0, i))
    full = lambda a: pl.BlockSpec(a.shape, lambda i: (0,) * a.ndim)
    outs = [2 * ML_W, ML_W, ML_W, n_gate]
    return pl.pallas_call(
        _ml_in_body,
        grid=(t // tm,),
        in_specs=[row(D_MODEL), pl.BlockSpec((1, 6, D_MODEL), lambda i: (i // per_b, 0, 0)), full(w), full(wt),
                  full(gb), pl.BlockSpec((n_gate, 1), lambda i: (0, 0))],
        out_specs=[row(n) for n in outs] + [colblk(ML_W), colblk(n_gate)],
        out_shape=([jax.ShapeDtypeStruct((t, n), F32) for n in outs]
                   + [jax.ShapeDtypeStruct((ML_W, t), F32), jax.ShapeDtypeStruct((n_gate, t), F32)]),
        compiler_params=_cparams("parallel"),
        name="ml_in",
    )(x, mod6, w, wt, gb, gb.reshape(n_gate, 1))


def _mlstm_body(q_ref, k_ref, v_ref, og_ref, gt_ref, vt_ref, gtt_ref, wq_ref, wk_ref, bq_ref, bk_ref, ng_ref,
                y_ref, cq_ref, ck_ref, qs_ref, ks_ref, s_ref, n_ref, m_ref):
    tl = q_ref.shape[0]

    @pl.when(pl.program_id(1) == 0)
    def _():
        for ref in (s_ref, n_ref, m_ref, cq_ref, ck_ref):
            ref[...] = jnp.zeros_like(ref)

    def conv(x_ref, carry_ref, w_ref, b_ref, out_ref, scale):
        for h in range(ML_HEADS):
            cols = slice(ML_DH * h, ML_DH * (h + 1))
            x = x_ref[:, cols]
            xin = jnp.concatenate([carry_ref[:, cols], x], axis=0)
            acc = jnp.zeros((tl, ML_DH), F32) + b_ref[:, cols]
            for i in range(CONV_W):
                sh = CONV_W - 1 - i
                xs = xin if sh == 0 else pltpu.roll(xin, sh, 0)
                acc = acc + w_ref[i:i + 1, cols] * xs[8:8 + tl]
            carry_ref[:, cols] = x[tl - 8:tl]
            out_ref[:, cols] = _silu(acc) * scale

    conv(q_ref, cq_ref, wq_ref, bq_ref, qs_ref, 1.0)
    conv(k_ref, ck_ref, wk_ref, bk_ref, ks_ref, ML_DH ** -0.5)

    cs = ML_CHUNK
    row = lax.broadcasted_iota(jnp.int32, (cs, cs), 0)
    col = lax.broadcasted_iota(jnp.int32, (cs, cs), 1)
    causal = col <= row
    tri = causal.astype(BF16)
    tri_t = (row <= col).astype(BF16)

    for c in range(tl // cs):
        rows = slice(cs * c, cs * (c + 1))
        gt = gt_ref[rows, :]
        gt_t = gtt_ref[:, rows]
        lf_c = _log_sigmoid(gt)
        lf_r = _log_sigmoid(gt_t[ML_HEADS:2 * ML_HEADS])
        b_c_all = sum(_dot(tri, part) for part in _split3_bf16(lf_c))
        b_r_all = sum(_dot(part, tri_t) for part in _split3_bf16(lf_r))
        per_head = lambda f: jnp.stack([f(h) for h in range(ML_HEADS)])
        head_cols = lambda h: slice(ML_DH * h, ML_DH * (h + 1))
        q3 = per_head(lambda h: qs_ref[rows, head_cols(h)])
        k3 = per_head(lambda h: ks_ref[rows, head_cols(h)])
        vb = per_head(lambda h: v_ref[rows, head_cols(h)]).astype(BF16)
        vtb = per_head(lambda h: vt_ref[head_cols(h), rows]).astype(BF16)
        i_c = per_head(lambda h: gt[:, h:h + 1])
        b_c = per_head(lambda h: b_c_all[:, ML_HEADS + h:ML_HEADS + h + 1])
        i_r = per_head(lambda h: gt_t[h:h + 1, :])
        b_r = per_head(lambda h: b_r_all[h:h + 1, :])
        m = m_ref[:, :, 0:1]
        dmat = jnp.where(causal[None], b_c - b_r + i_r, -jnp.inf)
        inter_log = b_c + m
        m_t = jnp.maximum(inter_log, jnp.max(dmat, axis=2, keepdims=True))
        w_intra = jnp.exp(dmat - m_t)
        w_inter = jnp.exp(inter_log - m_t)
        qb = q3.astype(BF16)
        s = jnp.einsum('htd,hsd->hts', qb, k3.astype(BF16), preferred_element_type=F32) * w_intra
        state, nrm = s_ref[...], n_ref[...]
        num = (w_inter * jnp.einsum('htd,hvd->htv', qb, state.astype(BF16), preferred_element_type=F32)
               + jnp.einsum('hts,hsv->htv', s.astype(BF16), vb, preferred_element_type=F32))
        den = w_inter * jnp.sum(q3 * nrm, axis=2, keepdims=True) + jnp.sum(s, axis=2, keepdims=True)
        hc = num / jnp.maximum(jnp.abs(den), jnp.exp(-m_t))
        hc = hc * lax.rsqrt(jnp.mean(hc * hc, axis=2, keepdims=True) + RMS_EPS)
        for h in range(ML_HEADS):
            cols = head_cols(h)
            y_ref[rows, cols] = _sigmoid(og_ref[rows, cols]) * (hc[h] * ng_ref[:, cols])
        b_last = b_c[:, cs - 1:cs, :]
        gs_c = b_last - b_c + i_c
        gs_r = b_last - b_r + i_r
        m_new = jnp.maximum(b_last + m, jnp.max(gs_r, axis=2, keepdims=True))
        dec = jnp.exp(b_last + m - m_new)
        kw = k3 * jnp.exp(gs_c - m_new)
        s_ref[...] = dec * state + jnp.einsum('hvt,htd->hvd', vtb, kw.astype(BF16), preferred_element_type=F32)
        n_ref[...] = dec * nrm + jnp.sum(kw, axis=1, keepdims=True)
        m_ref[...] = jnp.broadcast_to(m_new, m_ref.shape)


def _mlstm(qk, v, og, gates, v_t, gates_t, conv_w, conv_b, norm_g, bsz, seq):
    tl = SEQ_TILE
    nl = seq // tl
    blk = lambda n, off: pl.BlockSpec((tl, n), lambda b, l: (b * nl + l, off))
    blk_t = lambda n: pl.BlockSpec((n, tl), lambda b, l: (0, b * nl + l))
    par = lambda rows, off: pl.BlockSpec((rows, ML_W), lambda b, l: (0, off))
    state = [pltpu.VMEM((ML_HEADS, ML_DH, ML_DH), F32), pltpu.VMEM((ML_HEADS, 1, ML_DH), F32),
             pltpu.VMEM((ML_HEADS, 1, ML_DH), F32)]
    return pl.pallas_call(
        _mlstm_body,
        grid=(bsz, nl),
        in_specs=[blk(ML_W, 0), blk(ML_W, 1), blk(ML_W, 0), blk(ML_W, 0), blk(gates.shape[1], 0),
                  blk_t(ML_W), blk_t(gates_t.shape[0]),
                  par(CONV_W, 0), par(CONV_W, 1), par(1, 0), par(1, 1), par(1, 0)],
        out_specs=blk(ML_W, 0),
        out_shape=jax.ShapeDtypeStruct((bsz * seq, ML_W), F32),
        scratch_shapes=[pltpu.VMEM((8, ML_W), F32), pltpu.VMEM((8, ML_W), F32),
                        pltpu.VMEM((tl, ML_W), F32), pltpu.VMEM((tl, ML_W), F32)] + state,
        compiler_params=_cparams("parallel", "arbitrary"),
        name="mlstm",
    )(qk, qk, v, og, gates, v_t, gates_t, conv_w, conv_w, conv_b.reshape(1, -1), conv_b.reshape(1, -1),
      norm_g.reshape(1, ML_W))


def _router_body(x_ref, mod_ref, rw_ref, rb_ref, info_ref, cnt_ref, carry_ref):
    tr = x_ref.shape[0]

    @pl.when(pl.program_id(0) == 0)
    def _():
        carry_ref[...] = jnp.zeros_like(carry_ref)

    h = x_ref[...] * (1.0 + mod_ref[0, 4:5, :]) + mod_ref[0, 3:4, :]
    logits = lax.dot_general(rw_ref[...], h, (((1,), (1,)), ((), ())), precision=lax.Precision.HIGHEST,
                             preferred_element_type=F32)
    aff = _sigmoid(logits)
    sel = aff + rb_ref[...]
    s = [sel[e:e + 1, :] for e in range(N_EXPERTS)]
    a = [aff[e:e + 1, :] for e in range(N_EXPERTS)]

    def top2_sum(v):
        best = v[0] + v[1]
        for lo_i, hi_i in zip(PAIR_LO[1:], PAIR_HI[1:]):
            best = jnp.maximum(best, v[lo_i] + v[hi_i])
        return best

    score = [top2_sum(s[EPG * g:EPG * (g + 1)]) for g in range(N_GROUPS)]
    g_idx = jnp.zeros((1, tr), jnp.int32)
    best = score[0]
    for g in range(1, N_GROUPS):
        upd = score[g] > best
        g_idx = jnp.where(upd, g, g_idx)
        best = jnp.where(upd, score[g], best)

    def pick_group(rows, j):
        out = rows[j]
        for g in range(1, N_GROUPS):
            out = jnp.where(g_idx == g, rows[EPG * g + j], out)
        return out

    sg = [pick_group(s, j) for j in range(EPG)]
    ag = [pick_group(a, j) for j in range(EPG)]
    first = jnp.zeros((1, tr), jnp.int32)
    best = sg[0]
    for j in range(1, EPG):
        upd = sg[j] > best
        first = jnp.where(upd, j, first)
        best = jnp.where(upd, sg[j], best)
    second = jnp.zeros((1, tr), jnp.int32)
    best2 = jnp.full((1, tr), -jnp.inf, F32)
    for j in range(EPG):
        upd = (first != j) & (sg[j] > best2)
        second = jnp.where(upd, j, second)
        best2 = jnp.where(upd, sg[j], best2)
    lo = jnp.minimum(first, second)
    hi = jnp.maximum(first, second)
    pair = jnp.where(lo == 0, hi - 1, jnp.where(lo == 1, hi + 1, N_PAIRS - 1))
    cls = g_idx * N_PAIRS + pair

    def pick_local(rows, idx):
        out = rows[0]
        for j in range(1, EPG):
            out = jnp.where(idx == j, rows[j], out)
        return out

    a_lo, a_hi = pick_local(ag, lo), pick_local(ag, hi)
    tot = a_lo + a_hi
    onehot = (lax.broadcasted_iota(jnp.int32, (32, tr), 0) == cls).astype(F32)
    before = (lax.broadcasted_iota(jnp.int32, (tr, tr), 0) < lax.broadcasted_iota(jnp.int32, (tr, tr), 1)).astype(BF16)
    cum = _dot(onehot.astype(BF16), before) + carry_ref[:, 0:1]
    rank = jnp.sum(onehot * cum, axis=0, keepdims=True)
    carry_ref[...] = carry_ref[...] + jnp.sum(onehot, axis=1, keepdims=True)
    cnt_ref[0] = carry_ref[...]
    zero = jnp.zeros((1, tr), F32)
    info_ref[...] = jnp.concatenate([cls.astype(F32), rank, a_lo / tot, a_hi / tot, zero, zero, zero, zero], axis=0)


def _router(x, mod6, router_w, router_bias, seq):
    t = x.shape[0]
    tr = ROUTE_TILE
    per_b = seq // tr
    return pl.pallas_call(
        _router_body,
        grid=(t // tr,),
        in_specs=[pl.BlockSpec((tr, D_MODEL), lambda i: (i, 0)),
                  pl.BlockSpec((1, 6, D_MODEL), lambda i: (i // per_b, 0, 0)),
                  pl.BlockSpec((N_EXPERTS, D_MODEL), lambda i: (0, 0)),
                  pl.BlockSpec((N_EXPERTS, 1), lambda i: (0, 0))],
        out_specs=[pl.BlockSpec((8, tr), lambda i: (0, i)), pl.BlockSpec((1, 32, 128), lambda i: (i, 0, 0))],
        out_shape=[jax.ShapeDtypeStruct((8, t), F32), jax.ShapeDtypeStruct((t // tr, 32, 128), F32)],
        scratch_shapes=[pltpu.VMEM((32, 128), F32)],
        compiler_params=_cparams("arbitrary"),
        name="router",
    )(x, mod6, router_w.T.astype(F32), router_bias.reshape(N_EXPERTS, 1).astype(F32))


def _from_row_tiles(ref):
    return jnp.concatenate([ref[:, j, :] for j in range(N_LANE_GROUPS)], axis=1)


def _store_row_tiles(ref, x):
    for j in range(N_LANE_GROUPS):
        ref[:, j, :] = x[:, 128 * j:128 * (j + 1)]


def _run_dmas(make_copy, length):
    off = jnp.int32(0)
    size = PERM_TILE
    while size >= 1:
        has = (length & size) != 0

        @pl.when(has)
        def _(off=off, size=size):
            make_copy(off, size).start()

        off = off + jnp.where(has, size, 0)
        size //= 2


def _dispatch_body(pos_ref, gstart_ref, lstart_ref, len_ref, x_ref, dst_in_ref, dst_ref, stage, ordered, sem):
    del dst_in_ref
    i = pl.program_id(0)
    tm = x_ref.shape[0]
    _store_row_tiles(stage, x_ref[...])

    def place(r, c):
        ordered[pos_ref[i * tm + r]] = stage[r]
        return c

    lax.fori_loop(0, tm, place, 0)
    for c in range(N_CLASSES):
        run = i * N_CLASSES + c
        g0, l0 = gstart_ref[run], lstart_ref[run]
        _run_dmas(lambda off, n: pltpu.make_async_copy(ordered.at[pl.ds(l0 + off, n)], dst_ref.at[pl.ds(g0 + off, n)],
                                                       sem.at[0]), len_ref[run])
    pltpu.make_async_copy(ordered, dst_ref.at[pl.ds(0, tm)], sem.at[0]).wait()


def _dispatch(pos, gstart, lstart, length, x, n_dst):
    t = x.shape[0]
    tm = PERM_TILE
    dst0 = jnp.zeros((n_dst, N_LANE_GROUPS, 128), F32)
    return pl.pallas_call(
        _dispatch_body,
        grid_spec=pltpu.PrefetchScalarGridSpec(
            num_scalar_prefetch=4,
            grid=(t // tm,),
            in_specs=[pl.BlockSpec((tm, D_MODEL), lambda i, *s: (i, 0)), pl.BlockSpec(memory_space=pl.ANY)],
            out_specs=pl.BlockSpec(memory_space=pl.ANY),
            scratch_shapes=[pltpu.VMEM((tm, N_LANE_GROUPS, 128), F32)] * 2 + [pltpu.SemaphoreType.DMA((1,))]),
        out_shape=jax.ShapeDtypeStruct(dst0.shape, F32),
        input_output_aliases={5: 0},
        compiler_params=_cparams("arbitrary"),
        name="moe_dispatch",
    )(pos, gstart, lstart, length, x, dst0)


def _combine_body(pos_ref, gstart_ref, lstart_ref, len_ref, src_ref, o_ref, ordered, stage, sem):
    i = pl.program_id(0)
    tm = o_ref.shape[0]
    for c in range(N_CLASSES):
        run = i * N_CLASSES + c
        g0, l0 = gstart_ref[run], lstart_ref[run]
        _run_dmas(lambda off, n: pltpu.make_async_copy(src_ref.at[pl.ds(g0 + off, n)], ordered.at[pl.ds(l0 + off, n)],
                                                       sem.at[0]), len_ref[run])
    pltpu.make_async_copy(src_ref.at[pl.ds(0, tm)], ordered, sem.at[0]).wait()

    def take(r, c):
        stage[r] = ordered[pos_ref[i * tm + r]]
        return c

    lax.fori_loop(0, tm, take, 0)
    o_ref[...] = _from_row_tiles(stage)


def _combine(pos, gstart, lstart, length, src):
    t = pos.shape[0]
    tm = PERM_TILE
    return pl.pallas_call(
        _combine_body,
        grid_spec=pltpu.PrefetchScalarGridSpec(
            num_scalar_prefetch=4,
            grid=(t // tm,),
            in_specs=[pl.BlockSpec(memory_space=pl.ANY)],
            out_specs=pl.BlockSpec((tm, D_MODEL), lambda i, *s: (i, 0)),
            scratch_shapes=[pltpu.VMEM((tm, N_LANE_GROUPS, 128), F32)] * 2 + [pltpu.SemaphoreType.DMA((1,))]),
        out_shape=jax.ShapeDtypeStruct((t, D_MODEL), F32),
        compiler_params=_cparams("arbitrary"),
        name="moe_combine",
    )(pos, gstart, lstart, length, src)


def _moe_body(e_lo_ref, e_hi_ref, blk_ref, new_ref, used_ref, x_ref, meta_ref, mod_ref,
              wg_lo32, wu_lo32, wd_lo32, wg_hi32, wu_hi32, wd_hi32, lg_ref, lb_ref, o_ref,
              wg_lo, wu_lo, wd_lo, wg_hi, wu_hi, wd_hi):
    del e_lo_ref, e_hi_ref, blk_ref
    i = pl.program_id(0)
    used = used_ref[0]

    @pl.when(new_ref[i] == 1)
    def _():
        for src, dst in ((wg_lo32, wg_lo), (wu_lo32, wu_lo), (wd_lo32, wd_lo),
                         (wg_hi32, wg_hi), (wu_hi32, wu_hi), (wd_hi32, wd_hi)):
            dst[...] = src[...].astype(BF16)

    @pl.when(i >= used)
    def _():
        o_ref[...] = jnp.zeros_like(o_ref)

    @pl.when(i < used)
    def _():
        x = _from_row_tiles(x_ref)
        meta = meta_ref[...]
        bid, g_lo, g_hi = meta[:, 0:1], meta[:, 1:2], meta[:, 2:3]
        nb = mod_ref.shape[0]

        def per_row(r):
            out = mod_ref[0, r:r + 1, :]
            for b in range(1, nb):
                out = jnp.where(bid == float(b), mod_ref[b, r:r + 1, :], out)
            return out

        h = (x * (1.0 + per_row(4)) + per_row(3)).astype(BF16)

        def expert(wg, wu, wd):
            he = _silu(_dot(h, wg[...])) * _dot(h, wu[...])
            return _dot(he.astype(BF16), wd[...])

        y = g_lo * expert(wg_lo, wu_lo, wd_lo) + g_hi * expert(wg_hi, wu_hi, wd_hi)
        _store_row_tiles(o_ref, _layer_norm(ALPHA * x + (1.0 + per_row(5)) * y, lg_ref[...], lb_ref[...]))


def _moe_tiles(e_lo, e_hi, blk, new, used, xs, meta, mod6, layer, w_gate, w_up, w_down, ln_g, ln_b):
    tp = xs.shape[0]
    tm = ROW_TILE
    tiles = lambda index: pl.BlockSpec((tm, N_LANE_GROUPS, 128), index)
    wspec = lambda shape, which: pl.BlockSpec((None, None) + shape, lambda i, *s: (layer, s[which][i], 0, 0))
    full = lambda a: pl.BlockSpec(a.shape, lambda i, *s: (0,) * a.ndim)
    lg, lb = ln_g.reshape(1, D_MODEL), ln_b.reshape(1, D_MODEL)
    gu, dn = (D_MODEL, D_EXPERT), (D_EXPERT, D_MODEL)
    return pl.pallas_call(
        _moe_body,
        grid_spec=pltpu.PrefetchScalarGridSpec(
            num_scalar_prefetch=5,
            grid=(tp // tm,),
            in_specs=[tiles(lambda i, *s: (s[2][i], 0, 0)), pl.BlockSpec((tm, 8), lambda i, *s: (s[2][i], 0)), full(mod6),
                      wspec(gu, 0), wspec(gu, 0), wspec(dn, 0), wspec(gu, 1), wspec(gu, 1), wspec(dn, 1),
                      full(lg), full(lb)],
            out_specs=tiles(lambda i, *s: (i, 0, 0)),
            scratch_shapes=[pltpu.VMEM(gu, BF16), pltpu.VMEM(gu, BF16), pltpu.VMEM(dn, BF16)] * 2),
        out_shape=jax.ShapeDtypeStruct((tp, N_LANE_GROUPS, 128), F32),
        compiler_params=_cparams("arbitrary"),
        name="moe_tiles",
    )(e_lo, e_hi, blk, new, used, xs, meta, mod6, w_gate, w_up, w_down, w_gate, w_up, w_down, lg, lb)


def _moe_layer(x, mod6, router_w, router_bias, layer, w_gate, w_up, w_down, ln_g, ln_b, bsz, seq):
    t = x.shape[0]
    tm = ROW_TILE
    n_tiles = t // tm + N_CLASSES
    tp = n_tiles * tm
    info, seen = _router(x, mod6, router_w, router_bias, seq)
    cls = info[0].astype(jnp.int32)
    rank = info[1].astype(jnp.int32)
    seen = seen[:, :N_CLASSES, 0].astype(jnp.int32)
    cnt = seen[-1]
    tiles_c = (cnt + tm - 1) // tm
    tile_end = jnp.cumsum(tiles_c)
    offs = (tile_end - tiles_c) * tm
    onehot = cls[:, None] == jnp.arange(N_CLASSES)[None, :]
    dest = rank + jnp.sum(jnp.where(onehot, offs[None, :], 0), axis=1)
    used = tile_end[-1]
    tile_id = jnp.arange(n_tiles)
    blk = jnp.maximum(jnp.minimum(tile_id, used - 1), 0)
    tile_cls = jnp.minimum(jnp.sum(blk[:, None] >= tile_end[None, :], axis=1), N_CLASSES - 1)
    grp, pair = tile_cls // N_PAIRS, tile_cls % N_PAIRS
    e_lo = (grp * EPG + jnp.asarray(PAIR_LO, jnp.int32)[pair]).astype(jnp.int32)
    e_hi = (grp * EPG + jnp.asarray(PAIR_HI, jnp.int32)[pair]).astype(jnp.int32)
    new = jnp.concatenate([jnp.ones((1,), jnp.int32), (tile_cls[1:] != tile_cls[:-1]).astype(jnp.int32)])
    bid = (jnp.arange(t) // seq).astype(F32)
    meta_nat = jnp.stack([bid, info[2], info[3]] + [jnp.zeros((t,), F32)] * 5, axis=1)
    meta = jnp.zeros((tp, 8), F32).at[dest].set(meta_nat)
    pt = PERM_TILE
    before = jnp.concatenate([jnp.zeros((1, N_CLASSES), jnp.int32), seen[:-1]], axis=0)
    length = seen - before
    lstart = jnp.cumsum(length, axis=1) - length
    gstart = offs[None, :] + before
    block = jnp.arange(t) // pt
    pos = rank + jnp.sum(jnp.where(onehot, (lstart - before)[block], 0), axis=1)
    runs = (pos.astype(jnp.int32), gstart.reshape(-1).astype(jnp.int32), lstart.reshape(-1).astype(jnp.int32),
            length.reshape(-1).astype(jnp.int32))
    xs = _dispatch(*runs, x, tp)
    ys = _moe_tiles(e_lo, e_hi, blk.astype(jnp.int32), new, used.reshape(1).astype(jnp.int32), xs, meta,
                    mod6, layer, w_gate, w_up, w_down, ln_g, ln_b)
    return _combine(*runs, ys)


def kernel(x, c, router_w, router_bias, ada_w, ada_b, ln1_g, ln1_b, ln2_g, ln2_b, moe_w_gate, moe_w_up, moe_w_down,
           ab_w_in, s5_lam_re, s5_lam_im, s5_log_dt, s5_b_re, s5_b_im, s5_c_re, s5_c_im, s5_d, s5_glu_w, s5_glu_b,
           gla_gate_w, gla_gate_b, gla_norm_g, ab_w_out, ml_w_in, ml_conv_w, ml_conv_b, ml_igate_b, ml_fgate_b,
           ml_norm_g, ml_w_out):
    bsz, seq, d = x.shape
    t = bsz * seq
    mod = _ada_mod(c, ada_w, ada_b)
    xt = x.reshape(t, d)
    for layer in range(DEPTH):
        j = layer // 2
        mod6 = mod[layer]
        if layer % 2 == 0:
            u, q, k, v, r, la = _ab_in(xt, mod6, ab_w_in[j], gla_gate_w[j], gla_gate_b[j], seq)
            tables = _s5_prep(s5_lam_re[j], s5_lam_im[j], s5_log_dt[j], s5_b_re[j], s5_b_im[j], s5_c_re[j], s5_c_im[j])
            ys = _s5_conv(u, tables, bsz, seq)
            yb = _gla(q, k, la, v, r, gla_norm_g[j], bsz, seq)
            w_out = ab_w_out[j].astype(BF16)
            consts = [s5_d[j].reshape(1, S5_W), s5_glu_w[j].astype(BF16), s5_glu_b[j].reshape(1, S5_W),
                      w_out[:S5_W], w_out[S5_W:]]
            xt = _out_call(_ab_out_body, "ab_out", [ys, u, yb], xt, mod6, consts, ln1_g[layer], ln1_b[layer], seq)
        else:
            qk, v, og, gates, v_t, gates_t = _ml_in(xt, mod6, ml_w_in[j], ml_igate_b[j], ml_fgate_b[j], seq)
            y = _mlstm(qk, v, og, gates, v_t, gates_t, ml_conv_w[j], ml_conv_b[j], ml_norm_g[j], bsz, seq)
            xt = _out_call(_ml_out_body, "ml_out", [y], xt, mod6, [ml_w_out[j].astype(BF16)],
                           ln1_g[layer], ln1_b[layer], seq)
        xt = _moe_layer(xt, mod6, router_w, router_bias, layer, moe_w_gate, moe_w_up, moe_w_down,
                        ln2_g[layer], ln2_b[layer], bsz, seq)
    return xt.reshape(bsz, seq, d)
```

```python
import functools
import math

import jax
import jax.numpy as jnp
from jax import lax
from jax.experimental import pallas as pl
from jax.experimental.pallas import tpu as pltpu

F32 = jnp.float32
BF16 = jnp.bfloat16

D_MODEL = 1024
DEPTH = 4
S5_W = 512
S5_GROUP = 16
S5_GROUPS = 32
S5_STATE = 64
S5_CHUNK = 16
S5_SCAN_ROWS = 8
GLA_HEADS = 4
GLA_DV = 128
GLA_DK = 64
GLA_QK = 256
GLA_V = 512
GLA_RANK = 16
GLA_GATE_NORM = 16.0
GLA_SUB = 16
AB_IN = S5_W + 2 * GLA_QK + 2 * GLA_V + GLA_RANK
AB_IN_PAD = 2176
ML_HEADS = 8
ML_W = 1024
ML_DH = 128
ML_IN = 4 * ML_W + 2 * ML_HEADS
ML_IN_PAD = 4224
CONV_W = 4
CHUNK = 64
ML_CHUNK = 128
N_EXPERTS = 16
N_GROUPS = 4
EPG = 4
D_EXPERT = 256
N_PAIRS = 6
N_CLASSES = N_GROUPS * N_PAIRS
PAIR_LO = (0, 0, 0, 1, 1, 2)
PAIR_HI = (1, 2, 3, 2, 3, 3)
ALPHA = (2.0 * DEPTH) ** 0.25
LN_EPS = 1e-5
RMS_EPS = 1e-6

V7X_VMEM_BYTES = 64 * 1024 * 1024
VMEM_LIMIT = (V7X_VMEM_BYTES * 3) // 4
ROW_TILE = 256
SEQ_TILE = 256
ROUTE_TILE = 512
PERM_TILE = 512
assert ROUTE_TILE == PERM_TILE
N_LANE_GROUPS = D_MODEL // 128


def _cparams(*sem):
    return pltpu.CompilerParams(dimension_semantics=sem, vmem_limit_bytes=VMEM_LIMIT)


def _sigmoid(x):
    return 1.0 / (1.0 + jnp.exp(-x))


def _silu(x):
    return x * _sigmoid(x)


def _log_sigmoid(z):
    return jnp.minimum(z, 0.0) - jnp.log1p(jnp.exp(-jnp.abs(z)))


def _gelu_tanh(x):
    return 0.5 * x * (1.0 + jnp.tanh(math.sqrt(2.0 / math.pi) * (x + 0.044715 * (x * x * x))))


def _layer_norm(x, g, b):
    mu = jnp.mean(x, axis=-1, keepdims=True)
    xc = x - mu
    var = jnp.mean(xc * xc, axis=-1, keepdims=True)
    return xc * lax.rsqrt(var + LN_EPS) * g + b


def _split3_bf16(x):
    hi = x.astype(BF16)
    r = x - hi.astype(F32)
    mid = r.astype(BF16)
    lo = (r - mid.astype(F32)).astype(BF16)
    return hi, mid, lo


def _dot(a, b):
    return jnp.dot(a, b, preferred_element_type=F32)


def _dot_nt(a, b):
    return lax.dot_general(a, b, (((1,), (1,)), ((), ())), preferred_element_type=F32)


def _dot_tn(a, b):
    return lax.dot_general(a, b, (((0,), (0,)), ((), ())), preferred_element_type=F32)


def _ada_body(c_ref, w_ref, b_ref, o_ref):
    c = c_ref[...]
    o_ref[0] = _dot(_silu(c), w_ref[0]) + b_ref[0]


def _ada_mod(c, ada_w, ada_b):
    bsz = c.shape[0]
    tn = 1536
    cp = jnp.zeros((8, D_MODEL), F32).at[:bsz].set(c)
    out = pl.pallas_call(
        _ada_body,
        grid=(DEPTH, 6 * D_MODEL // tn),
        in_specs=[pl.BlockSpec((8, D_MODEL), lambda l, j: (0, 0)),
                  pl.BlockSpec((1, D_MODEL, tn), lambda l, j: (l, 0, j)),
                  pl.BlockSpec((1, 1, tn), lambda l, j: (l, 0, j))],
        out_specs=pl.BlockSpec((1, 8, tn), lambda l, j: (l, 0, j)),
        out_shape=jax.ShapeDtypeStruct((DEPTH, 8, 6 * D_MODEL), F32),
        compiler_params=_cparams("parallel", "parallel"),
        name="ada_mod",
    )(cp, ada_w, ada_b.reshape(DEPTH, 1, 6 * D_MODEL))
    return out[:, :bsz].reshape(DEPTH, bsz, 6, D_MODEL)


def _ab_in_body(x_ref, mod_ref, w_ref, gw_ref, gb_ref, u_ref, q_ref, k_ref, v_ref, r_ref, la_ref):
    x = x_ref[...]
    h = (x * (1.0 + mod_ref[0, 1:2, :]) + mod_ref[0, 0:1, :]).astype(BF16)

    def seg(a, b):
        return _dot(h, w_ref[:, a:b])

    u_ref[...] = seg(0, 512)
    q_ref[...] = seg(512, 768)
    k_ref[...] = seg(768, 1024)
    v_ref[...] = seg(1024, 1536)
    r_ref[...] = seg(1536, 2048)
    a_lr = seg(2048, AB_IN_PAD)
    z = _dot(a_lr.astype(BF16), gw_ref[...]) + gb_ref[...]
    la_ref[...] = _log_sigmoid(z) * (1.0 / GLA_GATE_NORM)


def _ab_in(x, mod6, w_in, gate_w, gate_b, seq):
    t = x.shape[0]
    tm = ROW_TILE
    per_b = seq // tm
    w = jnp.zeros((D_MODEL, AB_IN_PAD), BF16).at[:, :AB_IN].set(w_in.astype(BF16))
    gw = jnp.zeros((AB_IN_PAD - 2048, GLA_QK), BF16).at[:GLA_RANK].set(gate_w.astype(BF16))
    row = lambda n: pl.BlockSpec((tm, n), lambda i: (i, 0))
    full = lambda a: pl.BlockSpec(a.shape, lambda i: (0,) * a.ndim)
    outs = [S5_W, GLA_QK, GLA_QK, GLA_V, GLA_V, GLA_QK]
    return pl.pallas_call(
        _ab_in_body,
        grid=(t // tm,),
        in_specs=[row(D_MODEL),
                  pl.BlockSpec((1, 6, D_MODEL), lambda i: (i // per_b, 0, 0)),
                  full(w), full(gw), pl.BlockSpec((1, GLA_QK), lambda i: (0, 0))],
        out_specs=[row(n) for n in outs],
        out_shape=[jax.ShapeDtypeStruct((t, n), F32) for n in outs],
        compiler_params=_cparams("parallel"),
        name="ab_in",
    )(x, mod6, w, gw, gate_b.reshape(1, GLA_QK))


def _s5_prep(lam_re, lam_im, log_dt, b_re, b_im, c_re, c_im):
    hp = lax.Precision.HIGHEST
    n = S5_CHUNK
    lre, lim = lam_re.astype(F32), lam_im.astype(F32)
    dt = jnp.exp(log_dt.astype(F32))[:, None]
    d = jnp.arange(n + 1, dtype=F32)[:, None, None]
    mag = jnp.exp(lre * dt * d)
    ang = lim * dt * d
    pw_re, pw_im = mag * jnp.cos(ang), mag * jnp.sin(ang)
    den = lre * lre + lim * lim
    nre, nim = pw_re[1] - 1.0, pw_im[1]
    coef_re = (nre * lre + nim * lim) / den
    coef_im = (nim * lre - nre * lim) / den
    bre, bim = b_re.astype(F32), b_im.astype(F32)
    bb_re = coef_re[..., None] * bre - coef_im[..., None] * bim
    bb_im = coef_re[..., None] * bim + coef_im[..., None] * bre
    cre, cim = c_re.astype(F32), c_im.astype(F32)
    cp_re = cre[None] * pw_re[:, :, None, :] - cim[None] * pw_im[:, :, None, :]
    cp_im = cre[None] * pw_im[:, :, None, :] + cim[None] * pw_re[:, :, None, :]
    kern = (jnp.einsum('dgcp,gpe->dgce', cp_re, bb_re, precision=hp)
            - jnp.einsum('dgcp,gpe->dgce', cp_im, bb_im, precision=hp))
    jj = jnp.arange(n)[:, None]
    tt = jnp.arange(n)[None, :]
    lag = tt - jj
    kg = jnp.where((lag >= 0)[:, :, None, None, None], kern[jnp.clip(lag, 0, n)], 0.0)
    m_intra = jnp.transpose(kg, (2, 0, 4, 1, 3)).reshape(S5_GROUPS, n * S5_GROUP, n * S5_GROUP)
    pws_re, pws_im = pw_re[n - 1 - jnp.arange(n)], pw_im[n - 1 - jnp.arange(n)]
    ws_re = pws_re[..., None] * bb_re[None] - pws_im[..., None] * bb_im[None]
    ws_im = pws_re[..., None] * bb_im[None] + pws_im[..., None] * bb_re[None]
    w_state = jnp.transpose(jnp.concatenate([ws_re, ws_im], axis=2), (1, 0, 3, 2))
    w_state = w_state.reshape(S5_GROUPS, n * S5_GROUP, 2 * S5_STATE)
    wi_re = jnp.transpose(cp_re[1:], (1, 3, 0, 2))
    wi_im = -jnp.transpose(cp_im[1:], (1, 3, 0, 2))
    w_inter = jnp.concatenate([wi_re, wi_im], axis=1).reshape(S5_GROUPS, 2 * S5_STATE, n * S5_GROUP)
    r = (n * jnp.arange(S5_SCAN_ROWS + 1, dtype=F32))[None, :, None]
    cmag, cang = jnp.exp(lre[:, None, :] * dt[:, None, :] * r), lim[:, None, :] * dt[:, None, :] * r
    a_pows = jnp.concatenate([cmag * jnp.cos(cang), cmag * jnp.sin(cang)], axis=1)
    return m_intra.astype(BF16), w_state.astype(BF16), w_inter.astype(BF16), a_pows


def _s5_body(u_ref, m_ref, ws_ref, wi_ref, a_ref, y_ref, ere, eim, sre, sim, *, bsz):
    u = u_ref[0]
    e = _dot(u, ws_ref[0])
    ere[...] = e[:, :S5_STATE]
    eim[...] = e[:, S5_STATE:]
    n_pow = S5_SCAN_ROWS + 1
    pr, pi = a_ref[0, 0:n_pow, :], a_ref[0, n_pow:2 * n_pow, :]
    tab_r, tab_i = pr[0:S5_SCAN_ROWS], pi[0:S5_SCAN_ROWS]
    rowi = lax.broadcasted_iota(jnp.int32, (S5_SCAN_ROWS, S5_STATE), 0)
    nk = ere.shape[0] // bsz

    def cmul(a_r, a_i, x_r, x_i):
        return a_r * x_r - a_i * x_i, a_r * x_i + a_i * x_r

    def shifted(x, s):
        return jnp.where(rowi >= s, pltpu.roll(x, s, 0), 0.0)

    def tile(m, carry):
        out = []
        for b in range(bsz):
            c_r, c_i = carry[2 * b], carry[2 * b + 1]
            r0 = pl.multiple_of(b * nk + m * S5_SCAN_ROWS, S5_SCAN_ROWS)
            x_r, x_i = ere[pl.ds(r0, S5_SCAN_ROWS), :], eim[pl.ds(r0, S5_SCAN_ROWS), :]
            for s in (1, 2, 4):
                d_r, d_i = cmul(pr[s:s + 1], pi[s:s + 1], shifted(x_r, s), shifted(x_i, s))
                x_r, x_i = x_r + d_r, x_i + d_i
            t_r, t_i = cmul(tab_r, tab_i, c_r, c_i)
            sre[pl.ds(r0, S5_SCAN_ROWS), :] = shifted(x_r, 1) + t_r
            sim[pl.ds(r0, S5_SCAN_ROWS), :] = shifted(x_i, 1) + t_i
            n_r, n_i = cmul(pr[S5_SCAN_ROWS:n_pow], pi[S5_SCAN_ROWS:n_pow], c_r, c_i)
            last = slice(S5_SCAN_ROWS - 1, S5_SCAN_ROWS)
            out += [n_r + jnp.broadcast_to(x_r[last], c_r.shape), n_i + jnp.broadcast_to(x_i[last], c_i.shape)]
        return tuple(out)

    z = jnp.zeros((S5_SCAN_ROWS, S5_STATE), F32)
    lax.fori_loop(0, nk // S5_SCAN_ROWS, tile, (z,) * (2 * bsz))
    y = _dot(u, m_ref[0])
    y = y + _dot(sre[...].astype(BF16), wi_ref[0, :S5_STATE, :])
    y = y + _dot(sim[...].astype(BF16), wi_ref[0, S5_STATE:, :])
    y_ref[0] = y


def _s5_pack_body(u_ref, o_ref):
    nk = u_ref.shape[0] // S5_CHUNK
    by_step = [u_ref[pl.ds(t, nk, stride=S5_CHUNK), :] for t in range(S5_CHUNK)]
    for g in range(128 // S5_GROUP):
        cols = slice(S5_GROUP * g, S5_GROUP * (g + 1))
        o_ref[g] = jnp.concatenate([x[:, cols] for x in by_step], axis=1).astype(o_ref.dtype)


def _s5_unpack_body(y_ref, o_ref):
    nk = y_ref.shape[1]
    groups = [y_ref[g] for g in range(128 // S5_GROUP)]
    for t in range(S5_CHUNK):
        cols = slice(S5_GROUP * t, S5_GROUP * (t + 1))
        o_ref[pl.ds(t, nk, stride=S5_CHUNK), :] = jnp.concatenate([y[:, cols] for y in groups], axis=1)


def _s5_conv(u, tables, bsz, seq):
    m_intra, w_state, w_inter, a_pows = tables
    t = bsz * seq
    rows = t // S5_CHUNK
    lanes = S5_CHUNK * S5_GROUP
    tm = PERM_TILE
    gpb = 128 // S5_GROUP
    tok = pl.BlockSpec((tm, 128), lambda i, j: (i, j))
    chunked = pl.BlockSpec((gpb, tm // S5_CHUNK, lanes), lambda i, j: (j, i, 0))
    uf = pl.pallas_call(
        _s5_pack_body,
        grid=(t // tm, S5_W // 128),
        in_specs=[tok],
        out_specs=chunked,
        out_shape=jax.ShapeDtypeStruct((S5_GROUPS, rows, lanes), BF16),
        compiler_params=_cparams("parallel", "parallel"),
        name="s5_pack",
    )(u)
    grp = lambda shape: pl.BlockSpec((1,) + shape, lambda g: (g, 0, 0))
    yf = pl.pallas_call(
        functools.partial(_s5_body, bsz=bsz),
        grid=(S5_GROUPS,),
        in_specs=[grp((rows, lanes)), grp((lanes, lanes)), grp((lanes, 2 * S5_STATE)),
                  grp((2 * S5_STATE, lanes)), grp(a_pows.shape[1:])],
        out_specs=grp((rows, lanes)),
        out_shape=jax.ShapeDtypeStruct((S5_GROUPS, rows, lanes), F32),
        scratch_shapes=[pltpu.VMEM((rows, S5_STATE), F32)] * 4,
        compiler_params=_cparams("parallel"),
        name="s5_conv",
    )(uf, m_intra, w_state, w_inter, a_pows)
    return pl.pallas_call(
        _s5_unpack_body,
        grid=(t // tm, S5_W // 128),
        in_specs=[chunked],
        out_specs=tok,
        out_shape=jax.ShapeDtypeStruct((t, S5_W), F32),
        compiler_params=_cparams("parallel", "parallel"),
        name="s5_unpack",
    )(yf)


def _gla_body(q_ref, k_ref, g_ref, v_ref, r_ref, ng_ref, y_ref, s_ref, bc_ref, o_ref):
    tl = q_ref.shape[0]
    scale = GLA_DK ** -0.5

    @pl.when(pl.program_id(1) == 0)
    def _():
        s_ref[...] = jnp.zeros_like(s_ref)

    row = lax.broadcasted_iota(jnp.int32, (CHUNK, CHUNK), 0)
    col = lax.broadcasted_iota(jnp.int32, (CHUNK, CHUNK), 1)
    tri = col <= row
    tref = col < (row // GLA_SUB) * GLA_SUB
    cum_mat = jnp.concatenate([tri, tref], axis=0).astype(BF16)
    lane_head = lax.broadcasted_iota(jnp.int32, (GLA_SUB, GLA_QK), 1) // GLA_DK
    bd_mask = (lax.broadcasted_iota(jnp.int32, (GLA_QK, GLA_V), 0) // GLA_DK
               == lax.broadcasted_iota(jnp.int32, (GLA_QK, GLA_V), 1) // GLA_DV)
    ones_bd = bd_mask.astype(BF16)
    ones_v = jnp.ones((CHUNK, GLA_V), BF16)

    for c in range(tl // CHUNK):
        r0 = c * CHUNK
        qc = q_ref[pl.ds(r0, CHUNK), :] * scale
        kc = k_ref[pl.ds(r0, CHUNK), :]
        g_parts = _split3_bf16(g_ref[pl.ds(r0, CHUNK), :])
        vb = v_ref[pl.ds(r0, CHUNK), :].astype(BF16)
        br = sum(_dot(cum_mat, part) for part in g_parts)
        bc, ref = br[:CHUNK], br[CHUNK:]
        bc_ref[pl.ds(r0, CHUNK), :] = bc
        qt = qc * jnp.exp(bc - ref)
        state = s_ref[...]
        o = _dot((qt * jnp.exp(ref)).astype(BF16), state.astype(BF16))
        blocks = [o[:GLA_SUB]]
        for i in range(1, CHUNK // GLA_SUB):
            ri = bc[GLA_SUB * i - 1:GLA_SUB * i, :]
            kt = (kc * jnp.exp(jnp.minimum(ri - bc, 0.0))).astype(BF16)
            qi = qt[GLA_SUB * i:GLA_SUB * (i + 1)]
            lhs = jnp.concatenate([jnp.where(lane_head == h, qi, 0.0) for h in range(GLA_HEADS)], axis=0)
            att = _dot_nt(lhs.astype(BF16), kt)
            att = jnp.where(col < GLA_SUB * i, att, 0.0)
            ov = _dot(att.astype(BF16), vb)
            oi = jnp.concatenate([ov[GLA_SUB * h:GLA_SUB * (h + 1), GLA_DV * h:GLA_DV * (h + 1)]
                                  for h in range(GLA_HEADS)], axis=1)
            blocks.append(o[GLA_SUB * i:GLA_SUB * (i + 1)] + oi)
        o_ref[pl.ds(r0, CHUNK), :] = jnp.concatenate(blocks, axis=0)
        bl = bc[CHUNK - 1:CHUNK, :]
        kh = (kc * jnp.exp(bl - bc)).astype(BF16)
        upd = _dot_tn(kh, vb)
        dcol = sum(_dot_tn(part, ones_v) for part in g_parts)
        s_ref[...] = jnp.exp(dcol) * state + jnp.where(bd_mask, upd, 0.0)

    q = q_ref[...] * scale
    k = k_ref[...]
    v = v_ref[...]
    bc = bc_ref[...]
    rmod = lax.broadcasted_iota(jnp.int32, (tl, 1), 0) % GLA_SUB
    od = jnp.zeros((tl, GLA_V), F32)
    for d in range(GLA_SUB):
        kd = k if d == 0 else pltpu.roll(k, d, 0)
        bd = bc if d == 0 else pltpu.roll(bc, d, 0)
        vd = v if d == 0 else pltpu.roll(v, d, 0)
        valid = rmod >= d
        e = jnp.exp(jnp.where(valid, bc - bd, 0.0))
        p = jnp.where(valid, q * kd * e, 0.0).astype(BF16)
        od = od + _dot(p, ones_bd) * vd
    o = o_ref[...] + od
    ng = ng_ref[...]
    outs = []
    for h in range(GLA_HEADS):
        oh = o[:, GLA_DV * h:GLA_DV * (h + 1)]
        outs.append(oh * lax.rsqrt(jnp.mean(oh * oh, axis=-1, keepdims=True) + RMS_EPS))
    y_ref[...] = jnp.concatenate(outs, axis=1) * ng * _silu(r_ref[...])


def _gla(q, k, la, v, r, norm_g, bsz, seq):
    tl = SEQ_TILE
    nl = seq // tl
    blk = lambda n: pl.BlockSpec((tl, n), lambda b, l: (b * nl + l, 0))
    return pl.pallas_call(
        _gla_body,
        grid=(bsz, nl),
        in_specs=[blk(GLA_QK), blk(GLA_QK), blk(GLA_QK), blk(GLA_V), blk(GLA_V),
                  pl.BlockSpec((1, GLA_V), lambda b, l: (0, 0))],
        out_specs=blk(GLA_V),
        out_shape=jax.ShapeDtypeStruct((bsz * seq, GLA_V), F32),
        scratch_shapes=[pltpu.VMEM((GLA_QK, GLA_V), F32), pltpu.VMEM((tl, GLA_QK), F32),
                        pltpu.VMEM((tl, GLA_V), F32)],
        compiler_params=_cparams("parallel", "arbitrary"),
        name="gla",
    )(q, k, la, v, r, norm_g.reshape(1, GLA_V))


def _residual_ln(x, y, mod_ref, gate_row, lg_ref, lb_ref):
    return _layer_norm(ALPHA * x + (1.0 + mod_ref[0, gate_row:gate_row + 1, :]) * y, lg_ref[...], lb_ref[...])


def _ab_out_body(ys_ref, u_ref, yb_ref, x_ref, mod_ref, d_ref, gw_ref, gb_ref, wa_ref, wb_ref, lg_ref, lb_ref, o_ref):
    z = _gelu_tanh(ys_ref[...] + d_ref[...] * u_ref[...])
    ya = z * _sigmoid(_dot(z.astype(BF16), gw_ref[...]) + gb_ref[...])
    y = _dot(ya.astype(BF16), wa_ref[...]) + _dot(yb_ref[...].astype(BF16), wb_ref[...])
    o_ref[...] = _residual_ln(x_ref[...], y, mod_ref, 2, lg_ref, lb_ref)


def _ml_out_body(y_ref, x_ref, mod_ref, w_ref, lg_ref, lb_ref, o_ref):
    y = _dot(y_ref[...].astype(BF16), w_ref[...])
    o_ref[...] = _residual_ln(x_ref[...], y, mod_ref, 2, lg_ref, lb_ref)


def _out_call(body, name, row_inputs, x, mod6, consts, ln_g, ln_b, seq):
    t = x.shape[0]
    tm = ROW_TILE
    per_b = seq // tm
    row = lambda a: pl.BlockSpec((tm, a.shape[1]), lambda i: (i, 0))
    full = lambda a: pl.BlockSpec(a.shape, lambda i: (0,) * a.ndim)
    lg, lb = ln_g.reshape(1, D_MODEL), ln_b.reshape(1, D_MODEL)
    return pl.pallas_call(
        body,
        grid=(t // tm,),
        in_specs=([row(a) for a in row_inputs] + [row(x), pl.BlockSpec((1, 6, D_MODEL), lambda i: (i // per_b, 0, 0))]
                  + [full(a) for a in consts] + [full(lg), full(lb)]),
        out_specs=row(x),
        out_shape=jax.ShapeDtypeStruct((t, D_MODEL), F32),
        compiler_params=_cparams("parallel"),
        name=name,
    )(*row_inputs, x, mod6, *consts, lg, lb)


def _ml_in_body(x_ref, mod_ref, w_ref, wt_ref, gb_ref, gbt_ref, qk_ref, v_ref, o_ref, gt_ref, vt_ref, gtt_ref):
    x = x_ref[...]
    h = (x * (1.0 + mod_ref[0, 1:2, :]) + mod_ref[0, 0:1, :]).astype(BF16)
    qk_ref[...] = _dot(h, w_ref[:, 0:2 * ML_W])
    v_ref[...] = _dot(h, w_ref[:, 2 * ML_W:3 * ML_W])
    o_ref[...] = _dot(h, w_ref[:, 3 * ML_W:4 * ML_W])
    gt_ref[...] = _dot(h, w_ref[:, 4 * ML_W:ML_IN_PAD]) + gb_ref[...]
    vt_ref[...] = _dot_nt(wt_ref[0:ML_W, :], h)
    gtt_ref[...] = _dot_nt(wt_ref[ML_W:, :], h) + gbt_ref[...]


def _ml_in(x, mod6, w_in, igate_b, fgate_b, seq):
    t = x.shape[0]
    tm = ROW_TILE
    per_b = seq // tm
    n_gate = ML_IN_PAD - 4 * ML_W
    w = jnp.zeros((D_MODEL, ML_IN_PAD), BF16).at[:, :ML_IN].set(w_in.astype(BF16))
    wt = w[:, 2 * ML_W:3 * ML_W].T
    wt = jnp.concatenate([wt, w[:, 4 * ML_W:].T], axis=0)
    gb = jnp.zeros((1, n_gate), F32).at[0, :2 * ML_HEADS].set(jnp.concatenate([igate_b, fgate_b]))
    row = lambda n: pl.BlockSpec((tm, n), lambda i: (i, 0))
    colblk = lambda n: pl.BlockSpec((n, tm), lambda i: (0, i))
    full = lambda a: pl.BlockSpec(a.shape, lambda i: (0,) * a.ndim)
    outs = [2 * ML_W, ML_W, ML_W, n_gate]
    return pl.pallas_call(
        _ml_in_body,
        grid=(t // tm,),
        in_specs=[row(D_MODEL), pl.BlockSpec((1, 6, D_MODEL), lambda i: (i // per_b, 0, 0)), full(w), full(wt),
                  full(gb), pl.BlockSpec((n_gate, 1), lambda i: (0, 0))],
        out_specs=[row(n) for n in outs] + [colblk(ML_W), colblk(n_gate)],
        out_shape=([jax.ShapeDtypeStruct((t, n), F32) for n in outs]
                   + [jax.ShapeDtypeStruct((ML_W, t), F32), jax.ShapeDtypeStruct((n_gate, t), F32)]),
        compiler_params=_cparams("parallel"),
        name="ml_in",
    )(x, mod6, w, wt, gb, gb.reshape(n_gate, 1))


def _mlstm_body(q_ref, k_ref, v_ref, og_ref, gt_ref, vt_ref, gtt_ref, wq_ref, wk_ref, bq_ref, bk_ref, ng_ref,
                y_ref, cq_ref, ck_ref, qs_ref, ks_ref, s_ref, n_ref, m_ref):
    tl = q_ref.shape[0]

    @pl.when(pl.program_id(1) == 0)
    def _():
        for ref in (s_ref, n_ref, m_ref, cq_ref, ck_ref):
            ref[...] = jnp.zeros_like(ref)

    def conv(x_ref, carry_ref, w_ref, b_ref, out_ref, scale):
        for h in range(ML_HEADS):
            cols = slice(ML_DH * h, ML_DH * (h + 1))
            x = x_ref[:, cols]
            xin = jnp.concatenate([carry_ref[:, cols], x], axis=0)
            acc = jnp.zeros((tl, ML_DH), F32) + b_ref[:, cols]
            for i in range(CONV_W):
                sh = CONV_W - 1 - i
                xs = xin if sh == 0 else pltpu.roll(xin, sh, 0)
                acc = acc + w_ref[i:i + 1, cols] * xs[8:8 + tl]
            carry_ref[:, cols] = x[tl - 8:tl]
            out_ref[:, cols] = _silu(acc) * scale

    conv(q_ref, cq_ref, wq_ref, bq_ref, qs_ref, 1.0)
    conv(k_ref, ck_ref, wk_ref, bk_ref, ks_ref, ML_DH ** -0.5)

    cs = ML_CHUNK
    row = lax.broadcasted_iota(jnp.int32, (cs, cs), 0)
    col = lax.broadcasted_iota(jnp.int32, (cs, cs), 1)
    causal = col <= row
    tri = causal.astype(BF16)
    tri_t = (row <= col).astype(BF16)

    for c in range(tl // cs):
        rows = slice(cs * c, cs * (c + 1))
        gt = gt_ref[rows, :]
        gt_t = gtt_ref[:, rows]
        lf_c = _log_sigmoid(gt)
        lf_r = _log_sigmoid(gt_t[ML_HEADS:2 * ML_HEADS])
        b_c_all = sum(_dot(tri, part) for part in _split3_bf16(lf_c))
        b_r_all = sum(_dot(part, tri_t) for part in _split3_bf16(lf_r))
        per_head = lambda f: jnp.stack([f(h) for h in range(ML_HEADS)])
        head_cols = lambda h: slice(ML_DH * h, ML_DH * (h + 1))
        q3 = per_head(lambda h: qs_ref[rows, head_cols(h)])
        k3 = per_head(lambda h: ks_ref[rows, head_cols(h)])
        vb = per_head(lambda h: v_ref[rows, head_cols(h)]).astype(BF16)
        vtb = per_head(lambda h: vt_ref[head_cols(h), rows]).astype(BF16)
        i_c = per_head(lambda h: gt[:, h:h + 1])
        b_c = per_head(lambda h: b_c_all[:, ML_HEADS + h:ML_HEADS + h + 1])
        i_r = per_head(lambda h: gt_t[h:h + 1, :])
        b_r = per_head(lambda h: b_r_all[h:h + 1, :])
        m = m_ref[:, :, 0:1]
        dmat = jnp.where(causal[None], b_c - b_r + i_r, -jnp.inf)
        inter_log = b_c + m
        m_t = jnp.maximum(inter_log, jnp.max(dmat, axis=2, keepdims=True))
        w_intra = jnp.exp(dmat - m_t)
        w_inter = jnp.exp(inter_log - m_t)
        qb = q3.astype(BF16)
        s = jnp.einsum('htd,hsd->hts', qb, k3.astype(BF16), preferred_element_type=F32) * w_intra
        state, nrm = s_ref[...], n_ref[...]
        num = (w_inter * jnp.einsum('htd,hvd->htv', qb, state.astype(BF16), preferred_element_type=F32)
               + jnp.einsum('hts,hsv->htv', s.astype(BF16), vb, preferred_element_type=F32))
        den = w_inter * jnp.sum(q3 * nrm, axis=2, keepdims=True) + jnp.sum(s, axis=2, keepdims=True)
        hc = num / jnp.maximum(jnp.abs(den), jnp.exp(-m_t))
        hc = hc * lax.rsqrt(jnp.mean(hc * hc, axis=2, keepdims=True) + RMS_EPS)
        for h in range(ML_HEADS):
            cols = head_cols(h)
            y_ref[rows, cols] = _sigmoid(og_ref[rows, cols]) * (hc[h] * ng_ref[:, cols])
        b_last = b_c[:, cs - 1:cs, :]
        gs_c = b_last - b_c + i_c
        gs_r = b_last - b_r + i_r
        m_new = jnp.maximum(b_last + m, jnp.max(gs_r, axis=2, keepdims=True))
        dec = jnp.exp(b_last + m - m_new)
        kw = k3 * jnp.exp(gs_c - m_new)
        s_ref[...] = dec * state + jnp.einsum('hvt,htd->hvd', vtb, kw.astype(BF16), preferred_element_type=F32)
        n_ref[...] = dec * nrm + jnp.sum(kw, axis=1, keepdims=True)
        m_ref[...] = jnp.broadcast_to(m_new, m_ref.shape)


def _mlstm(qk, v, og, gates, v_t, gates_t, conv_w, conv_b, norm_g, bsz, seq):
    tl = SEQ_TILE
    nl = seq // tl
    blk = lambda n, off: pl.BlockSpec((tl, n), lambda b, l: (b * nl + l, off))
    blk_t = lambda n: pl.BlockSpec((n, tl), lambda b, l: (0, b * nl + l))
    par = lambda rows, off: pl.BlockSpec((rows, ML_W), lambda b, l: (0, off))
    state = [pltpu.VMEM((ML_HEADS, ML_DH, ML_DH), F32), pltpu.VMEM((ML_HEADS, 1, ML_DH), F32),
             pltpu.VMEM((ML_HEADS, 1, ML_DH), F32)]
    return pl.pallas_call(
        _mlstm_body,
        grid=(bsz, nl),
        in_specs=[blk(ML_W, 0), blk(ML_W, 1), blk(ML_W, 0), blk(ML_W, 0), blk(gates.shape[1], 0),
                  blk_t(ML_W), blk_t(gates_t.shape[0]),
                  par(CONV_W, 0), par(CONV_W, 1), par(1, 0), par(1, 1), par(1, 0)],
        out_specs=blk(ML_W, 0),
        out_shape=jax.ShapeDtypeStruct((bsz * seq, ML_W), F32),
        scratch_shapes=[pltpu.VMEM((8, ML_W), F32), pltpu.VMEM((8, ML_W), F32),
                        pltpu.VMEM((tl, ML_W), F32), pltpu.VMEM((tl, ML_W), F32)] + state,
        compiler_params=_cparams("parallel", "arbitrary"),
        name="mlstm",
    )(qk, qk, v, og, gates, v_t, gates_t, conv_w, conv_w, conv_b.reshape(1, -1), conv_b.reshape(1, -1),
      norm_g.reshape(1, ML_W))


def _router_body(x_ref, mod_ref, rw_ref, rb_ref, info_ref, cnt_ref, carry_ref):
    tr = x_ref.shape[0]

    @pl.when(pl.program_id(0) == 0)
    def _():
        carry_ref[...] = jnp.zeros_like(carry_ref)

    h = x_ref[...] * (1.0 + mod_ref[0, 4:5, :]) + mod_ref[0, 3:4, :]
    logits = lax.dot_general(rw_ref[...], h, (((1,), (1,)), ((), ())), precision=lax.Precision.HIGHEST,
                             preferred_element_type=F32)
    aff = _sigmoid(logits)
    sel = aff + rb_ref[...]
    s = [sel[e:e + 1, :] for e in range(N_EXPERTS)]
    a = [aff[e:e + 1, :] for e in range(N_EXPERTS)]

    def top2_sum(v):
        best = v[0] + v[1]
        for lo_i, hi_i in zip(PAIR_LO[1:], PAIR_HI[1:]):
            best = jnp.maximum(best, v[lo_i] + v[hi_i])
        return best

    score = [top2_sum(s[EPG * g:EPG * (g + 1)]) for g in range(N_GROUPS)]
    g_idx = jnp.zeros((1, tr), jnp.int32)
    best = score[0]
    for g in range(1, N_GROUPS):
        upd = score[g] > best
        g_idx = jnp.where(upd, g, g_idx)
        best = jnp.where(upd, score[g], best)

    def pick_group(rows, j):
        out = rows[j]
        for g in range(1, N_GROUPS):
            out = jnp.where(g_idx == g, rows[EPG * g + j], out)
        return out

    sg = [pick_group(s, j) for j in range(EPG)]
    ag = [pick_group(a, j) for j in range(EPG)]
    first = jnp.zeros((1, tr), jnp.int32)
    best = sg[0]
    for j in range(1, EPG):
        upd = sg[j] > best
        first = jnp.where(upd, j, first)
        best = jnp.where(upd, sg[j], best)
    second = jnp.zeros((1, tr), jnp.int32)
    best2 = jnp.full((1, tr), -jnp.inf, F32)
    for j in range(EPG):
        upd = (first != j) & (sg[j] > best2)
        second = jnp.where(upd, j, second)
        best2 = jnp.where(upd, sg[j], best2)
    lo = jnp.minimum(first, second)
    hi = jnp.maximum(first, second)
    pair = jnp.where(lo == 0, hi - 1, jnp.where(lo == 1, hi + 1, N_PAIRS - 1))
    cls = g_idx * N_PAIRS + pair

    def pick_local(rows, idx):
        out = rows[0]
        for j in range(1, EPG):
            out = jnp.where(idx == j, rows[j], out)
        return out

    a_lo, a_hi = pick_local(ag, lo), pick_local(ag, hi)
    tot = a_lo + a_hi
    onehot = (lax.broadcasted_iota(jnp.int32, (32, tr), 0) == cls).astype(F32)
    before = (lax.broadcasted_iota(jnp.int32, (tr, tr), 0) < lax.broadcasted_iota(jnp.int32, (tr, tr), 1)).astype(BF16)
    cum = _dot(onehot.astype(BF16), before) + carry_ref[:, 0:1]
    rank = jnp.sum(onehot * cum, axis=0, keepdims=True)
    carry_ref[...] = carry_ref[...] + jnp.sum(onehot, axis=1, keepdims=True)
    cnt_ref[0] = carry_ref[...]
    zero = jnp.zeros((1, tr), F32)
    info_ref[...] = jnp.concatenate([cls.astype(F32), rank, a_lo / tot, a_hi / tot, zero, zero, zero, zero], axis=0)


def _router(x, mod6, router_w, router_bias, seq):
    t = x.shape[0]
    tr = ROUTE_TILE
    per_b = seq // tr
    return pl.pallas_call(
        _router_body,
        grid=(t // tr,),
        in_specs=[pl.BlockSpec((tr, D_MODEL), lambda i: (i, 0)),
                  pl.BlockSpec((1, 6, D_MODEL), lambda i: (i // per_b, 0, 0)),
                  pl.BlockSpec((N_EXPERTS, D_MODEL), lambda i: (0, 0)),
                  pl.BlockSpec((N_EXPERTS, 1), lambda i: (0, 0))],
        out_specs=[pl.BlockSpec((8, tr), lambda i: (0, i)), pl.BlockSpec((1, 32, 128), lambda i: (i, 0, 0))],
        out_shape=[jax.ShapeDtypeStruct((8, t), F32), jax.ShapeDtypeStruct((t // tr, 32, 128), F32)],
        scratch_shapes=[pltpu.VMEM((32, 128), F32)],
        compiler_params=_cparams("arbitrary"),
        name="router",
    )(x, mod6, router_w.T.astype(F32), router_bias.reshape(N_EXPERTS, 1).astype(F32))


def _from_row_tiles(ref):
    return jnp.concatenate([ref[:, j, :] for j in range(N_LANE_GROUPS)], axis=1)


def _store_row_tiles(ref, x):
    for j in range(N_LANE_GROUPS):
        ref[:, j, :] = x[:, 128 * j:128 * (j + 1)]


def _run_dmas(make_copy, length):
    off = jnp.int32(0)
    size = PERM_TILE
    while size >= 1:
        has = (length & size) != 0

        @pl.when(has)
        def _(off=off, size=size):
            make_copy(off, size).start()

        off = off + jnp.where(has, size, 0)
        size //= 2


def _dispatch_body(pos_ref, gstart_ref, lstart_ref, len_ref, x_ref, dst_in_ref, dst_ref, stage, ordered, sem):
    del dst_in_ref
    i = pl.program_id(0)
    tm = x_ref.shape[0]
    _store_row_tiles(stage, x_ref[...])

    def place(r, c):
        ordered[pos_ref[i * tm + r]] = stage[r]
        return c

    lax.fori_loop(0, tm, place, 0, unroll=8)
    for c in range(N_CLASSES):
        run = i * N_CLASSES + c
        g0, l0 = gstart_ref[run], lstart_ref[run]
        _run_dmas(lambda off, n: pltpu.make_async_copy(ordered.at[pl.ds(l0 + off, n)], dst_ref.at[pl.ds(g0 + off, n)],
                                                       sem.at[0]), len_ref[run])
    pltpu.make_async_copy(ordered, dst_ref.at[pl.ds(0, tm)], sem.at[0]).wait()


def _dispatch(pos, gstart, lstart, length, x, n_dst):
    t = x.shape[0]
    tm = PERM_TILE
    dst0 = jnp.zeros((n_dst, N_LANE_GROUPS, 128), F32)
    return pl.pallas_call(
        _dispatch_body,
        grid_spec=pltpu.PrefetchScalarGridSpec(
            num_scalar_prefetch=4,
            grid=(t // tm,),
            in_specs=[pl.BlockSpec((tm, D_MODEL), lambda i, *s: (i, 0)), pl.BlockSpec(memory_space=pl.ANY)],
            out_specs=pl.BlockSpec(memory_space=pl.ANY),
            scratch_shapes=[pltpu.VMEM((tm, N_LANE_GROUPS, 128), F32)] * 2 + [pltpu.SemaphoreType.DMA((1,))]),
        out_shape=jax.ShapeDtypeStruct(dst0.shape, F32),
        input_output_aliases={5: 0},
        compiler_params=_cparams("arbitrary"),
        name="moe_dispatch",
    )(pos, gstart, lstart, length, x, dst0)


def _combine_body(pos_ref, gstart_ref, lstart_ref, len_ref, src_ref, o_ref, ordered, stage, sem):
    i = pl.program_id(0)
    tm = o_ref.shape[0]
    for c in range(N_CLASSES):
        run = i * N_CLASSES + c
        g0, l0 = gstart_ref[run], lstart_ref[run]
        _run_dmas(lambda off, n: pltpu.make_async_copy(src_ref.at[pl.ds(g0 + off, n)], ordered.at[pl.ds(l0 + off, n)],
                                                       sem.at[0]), len_ref[run])
    pltpu.make_async_copy(src_ref.at[pl.ds(0, tm)], ordered, sem.at[0]).wait()

    def take(r, c):
        stage[r] = ordered[pos_ref[i * tm + r]]
        return c

    lax.fori_loop(0, tm, take, 0, unroll=8)
    o_ref[...] = _from_row_tiles(stage)


def _combine(pos, gstart, lstart, length, src):
    t = pos.shape[0]
    tm = PERM_TILE
    return pl.pallas_call(
        _combine_body,
        grid_spec=pltpu.PrefetchScalarGridSpec(
            num_scalar_prefetch=4,
            grid=(t // tm,),
            in_specs=[pl.BlockSpec(memory_space=pl.ANY)],
            out_specs=pl.BlockSpec((tm, D_MODEL), lambda i, *s: (i, 0)),
            scratch_shapes=[pltpu.VMEM((tm, N_LANE_GROUPS, 128), F32)] * 2 + [pltpu.SemaphoreType.DMA((1,))]),
        out_shape=jax.ShapeDtypeStruct((t, D_MODEL), F32),
        compiler_params=_cparams("arbitrary"),
        name="moe_combine",
    )(pos, gstart, lstart, length, src)


def _moe_body(e_lo_ref, e_hi_ref, blk_ref, new_ref, used_ref, x_ref, meta_ref, mod_ref,
              wg_lo32, wu_lo32, wd_lo32, wg_hi32, wu_hi32, wd_hi32, lg_ref, lb_ref, o_ref,
              wg_lo, wu_lo, wd_lo, wg_hi, wu_hi, wd_hi):
    del e_lo_ref, e_hi_ref, blk_ref
    i = pl.program_id(0)
    used = used_ref[0]

    @pl.when(new_ref[i] == 1)
    def _():
        for src, dst in ((wg_lo32, wg_lo), (wu_lo32, wu_lo), (wd_lo32, wd_lo),
                         (wg_hi32, wg_hi), (wu_hi32, wu_hi), (wd_hi32, wd_hi)):
            dst[...] = src[...].astype(BF16)

    @pl.when(i >= used)
    def _():
        o_ref[...] = jnp.zeros_like(o_ref)

    @pl.when(i < used)
    def _():
        x = _from_row_tiles(x_ref)
        meta = meta_ref[...]
        bid, g_lo, g_hi = meta[:, 0:1], meta[:, 1:2], meta[:, 2:3]
        nb = mod_ref.shape[0]

        def per_row(r):
            out = mod_ref[0, r:r + 1, :]
            for b in range(1, nb):
                out = jnp.where(bid == float(b), mod_ref[b, r:r + 1, :], out)
            return out

        h = (x * (1.0 + per_row(4)) + per_row(3)).astype(BF16)

        def expert(wg, wu, wd):
            he = _silu(_dot(h, wg[...])) * _dot(h, wu[...])
            return _dot(he.astype(BF16), wd[...])

        y = g_lo * expert(wg_lo, wu_lo, wd_lo) + g_hi * expert(wg_hi, wu_hi, wd_hi)
        _store_row_tiles(o_ref, _layer_norm(ALPHA * x + (1.0 + per_row(5)) * y, lg_ref[...], lb_ref[...]))


def _moe_tiles(e_lo, e_hi, blk, new, used, xs, meta, mod6, layer, w_gate, w_up, w_down, ln_g, ln_b):
    tp = xs.shape[0]
    tm = ROW_TILE
    tiles = lambda index: pl.BlockSpec((tm, N_LANE_GROUPS, 128), index)
    wspec = lambda shape, which: pl.BlockSpec((None, None) + shape, lambda i, *s: (layer, s[which][i], 0, 0))
    full = lambda a: pl.BlockSpec(a.shape, lambda i, *s: (0,) * a.ndim)
    lg, lb = ln_g.reshape(1, D_MODEL), ln_b.reshape(1, D_MODEL)
    gu, dn = (D_MODEL, D_EXPERT), (D_EXPERT, D_MODEL)
    return pl.pallas_call(
        _moe_body,
        grid_spec=pltpu.PrefetchScalarGridSpec(
            num_scalar_prefetch=5,
            grid=(tp // tm,),
            in_specs=[tiles(lambda i, *s: (s[2][i], 0, 0)), pl.BlockSpec((tm, 8), lambda i, *s: (s[2][i], 0)), full(mod6),
                      wspec(gu, 0), wspec(gu, 0), wspec(dn, 0), wspec(gu, 1), wspec(gu, 1), wspec(dn, 1),
                      full(lg), full(lb)],
            out_specs=tiles(lambda i, *s: (i, 0, 0)),
            scratch_shapes=[pltpu.VMEM(gu, BF16), pltpu.VMEM(gu, BF16), pltpu.VMEM(dn, BF16)] * 2),
        out_shape=jax.ShapeDtypeStruct((tp, N_LANE_GROUPS, 128), F32),
        compiler_params=_cparams("arbitrary"),
        name="moe_tiles",
    )(e_lo, e_hi, blk, new, used, xs, meta, mod6, w_gate, w_up, w_down, w_gate, w_up, w_down, lg, lb)


def _moe_layer(x, mod6, router_w, router_bias, layer, w_gate, w_up, w_down, ln_g, ln_b, bsz, seq):
    t = x.shape[0]
    tm = ROW_TILE
    n_tiles = t // tm + N_CLASSES
    tp = n_tiles * tm
    info, seen = _router(x, mod6, router_w, router_bias, seq)
    cls = info[0].astype(jnp.int32)
    rank = info[1].astype(jnp.int32)
    seen = seen[:, :N_CLASSES, 0].astype(jnp.int32)
    cnt = seen[-1]
    tiles_c = (cnt + tm - 1) // tm
    tile_end = jnp.cumsum(tiles_c)
    offs = (tile_end - tiles_c) * tm
    onehot = cls[:, None] == jnp.arange(N_CLASSES)[None, :]
    dest = rank + jnp.sum(jnp.where(onehot, offs[None, :], 0), axis=1)
    used = tile_end[-1]
    tile_id = jnp.arange(n_tiles)
    blk = jnp.maximum(jnp.minimum(tile_id, used - 1), 0)
    tile_cls = jnp.minimum(jnp.sum(blk[:, None] >= tile_end[None, :], axis=1), N_CLASSES - 1)
    grp, pair = tile_cls // N_PAIRS, tile_cls % N_PAIRS
    e_lo = (grp * EPG + jnp.asarray(PAIR_LO, jnp.int32)[pair]).astype(jnp.int32)
    e_hi = (grp * EPG + jnp.asarray(PAIR_HI, jnp.int32)[pair]).astype(jnp.int32)
    new = jnp.concatenate([jnp.ones((1,), jnp.int32), (tile_cls[1:] != tile_cls[:-1]).astype(jnp.int32)])
    bid = (jnp.arange(t) // seq).astype(F32)
    meta_nat = jnp.stack([bid, info[2], info[3]] + [jnp.zeros((t,), F32)] * 5, axis=1)
    meta = jnp.zeros((tp, 8), F32).at[dest].set(meta_nat)
    pt = PERM_TILE
    before = jnp.concatenate([jnp.zeros((1, N_CLASSES), jnp.int32), seen[:-1]], axis=0)
    length = seen - before
    lstart = jnp.cumsum(length, axis=1) - length
    gstart = offs[None, :] + before
    shift = jnp.broadcast_to((lstart - before)[:, None, :], (t // pt, pt, N_CLASSES)).reshape(t, N_CLASSES)
    pos = rank + jnp.sum(jnp.where(onehot, shift, 0), axis=1)
    runs = (pos.astype(jnp.int32), gstart.reshape(-1).astype(jnp.int32), lstart.reshape(-1).astype(jnp.int32),
            length.reshape(-1).astype(jnp.int32))
    xs = _dispatch(*runs, x, tp)
    ys = _moe_tiles(e_lo, e_hi, blk.astype(jnp.int32), new, used.reshape(1).astype(jnp.int32), xs, meta,
                    mod6, layer, w_gate, w_up, w_down, ln_g, ln_b)
    return _combine(*runs, ys)


def kernel(x, c, router_w, router_bias, ada_w, ada_b, ln1_g, ln1_b, ln2_g, ln2_b, moe_w_gate, moe_w_up, moe_w_down,
           ab_w_in, s5_lam_re, s5_lam_im, s5_log_dt, s5_b_re, s5_b_im, s5_c_re, s5_c_im, s5_d, s5_glu_w, s5_glu_b,
           gla_gate_w, gla_gate_b, gla_norm_g, ab_w_out, ml_w_in, ml_conv_w, ml_conv_b, ml_igate_b, ml_fgate_b,
           ml_norm_g, ml_w_out):
    bsz, seq, d = x.shape
    t = bsz * seq
    mod = _ada_mod(c, ada_w, ada_b)
    xt = x.reshape(t, d)
    for layer in range(DEPTH):
        j = layer // 2
        mod6 = mod[layer]
        if layer % 2 == 0:
            u, q, k, v, r, la = _ab_in(xt, mod6, ab_w_in[j], gla_gate_w[j], gla_gate_b[j], seq)
            tables = _s5_prep(s5_lam_re[j], s5_lam_im[j], s5_log_dt[j], s5_b_re[j], s5_b_im[j], s5_c_re[j], s5_c_im[j])
            ys = _s5_conv(u, tables, bsz, seq)
            yb = _gla(q, k, la, v, r, gla_norm_g[j], bsz, seq)
            w_out = ab_w_out[j].astype(BF16)
            consts = [s5_d[j].reshape(1, S5_W), s5_glu_w[j].astype(BF16), s5_glu_b[j].reshape(1, S5_W),
                      w_out[:S5_W], w_out[S5_W:]]
            xt = _out_call(_ab_out_body, "ab_out", [ys, u, yb], xt, mod6, consts, ln1_g[layer], ln1_b[layer], seq)
        else:
            qk, v, og, gates, v_t, gates_t = _ml_in(xt, mod6, ml_w_in[j], ml_igate_b[j], ml_fgate_b[j], seq)
            y = _mlstm(qk, v, og, gates, v_t, gates_t, ml_conv_w[j], ml_conv_b[j], ml_norm_g[j], bsz, seq)
            xt = _out_call(_ml_out_body, "ml_out", [y], xt, mod6, [ml_w_out[j].astype(BF16)],
                           ln1_g[layer], ln1_b[layer], seq)
        xt = _moe_layer(xt, mod6, router_w, router_bias, layer, moe_w_gate, moe_w_up, moe_w_down,
                        ln2_g[layer], ln2_b[layer], bsz, seq)
    return xt.reshape(bsz, seq, d)
```

```python
import functools
import math

import jax
import jax.numpy as jnp
from jax import lax
from jax.experimental import pallas as pl
from jax.experimental.pallas import tpu as pltpu

F32 = jnp.float32
BF16 = jnp.bfloat16

D_MODEL = 1024
DEPTH = 4
S5_W = 512
S5_GROUP = 16
S5_GROUPS = 32
S5_STATE = 64
S5_CHUNK = 16
S5_SCAN_ROWS = 8
GLA_HEADS = 4
GLA_DV = 128
GLA_DK = 64
GLA_QK = 256
GLA_V = 512
GLA_RANK = 16
GLA_GATE_NORM = 16.0
GLA_SUB = 16
AB_IN = S5_W + 2 * GLA_QK + 2 * GLA_V + GLA_RANK
AB_IN_PAD = 2176
ML_HEADS = 8
ML_W = 1024
ML_DH = 128
ML_IN = 4 * ML_W + 2 * ML_HEADS
ML_IN_PAD = 4224
CONV_W = 4
CHUNK = 64
ML_CHUNK = 128
N_EXPERTS = 16
N_GROUPS = 4
EPG = 4
D_EXPERT = 256
N_PAIRS = 6
N_CLASSES = N_GROUPS * N_PAIRS
PAIR_LO = (0, 0, 0, 1, 1, 2)
PAIR_HI = (1, 2, 3, 2, 3, 3)
ALPHA = (2.0 * DEPTH) ** 0.25
LN_EPS = 1e-5
RMS_EPS = 1e-6

V7X_VMEM_BYTES = 64 * 1024 * 1024
VMEM_LIMIT = (V7X_VMEM_BYTES * 3) // 4
ROW_TILE = 256
SEQ_TILE = 256
ROUTE_TILE = 512
PERM_TILE = 512
assert ROUTE_TILE == PERM_TILE
N_LANE_GROUPS = D_MODEL // 128


def _cparams(*sem):
    return pltpu.CompilerParams(dimension_semantics=sem, vmem_limit_bytes=VMEM_LIMIT)


def _sigmoid(x):
    return 1.0 / (1.0 + jnp.exp(-x))


def _silu(x):
    return x * _sigmoid(x)


def _log_sigmoid(z):
    return jnp.minimum(z, 0.0) - jnp.log1p(jnp.exp(-jnp.abs(z)))


def _gelu_tanh(x):
    return 0.5 * x * (1.0 + jnp.tanh(math.sqrt(2.0 / math.pi) * (x + 0.044715 * (x * x * x))))


def _layer_norm(x, g, b):
    mu = jnp.mean(x, axis=-1, keepdims=True)
    xc = x - mu
    var = jnp.mean(xc * xc, axis=-1, keepdims=True)
    return xc * lax.rsqrt(var + LN_EPS) * g + b


def _split3_bf16(x):
    hi = x.astype(BF16)
    r = x - hi.astype(F32)
    mid = r.astype(BF16)
    lo = (r - mid.astype(F32)).astype(BF16)
    return hi, mid, lo


def _dot(a, b):
    return jnp.dot(a, b, preferred_element_type=F32)


def _dot_nt(a, b):
    return lax.dot_general(a, b, (((1,), (1,)), ((), ())), preferred_element_type=F32)


def _dot_tn(a, b):
    return lax.dot_general(a, b, (((0,), (0,)), ((), ())), preferred_element_type=F32)


def _ada_body(c_ref, w_ref, b_ref, o_ref):
    c = c_ref[...]
    o_ref[0] = _dot(_silu(c), w_ref[0]) + b_ref[0]


def _ada_mod(c, ada_w, ada_b):
    bsz = c.shape[0]
    tn = 1536
    cp = jnp.zeros((8, D_MODEL), F32).at[:bsz].set(c)
    out = pl.pallas_call(
        _ada_body,
        grid=(DEPTH, 6 * D_MODEL // tn),
        in_specs=[pl.BlockSpec((8, D_MODEL), lambda l, j: (0, 0)),
                  pl.BlockSpec((1, D_MODEL, tn), lambda l, j: (l, 0, j)),
                  pl.BlockSpec((1, 1, tn), lambda l, j: (l, 0, j))],
        out_specs=pl.BlockSpec((1, 8, tn), lambda l, j: (l, 0, j)),
        out_shape=jax.ShapeDtypeStruct((DEPTH, 8, 6 * D_MODEL), F32),
        compiler_params=_cparams("parallel", "parallel"),
        name="ada_mod",
    )(cp, ada_w, ada_b.reshape(DEPTH, 1, 6 * D_MODEL))
    return out[:, :bsz].reshape(DEPTH, bsz, 6, D_MODEL)


def _ab_in_body(x_ref, mod_ref, w_ref, gw_ref, gb_ref, u_ref, q_ref, k_ref, v_ref, r_ref, la_ref):
    x = x_ref[...]
    h = (x * (1.0 + mod_ref[0, 1:2, :]) + mod_ref[0, 0:1, :]).astype(BF16)

    def seg(a, b):
        return _dot(h, w_ref[:, a:b])

    u_ref[...] = seg(0, 512)
    q_ref[...] = seg(512, 768)
    k_ref[...] = seg(768, 1024)
    v_ref[...] = seg(1024, 1536)
    r_ref[...] = seg(1536, 2048)
    a_lr = seg(2048, AB_IN_PAD)
    z = _dot(a_lr.astype(BF16), gw_ref[...]) + gb_ref[...]
    la_ref[...] = _log_sigmoid(z) * (1.0 / GLA_GATE_NORM)


def _ab_in(x, mod6, w_in, gate_w, gate_b, seq):
    t = x.shape[0]
    tm = ROW_TILE
    per_b = seq // tm
    w = jnp.zeros((D_MODEL, AB_IN_PAD), BF16).at[:, :AB_IN].set(w_in.astype(BF16))
    gw = jnp.zeros((AB_IN_PAD - 2048, GLA_QK), BF16).at[:GLA_RANK].set(gate_w.astype(BF16))
    row = lambda n: pl.BlockSpec((tm, n), lambda i: (i, 0))
    full = lambda a: pl.BlockSpec(a.shape, lambda i: (0,) * a.ndim)
    outs = [S5_W, GLA_QK, GLA_QK, GLA_V, GLA_V, GLA_QK]
    return pl.pallas_call(
        _ab_in_body,
        grid=(t // tm,),
        in_specs=[row(D_MODEL),
                  pl.BlockSpec((1, 6, D_MODEL), lambda i: (i // per_b, 0, 0)),
                  full(w), full(gw), pl.BlockSpec((1, GLA_QK), lambda i: (0, 0))],
        out_specs=[row(n) for n in outs],
        out_shape=[jax.ShapeDtypeStruct((t, n), F32) for n in outs],
        compiler_params=_cparams("parallel"),
        name="ab_in",
    )(x, mod6, w, gw, gate_b.reshape(1, GLA_QK))


def _s5_prep(lam_re, lam_im, log_dt, b_re, b_im, c_re, c_im):
    hp = lax.Precision.HIGHEST
    n = S5_CHUNK
    lre, lim = lam_re.astype(F32), lam_im.astype(F32)
    dt = jnp.exp(log_dt.astype(F32))[:, None]
    d = jnp.arange(n + 1, dtype=F32)[:, None, None]
    mag = jnp.exp(lre * dt * d)
    ang = lim * dt * d
    pw_re, pw_im = mag * jnp.cos(ang), mag * jnp.sin(ang)
    den = lre * lre + lim * lim
    nre, nim = pw_re[1] - 1.0, pw_im[1]
    coef_re = (nre * lre + nim * lim) / den
    coef_im = (nim * lre - nre * lim) / den
    bre, bim = b_re.astype(F32), b_im.astype(F32)
    bb_re = coef_re[..., None] * bre - coef_im[..., None] * bim
    bb_im = coef_re[..., None] * bim + coef_im[..., None] * bre
    cre, cim = c_re.astype(F32), c_im.astype(F32)
    cp_re = cre[None] * pw_re[:, :, None, :] - cim[None] * pw_im[:, :, None, :]
    cp_im = cre[None] * pw_im[:, :, None, :] + cim[None] * pw_re[:, :, None, :]
    kern = (jnp.einsum('dgcp,gpe->dgce', cp_re, bb_re, precision=hp)
            - jnp.einsum('dgcp,gpe->dgce', cp_im, bb_im, precision=hp))
    jj = jnp.arange(n)[:, None]
    tt = jnp.arange(n)[None, :]
    lag = tt - jj
    kg = jnp.where((lag >= 0)[:, :, None, None, None], kern[jnp.clip(lag, 0, n)], 0.0)
    m_intra = jnp.transpose(kg, (2, 0, 4, 1, 3)).reshape(S5_GROUPS, n * S5_GROUP, n * S5_GROUP)
    pws_re, pws_im = pw_re[n - 1 - jnp.arange(n)], pw_im[n - 1 - jnp.arange(n)]
    ws_re = pws_re[..., None] * bb_re[None] - pws_im[..., None] * bb_im[None]
    ws_im = pws_re[..., None] * bb_im[None] + pws_im[..., None] * bb_re[None]
    w_state = jnp.transpose(jnp.concatenate([ws_re, ws_im], axis=2), (1, 0, 3, 2))
    w_state = w_state.reshape(S5_GROUPS, n * S5_GROUP, 2 * S5_STATE)
    wi_re = jnp.transpose(cp_re[1:], (1, 3, 0, 2))
    wi_im = -jnp.transpose(cp_im[1:], (1, 3, 0, 2))
    w_inter = jnp.concatenate([wi_re, wi_im], axis=1).reshape(S5_GROUPS, 2 * S5_STATE, n * S5_GROUP)
    r = (n * jnp.arange(S5_SCAN_ROWS + 1, dtype=F32))[None, :, None]
    cmag, cang = jnp.exp(lre[:, None, :] * dt[:, None, :] * r), lim[:, None, :] * dt[:, None, :] * r
    a_pows = jnp.concatenate([cmag * jnp.cos(cang), cmag * jnp.sin(cang)], axis=1)
    return m_intra.astype(BF16), w_state.astype(BF16), w_inter.astype(BF16), a_pows


def _s5_body(u_ref, m_ref, ws_ref, wi_ref, a_ref, y_ref, ere, eim, sre, sim, *, bsz):
    u = u_ref[0]
    e = _dot(u, ws_ref[0])
    ere[...] = e[:, :S5_STATE]
    eim[...] = e[:, S5_STATE:]
    n_pow = S5_SCAN_ROWS + 1
    pr, pi = a_ref[0, 0:n_pow, :], a_ref[0, n_pow:2 * n_pow, :]
    tab_r, tab_i = pr[0:S5_SCAN_ROWS], pi[0:S5_SCAN_ROWS]
    rowi = lax.broadcasted_iota(jnp.int32, (S5_SCAN_ROWS, S5_STATE), 0)
    nk = ere.shape[0] // bsz

    def cmul(a_r, a_i, x_r, x_i):
        return a_r * x_r - a_i * x_i, a_r * x_i + a_i * x_r

    def shifted(x, s):
        return jnp.where(rowi >= s, pltpu.roll(x, s, 0), 0.0)

    def tile(m, carry):
        out = []
        for b in range(bsz):
            c_r, c_i = carry[2 * b], carry[2 * b + 1]
            r0 = pl.multiple_of(b * nk + m * S5_SCAN_ROWS, S5_SCAN_ROWS)
            x_r, x_i = ere[pl.ds(r0, S5_SCAN_ROWS), :], eim[pl.ds(r0, S5_SCAN_ROWS), :]
            for s in (1, 2, 4):
                d_r, d_i = cmul(pr[s:s + 1], pi[s:s + 1], shifted(x_r, s), shifted(x_i, s))
                x_r, x_i = x_r + d_r, x_i + d_i
            t_r, t_i = cmul(tab_r, tab_i, c_r, c_i)
            sre[pl.ds(r0, S5_SCAN_ROWS), :] = shifted(x_r, 1) + t_r
            sim[pl.ds(r0, S5_SCAN_ROWS), :] = shifted(x_i, 1) + t_i
            n_r, n_i = cmul(pr[S5_SCAN_ROWS:n_pow], pi[S5_SCAN_ROWS:n_pow], c_r, c_i)
            last = slice(S5_SCAN_ROWS - 1, S5_SCAN_ROWS)
            out += [n_r + jnp.broadcast_to(x_r[last], c_r.shape), n_i + jnp.broadcast_to(x_i[last], c_i.shape)]
        return tuple(out)

    z = jnp.zeros((S5_SCAN_ROWS, S5_STATE), F32)
    lax.fori_loop(0, nk // S5_SCAN_ROWS, tile, (z,) * (2 * bsz))
    y = _dot(u, m_ref[0])
    y = y + _dot(sre[...].astype(BF16), wi_ref[0, :S5_STATE, :])
    y = y + _dot(sim[...].astype(BF16), wi_ref[0, S5_STATE:, :])
    y_ref[0] = y


def _s5_pack_body(u_ref, o_ref):
    nk = u_ref.shape[0] // S5_CHUNK
    by_step = [u_ref[pl.ds(t, nk, stride=S5_CHUNK), :] for t in range(S5_CHUNK)]
    for g in range(128 // S5_GROUP):
        cols = slice(S5_GROUP * g, S5_GROUP * (g + 1))
        o_ref[g] = jnp.concatenate([x[:, cols] for x in by_step], axis=1).astype(o_ref.dtype)


def _s5_unpack_body(y_ref, o_ref):
    nk = y_ref.shape[1]
    groups = [y_ref[g] for g in range(128 // S5_GROUP)]
    for t in range(S5_CHUNK):
        cols = slice(S5_GROUP * t, S5_GROUP * (t + 1))
        o_ref[pl.ds(t, nk, stride=S5_CHUNK), :] = jnp.concatenate([y[:, cols] for y in groups], axis=1)


def _s5_conv(u, tables, bsz, seq):
    m_intra, w_state, w_inter, a_pows = tables
    t = bsz * seq
    rows = t // S5_CHUNK
    lanes = S5_CHUNK * S5_GROUP
    tm = PERM_TILE
    gpb = 128 // S5_GROUP
    tok = pl.BlockSpec((tm, 128), lambda i, j: (i, j))
    chunked = pl.BlockSpec((gpb, tm // S5_CHUNK, lanes), lambda i, j: (j, i, 0))
    uf = pl.pallas_call(
        _s5_pack_body,
        grid=(t // tm, S5_W // 128),
        in_specs=[tok],
        out_specs=chunked,
        out_shape=jax.ShapeDtypeStruct((S5_GROUPS, rows, lanes), BF16),
        compiler_params=_cparams("parallel", "parallel"),
        name="s5_pack",
    )(u)
    grp = lambda shape: pl.BlockSpec((1,) + shape, lambda g: (g, 0, 0))
    yf = pl.pallas_call(
        functools.partial(_s5_body, bsz=bsz),
        grid=(S5_GROUPS,),
        in_specs=[grp((rows, lanes)), grp((lanes, lanes)), grp((lanes, 2 * S5_STATE)),
                  grp((2 * S5_STATE, lanes)), grp(a_pows.shape[1:])],
        out_specs=grp((rows, lanes)),
        out_shape=jax.ShapeDtypeStruct((S5_GROUPS, rows, lanes), F32),
        scratch_shapes=[pltpu.VMEM((rows, S5_STATE), F32)] * 4,
        compiler_params=_cparams("parallel"),
        name="s5_conv",
    )(uf, m_intra, w_state, w_inter, a_pows)
    return pl.pallas_call(
        _s5_unpack_body,
        grid=(t // tm, S5_W // 128),
        in_specs=[chunked],
        out_specs=tok,
        out_shape=jax.ShapeDtypeStruct((t, S5_W), F32),
        compiler_params=_cparams("parallel", "parallel"),
        name="s5_unpack",
    )(yf)


def _gla_body(q_ref, k_ref, g_ref, v_ref, r_ref, ng_ref, y_ref, s_ref, bc_ref, o_ref):
    tl = q_ref.shape[0]
    scale = GLA_DK ** -0.5

    @pl.when(pl.program_id(1) == 0)
    def _():
        s_ref[...] = jnp.zeros_like(s_ref)

    row = lax.broadcasted_iota(jnp.int32, (CHUNK, CHUNK), 0)
    col = lax.broadcasted_iota(jnp.int32, (CHUNK, CHUNK), 1)
    tri = col <= row
    tref = col < (row // GLA_SUB) * GLA_SUB
    cum_mat = jnp.concatenate([tri, tref], axis=0).astype(BF16)
    lane_head = lax.broadcasted_iota(jnp.int32, (GLA_SUB, GLA_QK), 1) // GLA_DK
    bd_mask = (lax.broadcasted_iota(jnp.int32, (GLA_QK, GLA_V), 0) // GLA_DK
               == lax.broadcasted_iota(jnp.int32, (GLA_QK, GLA_V), 1) // GLA_DV)
    ones_bd = bd_mask.astype(BF16)
    ones_v = jnp.ones((CHUNK, GLA_V), BF16)

    for c in range(tl // CHUNK):
        r0 = c * CHUNK
        qc = q_ref[pl.ds(r0, CHUNK), :] * scale
        kc = k_ref[pl.ds(r0, CHUNK), :]
        g_parts = _split3_bf16(g_ref[pl.ds(r0, CHUNK), :])
        vb = v_ref[pl.ds(r0, CHUNK), :].astype(BF16)
        br = sum(_dot(cum_mat, part) for part in g_parts)
        bc, ref = br[:CHUNK], br[CHUNK:]
        bc_ref[pl.ds(r0, CHUNK), :] = bc
        qt = qc * jnp.exp(bc - ref)
        state = s_ref[...]
        o = _dot((qt * jnp.exp(ref)).astype(BF16), state.astype(BF16))
        blocks = [o[:GLA_SUB]]
        for i in range(1, CHUNK // GLA_SUB):
            ri = bc[GLA_SUB * i - 1:GLA_SUB * i, :]
            kt = (kc * jnp.exp(jnp.minimum(ri - bc, 0.0))).astype(BF16)
            qi = qt[GLA_SUB * i:GLA_SUB * (i + 1)]
            lhs = jnp.concatenate([jnp.where(lane_head == h, qi, 0.0) for h in range(GLA_HEADS)], axis=0)
            att = _dot_nt(lhs.astype(BF16), kt)
            att = jnp.where(col < GLA_SUB * i, att, 0.0)
            ov = _dot(att.astype(BF16), vb)
            oi = jnp.concatenate([ov[GLA_SUB * h:GLA_SUB * (h + 1), GLA_DV * h:GLA_DV * (h + 1)]
                                  for h in range(GLA_HEADS)], axis=1)
            blocks.append(o[GLA_SUB * i:GLA_SUB * (i + 1)] + oi)
        o_ref[pl.ds(r0, CHUNK), :] = jnp.concatenate(blocks, axis=0)
        bl = bc[CHUNK - 1:CHUNK, :]
        kh = (kc * jnp.exp(bl - bc)).astype(BF16)
        upd = _dot_tn(kh, vb)
        dcol = sum(_dot_tn(part, ones_v) for part in g_parts)
        s_ref[...] = jnp.exp(dcol) * state + jnp.where(bd_mask, upd, 0.0)

    q = q_ref[...] * scale
    k = k_ref[...]
    v = v_ref[...]
    bc = bc_ref[...]
    rmod = lax.broadcasted_iota(jnp.int32, (tl, 1), 0) % GLA_SUB
    od = jnp.zeros((tl, GLA_V), F32)
    for d in range(GLA_SUB):
        kd = k if d == 0 else pltpu.roll(k, d, 0)
        bd = bc if d == 0 else pltpu.roll(bc, d, 0)
        vd = v if d == 0 else pltpu.roll(v, d, 0)
        valid = rmod >= d
        e = jnp.exp(jnp.where(valid, bc - bd, 0.0))
        p = jnp.where(valid, q * kd * e, 0.0).astype(BF16)
        od = od + _dot(p, ones_bd) * vd
    o = o_ref[...] + od
    ng = ng_ref[...]
    outs = []
    for h in range(GLA_HEADS):
        oh = o[:, GLA_DV * h:GLA_DV * (h + 1)]
        outs.append(oh * lax.rsqrt(jnp.mean(oh * oh, axis=-1, keepdims=True) + RMS_EPS))
    y_ref[...] = jnp.concatenate(outs, axis=1) * ng * _silu(r_ref[...])


def _gla(q, k, la, v, r, norm_g, bsz, seq):
    tl = SEQ_TILE
    nl = seq // tl
    blk = lambda n: pl.BlockSpec((tl, n), lambda b, l: (b * nl + l, 0))
    return pl.pallas_call(
        _gla_body,
        grid=(bsz, nl),
        in_specs=[blk(GLA_QK), blk(GLA_QK), blk(GLA_QK), blk(GLA_V), blk(GLA_V),
                  pl.BlockSpec((1, GLA_V), lambda b, l: (0, 0))],
        out_specs=blk(GLA_V),
        out_shape=jax.ShapeDtypeStruct((bsz * seq, GLA_V), F32),
        scratch_shapes=[pltpu.VMEM((GLA_QK, GLA_V), F32), pltpu.VMEM((tl, GLA_QK), F32),
                        pltpu.VMEM((tl, GLA_V), F32)],
        compiler_params=_cparams("parallel", "arbitrary"),
        name="gla",
    )(q, k, la, v, r, norm_g.reshape(1, GLA_V))


def _residual_ln(x, y, mod_ref, gate_row, lg_ref, lb_ref):
    return _layer_norm(ALPHA * x + (1.0 + mod_ref[0, gate_row:gate_row + 1, :]) * y, lg_ref[...], lb_ref[...])


def _ab_out_body(ys_ref, u_ref, yb_ref, x_ref, mod_ref, d_ref, gw_ref, gb_ref, wa_ref, wb_ref, lg_ref, lb_ref, o_ref):
    z = _gelu_tanh(ys_ref[...] + d_ref[...] * u_ref[...])
    ya = z * _sigmoid(_dot(z.astype(BF16), gw_ref[...]) + gb_ref[...])
    y = _dot(ya.astype(BF16), wa_ref[...]) + _dot(yb_ref[...].astype(BF16), wb_ref[...])
    o_ref[...] = _residual_ln(x_ref[...], y, mod_ref, 2, lg_ref, lb_ref)


def _ml_out_body(y_ref, x_ref, mod_ref, w_ref, lg_ref, lb_ref, o_ref):
    y = _dot(y_ref[...].astype(BF16), w_ref[...])
    o_ref[...] = _residual_ln(x_ref[...], y, mod_ref, 2, lg_ref, lb_ref)


def _out_call(body, name, row_inputs, x, mod6, consts, ln_g, ln_b, seq):
    t = x.shape[0]
    tm = ROW_TILE
    per_b = seq // tm
    row = lambda a: pl.BlockSpec((tm, a.shape[1]), lambda i: (i, 0))
    full = lambda a: pl.BlockSpec(a.shape, lambda i: (0,) * a.ndim)
    lg, lb = ln_g.reshape(1, D_MODEL), ln_b.reshape(1, D_MODEL)
    return pl.pallas_call(
        body,
        grid=(t // tm,),
        in_specs=([row(a) for a in row_inputs] + [row(x), pl.BlockSpec((1, 6, D_MODEL), lambda i: (i // per_b, 0, 0))]
                  + [full(a) for a in consts] + [full(lg), full(lb)]),
        out_specs=row(x),
        out_shape=jax.ShapeDtypeStruct((t, D_MODEL), F32),
        compiler_params=_cparams("parallel"),
        name=name,
    )(*row_inputs, x, mod6, *consts, lg, lb)


def _ml_in_body(x_ref, mod_ref, w_ref, wt_ref, gb_ref, gbt_ref, qk_ref, v_ref, o_ref, gt_ref, vt_ref, gtt_ref):
    x = x_ref[...]
    h = (x * (1.0 + mod_ref[0, 1:2, :]) + mod_ref[0, 0:1, :]).astype(BF16)
    qk_ref[...] = _dot(h, w_ref[:, 0:2 * ML_W])
    v_ref[...] = _dot(h, w_ref[:, 2 * ML_W:3 * ML_W])
    o_ref[...] = _dot(h, w_ref[:, 3 * ML_W:4 * ML_W])
    gt_ref[...] = _dot(h, w_ref[:, 4 * ML_W:ML_IN_PAD]) + gb_ref[...]
    vt_ref[...] = _dot_nt(wt_ref[0:ML_W, :], h)
    gtt_ref[...] = _dot_nt(wt_ref[ML_W:, :], h) + gbt_ref[...]


def _ml_in(x, mod6, w_in, igate_b, fgate_b, seq):
    t = x.shape[0]
    tm = ROW_TILE
    per_b = seq // tm
    n_gate = ML_IN_PAD - 4 * ML_W
    w = jnp.zeros((D_MODEL, ML_IN_PAD), BF16).at[:, :ML_IN].set(w_in.astype(BF16))
    wt = w[:, 2 * ML_W:3 * ML_W].T
    wt = jnp.concatenate([wt, w[:, 4 * ML_W:].T], axis=0)
    gb = jnp.zeros((1, n_gate), F32).at[0, :2 * ML_HEADS].set(jnp.concatenate([igate_b, fgate_b]))
    row = lambda n: pl.BlockSpec((tm, n), lambda i: (i, 0))
    colblk = lambda n: pl.BlockSpec((n, tm), lambda i: (0, i))
    full = lambda a: pl.BlockSpec(a.shape, lambda i: (0,) * a.ndim)
    outs = [2 * ML_W, ML_W, ML_W, n_gate]
    return pl.pallas_call(
        _ml_in_body,
        grid=(t // tm,),
        in_specs=[row(D_MODEL), pl.BlockSpec((1, 6, D_MODEL), lambda i: (i // per_b, 0, 0)), full(w), full(wt),
                  full(gb), pl.BlockSpec((n_gate, 1), lambda i: (0, 0))],
        out_specs=[row(n) for n in outs] + [colblk(ML_W), colblk(n_gate)],
        out_shape=([jax.ShapeDtypeStruct((t, n), F32) for n in outs]
                   + [jax.ShapeDtypeStruct((ML_W, t), F32), jax.ShapeDtypeStruct((n_gate, t), F32)]),
        compiler_params=_cparams("parallel"),
        name="ml_in",
    )(x, mod6, w, wt, gb, gb.reshape(n_gate, 1))


def _mlstm_body(q_ref, k_ref, v_ref, og_ref, gt_ref, vt_ref, gtt_ref, wq_ref, wk_ref, bq_ref, bk_ref, ng_ref,
                y_ref, cq_ref, ck_ref, qs_ref, ks_ref, s_ref, n_ref, m_ref):
    tl = q_ref.shape[0]

    @pl.when(pl.program_id(1) == 0)
    def _():
        for ref in (s_ref, n_ref, m_ref, cq_ref, ck_ref):
            ref[...] = jnp.zeros_like(ref)

    def conv(x_ref, carry_ref, w_ref, b_ref, out_ref, scale):
        for h in range(ML_HEADS):
            cols = slice(ML_DH * h, ML_DH * (h + 1))
            x = x_ref[:, cols]
            xin = jnp.concatenate([carry_ref[:, cols], x], axis=0)
            acc = jnp.zeros((tl, ML_DH), F32) + b_ref[:, cols]
            for i in range(CONV_W):
                sh = CONV_W - 1 - i
                xs = xin if sh == 0 else pltpu.roll(xin, sh, 0)
                acc = acc + w_ref[i:i + 1, cols] * xs[8:8 + tl]
            carry_ref[:, cols] = x[tl - 8:tl]
            out_ref[:, cols] = _silu(acc) * scale

    conv(q_ref, cq_ref, wq_ref, bq_ref, qs_ref, 1.0)
    conv(k_ref, ck_ref, wk_ref, bk_ref, ks_ref, ML_DH ** -0.5)

    cs = ML_CHUNK
    row = lax.broadcasted_iota(jnp.int32, (cs, cs), 0)
    col = lax.broadcasted_iota(jnp.int32, (cs, cs), 1)
    causal = col <= row
    tri = causal.astype(BF16)
    tri_t = (row <= col).astype(BF16)

    for c in range(tl // cs):
        rows = slice(cs * c, cs * (c + 1))
        gt = gt_ref[rows, :]
        gt_t = gtt_ref[:, rows]
        lf_c = _log_sigmoid(gt)
        lf_r = _log_sigmoid(gt_t[ML_HEADS:2 * ML_HEADS])
        b_c_all = sum(_dot(tri, part) for part in _split3_bf16(lf_c))
        b_r_all = sum(_dot(part, tri_t) for part in _split3_bf16(lf_r))
        per_head = lambda f: jnp.stack([f(h) for h in range(ML_HEADS)])
        head_cols = lambda h: slice(ML_DH * h, ML_DH * (h + 1))
        q3 = per_head(lambda h: qs_ref[rows, head_cols(h)])
        k3 = per_head(lambda h: ks_ref[rows, head_cols(h)])
        vb = per_head(lambda h: v_ref[rows, head_cols(h)]).astype(BF16)
        vtb = per_head(lambda h: vt_ref[head_cols(h), rows]).astype(BF16)
        i_c = per_head(lambda h: gt[:, h:h + 1])
        b_c = per_head(lambda h: b_c_all[:, ML_HEADS + h:ML_HEADS + h + 1])
        i_r = per_head(lambda h: gt_t[h:h + 1, :])
        b_r = per_head(lambda h: b_r_all[h:h + 1, :])
        m = m_ref[:, :, 0:1]
        dmat = jnp.where(causal[None], b_c - b_r + i_r, -jnp.inf)
        inter_log = b_c + m
        m_t = jnp.maximum(inter_log, jnp.max(dmat, axis=2, keepdims=True))
        w_intra = jnp.exp(dmat - m_t)
        w_inter = jnp.exp(inter_log - m_t)
        qb = q3.astype(BF16)
        s = jnp.einsum('htd,hsd->hts', qb, k3.astype(BF16), preferred_element_type=F32) * w_intra
        state, nrm = s_ref[...], n_ref[...]
        num = (w_inter * jnp.einsum('htd,hvd->htv', qb, state.astype(BF16), preferred_element_type=F32)
               + jnp.einsum('hts,hsv->htv', s.astype(BF16), vb, preferred_element_type=F32))
        den = w_inter * jnp.sum(q3 * nrm, axis=2, keepdims=True) + jnp.sum(s, axis=2, keepdims=True)
        hc = num / jnp.maximum(jnp.abs(den), jnp.exp(-m_t))
        hc = hc * lax.rsqrt(jnp.mean(hc * hc, axis=2, keepdims=True) + RMS_EPS)
        for h in range(ML_HEADS):
            cols = head_cols(h)
            y_ref[rows, cols] = _sigmoid(og_ref[rows, cols]) * (hc[h] * ng_ref[:, cols])
        b_last = b_c[:, cs - 1:cs, :]
        gs_c = b_last - b_c + i_c
        gs_r = b_last - b_r + i_r
        m_new = jnp.maximum(b_last + m, jnp.max(gs_r, axis=2, keepdims=True))
        dec = jnp.exp(b_last + m - m_new)
        kw = k3 * jnp.exp(gs_c - m_new)
        s_ref[...] = dec * state + jnp.einsum('hvt,htd->hvd', vtb, kw.astype(BF16), preferred_element_type=F32)
        n_ref[...] = dec * nrm + jnp.sum(kw, axis=1, keepdims=True)
        m_ref[...] = jnp.broadcast_to(m_new, m_ref.shape)


def _mlstm(qk, v, og, gates, v_t, gates_t, conv_w, conv_b, norm_g, bsz, seq):
    tl = SEQ_TILE
    nl = seq // tl
    blk = lambda n, off: pl.BlockSpec((tl, n), lambda b, l: (b * nl + l, off))
    blk_t = lambda n: pl.BlockSpec((n, tl), lambda b, l: (0, b * nl + l))
    par = lambda rows, off: pl.BlockSpec((rows, ML_W), lambda b, l: (0, off))
    state = [pltpu.VMEM((ML_HEADS, ML_DH, ML_DH), F32), pltpu.VMEM((ML_HEADS, 1, ML_DH), F32),
             pltpu.VMEM((ML_HEADS, 1, ML_DH), F32)]
    return pl.pallas_call(
        _mlstm_body,
        grid=(bsz, nl),
        in_specs=[blk(ML_W, 0), blk(ML_W, 1), blk(ML_W, 0), blk(ML_W, 0), blk(gates.shape[1], 0),
                  blk_t(ML_W), blk_t(gates_t.shape[0]),
                  par(CONV_W, 0), par(CONV_W, 1), par(1, 0), par(1, 1), par(1, 0)],
        out_specs=blk(ML_W, 0),
        out_shape=jax.ShapeDtypeStruct((bsz * seq, ML_W), F32),
        scratch_shapes=[pltpu.VMEM((8, ML_W), F32), pltpu.VMEM((8, ML_W), F32),
                        pltpu.VMEM((tl, ML_W), F32), pltpu.VMEM((tl, ML_W), F32)] + state,
        compiler_params=_cparams("parallel", "arbitrary"),
        name="mlstm",
    )(qk, qk, v, og, gates, v_t, gates_t, conv_w, conv_w, conv_b.reshape(1, -1), conv_b.reshape(1, -1),
      norm_g.reshape(1, ML_W))


def _router_body(x_ref, mod_ref, rw_ref, rb_ref, info_ref, cnt_ref, carry_ref):
    tr = x_ref.shape[0]

    @pl.when(pl.program_id(0) == 0)
    def _():
        carry_ref[...] = jnp.zeros_like(carry_ref)

    h = x_ref[...] * (1.0 + mod_ref[0, 4:5, :]) + mod_ref[0, 3:4, :]
    logits = lax.dot_general(rw_ref[...], h, (((1,), (1,)), ((), ())), precision=lax.Precision.HIGHEST,
                             preferred_element_type=F32)
    aff = _sigmoid(logits)
    sel = aff + rb_ref[...]
    s = [sel[e:e + 1, :] for e in range(N_EXPERTS)]
    a = [aff[e:e + 1, :] for e in range(N_EXPERTS)]

    def top2_sum(v):
        best = v[0] + v[1]
        for lo_i, hi_i in zip(PAIR_LO[1:], PAIR_HI[1:]):
            best = jnp.maximum(best, v[lo_i] + v[hi_i])
        return best

    score = [top2_sum(s[EPG * g:EPG * (g + 1)]) for g in range(N_GROUPS)]
    g_idx = jnp.zeros((1, tr), jnp.int32)
    best = score[0]
    for g in range(1, N_GROUPS):
        upd = score[g] > best
        g_idx = jnp.where(upd, g, g_idx)
        best = jnp.where(upd, score[g], best)

    def pick_group(rows, j):
        out = rows[j]
        for g in range(1, N_GROUPS):
            out = jnp.where(g_idx == g, rows[EPG * g + j], out)
        return out

    sg = [pick_group(s, j) for j in range(EPG)]
    ag = [pick_group(a, j) for j in range(EPG)]
    first = jnp.zeros((1, tr), jnp.int32)
    best = sg[0]
    for j in range(1, EPG):
        upd = sg[j] > best
        first = jnp.where(upd, j, first)
        best = jnp.where(upd, sg[j], best)
    second = jnp.zeros((1, tr), jnp.int32)
    best2 = jnp.full((1, tr), -jnp.inf, F32)
    for j in range(EPG):
        upd = (first != j) & (sg[j] > best2)
        second = jnp.where(upd, j, second)
        best2 = jnp.where(upd, sg[j], best2)
    lo = jnp.minimum(first, second)
    hi = jnp.maximum(first, second)
    pair = jnp.where(lo == 0, hi - 1, jnp.where(lo == 1, hi + 1, N_PAIRS - 1))
    cls = g_idx * N_PAIRS + pair

    def pick_local(rows, idx):
        out = rows[0]
        for j in range(1, EPG):
            out = jnp.where(idx == j, rows[j], out)
        return out

    a_lo, a_hi = pick_local(ag, lo), pick_local(ag, hi)
    tot = a_lo + a_hi
    onehot = (lax.broadcasted_iota(jnp.int32, (32, tr), 0) == cls).astype(F32)
    before = (lax.broadcasted_iota(jnp.int32, (tr, tr), 0) < lax.broadcasted_iota(jnp.int32, (tr, tr), 1)).astype(BF16)
    cum = _dot(onehot.astype(BF16), before) + carry_ref[:, 0:1]
    rank = jnp.sum(onehot * cum, axis=0, keepdims=True)
    carry_ref[...] = carry_ref[...] + jnp.sum(onehot, axis=1, keepdims=True)
    cnt_ref[0] = carry_ref[...]
    zero = jnp.zeros((1, tr), F32)
    info_ref[...] = jnp.concatenate([cls.astype(F32), rank, a_lo / tot, a_hi / tot, zero, zero, zero, zero], axis=0)


def _router(x, mod6, router_w, router_bias, seq):
    t = x.shape[0]
    tr = ROUTE_TILE
    per_b = seq // tr
    return pl.pallas_call(
        _router_body,
        grid=(t // tr,),
        in_specs=[pl.BlockSpec((tr, D_MODEL), lambda i: (i, 0)),
                  pl.BlockSpec((1, 6, D_MODEL), lambda i: (i // per_b, 0, 0)),
                  pl.BlockSpec((N_EXPERTS, D_MODEL), lambda i: (0, 0)),
                  pl.BlockSpec((N_EXPERTS, 1), lambda i: (0, 0))],
        out_specs=[pl.BlockSpec((8, tr), lambda i: (0, i)), pl.BlockSpec((1, 32, 128), lambda i: (i, 0, 0))],
        out_shape=[jax.ShapeDtypeStruct((8, t), F32), jax.ShapeDtypeStruct((t // tr, 32, 128), F32)],
        scratch_shapes=[pltpu.VMEM((32, 128), F32)],
        compiler_params=_cparams("arbitrary"),
        name="router",
    )(x, mod6, router_w.T.astype(F32), router_bias.reshape(N_EXPERTS, 1).astype(F32))


def _from_row_tiles(ref):
    return jnp.concatenate([ref[:, j, :] for j in range(N_LANE_GROUPS)], axis=1)


def _store_row_tiles(ref, x):
    for j in range(N_LANE_GROUPS):
        ref[:, j, :] = x[:, 128 * j:128 * (j + 1)]


def _run_dmas(make_copy, length):
    off = jnp.int32(0)
    size = PERM_TILE
    while size >= 1:
        has = (length & size) != 0

        @pl.when(has)
        def _(off=off, size=size):
            make_copy(off, size).start()

        off = off + jnp.where(has, size, 0)
        size //= 2


def _dispatch_body(pos_ref, gstart_ref, lstart_ref, len_ref, x_ref, dst_in_ref, dst_ref, stage, ordered, sem):
    del dst_in_ref
    i = pl.program_id(0)
    tm = x_ref.shape[0]
    _store_row_tiles(stage, x_ref[...])

    def place(r, c):
        ordered[pos_ref[i * tm + r]] = stage[r]
        return c

    lax.fori_loop(0, tm, place, 0, unroll=8)
    for c in range(N_CLASSES):
        run = i * N_CLASSES + c
        g0, l0 = gstart_ref[run], lstart_ref[run]
        _run_dmas(lambda off, n: pltpu.make_async_copy(ordered.at[pl.ds(l0 + off, n)], dst_ref.at[pl.ds(g0 + off, n)],
                                                       sem.at[0]), len_ref[run])
    pltpu.make_async_copy(ordered, dst_ref.at[pl.ds(0, tm)], sem.at[0]).wait()


def _dispatch(pos, gstart, lstart, length, x, n_dst):
    t = x.shape[0]
    tm = PERM_TILE
    dst0 = jnp.zeros((n_dst, N_LANE_GROUPS, 128), F32)
    return pl.pallas_call(
        _dispatch_body,
        grid_spec=pltpu.PrefetchScalarGridSpec(
            num_scalar_prefetch=4,
            grid=(t // tm,),
            in_specs=[pl.BlockSpec((tm, D_MODEL), lambda i, *s: (i, 0)), pl.BlockSpec(memory_space=pl.ANY)],
            out_specs=pl.BlockSpec(memory_space=pl.ANY),
            scratch_shapes=[pltpu.VMEM((tm, N_LANE_GROUPS, 128), F32)] * 2 + [pltpu.SemaphoreType.DMA((1,))]),
        out_shape=jax.ShapeDtypeStruct(dst0.shape, F32),
        input_output_aliases={5: 0},
        compiler_params=_cparams("arbitrary"),
        name="moe_dispatch",
    )(pos, gstart, lstart, length, x, dst0)


def _combine_body(pos_ref, gstart_ref, lstart_ref, len_ref, src_ref, o_ref, ordered, stage, sem):
    i = pl.program_id(0)
    tm = o_ref.shape[0]
    for c in range(N_CLASSES):
        run = i * N_CLASSES + c
        g0, l0 = gstart_ref[run], lstart_ref[run]
        _run_dmas(lambda off, n: pltpu.make_async_copy(src_ref.at[pl.ds(g0 + off, n)], ordered.at[pl.ds(l0 + off, n)],
                                                       sem.at[0]), len_ref[run])
    pltpu.make_async_copy(src_ref.at[pl.ds(0, tm)], ordered, sem.at[0]).wait()

    def take(r, c):
        stage[r] = ordered[pos_ref[i * tm + r]]
        return c

    lax.fori_loop(0, tm, take, 0, unroll=8)
    o_ref[...] = _from_row_tiles(stage)


def _combine(pos, gstart, lstart, length, src):
    t = pos.shape[0]
    tm = PERM_TILE
    return pl.pallas_call(
        _combine_body,
        grid_spec=pltpu.PrefetchScalarGridSpec(
            num_scalar_prefetch=4,
            grid=(t // tm,),
            in_specs=[pl.BlockSpec(memory_space=pl.ANY)],
            out_specs=pl.BlockSpec((tm, D_MODEL), lambda i, *s: (i, 0)),
            scratch_shapes=[pltpu.VMEM((tm, N_LANE_GROUPS, 128), F32)] * 2 + [pltpu.SemaphoreType.DMA((1,))]),
        out_shape=jax.ShapeDtypeStruct((t, D_MODEL), F32),
        compiler_params=_cparams("arbitrary"),
        name="moe_combine",
    )(pos, gstart, lstart, length, src)


def _moe_body(e_lo_ref, e_hi_ref, blk_ref, new_ref, used_ref, x_ref, meta_ref, mod_ref,
              wg_lo32, wu_lo32, wd_lo32, wg_hi32, wu_hi32, wd_hi32, lg_ref, lb_ref, o_ref,
              wg_lo, wu_lo, wd_lo, wg_hi, wu_hi, wd_hi):
    del e_lo_ref, e_hi_ref, blk_ref
    i = pl.program_id(0)
    used = used_ref[0]

    @pl.when(new_ref[i] == 1)
    def _():
        for src, dst in ((wg_lo32, wg_lo), (wu_lo32, wu_lo), (wd_lo32, wd_lo),
                         (wg_hi32, wg_hi), (wu_hi32, wu_hi), (wd_hi32, wd_hi)):
            dst[...] = src[...].astype(BF16)

    @pl.when(i >= used)
    def _():
        o_ref[...] = jnp.zeros_like(o_ref)

    @pl.when(i < used)
    def _():
        x = _from_row_tiles(x_ref)
        meta = meta_ref[...]
        bid, g_lo, g_hi = meta[:, 0:1], meta[:, 1:2], meta[:, 2:3]
        lane = lax.broadcasted_iota(jnp.int32, (x.shape[0], 128), 1).astype(F32)
        pick = (lane == bid).astype(BF16)
        pick = jnp.concatenate([pick, pick, pick], axis=1)

        def per_row(r):
            return _dot(pick, mod_ref[r - 3])

        h = (x * (1.0 + per_row(4)) + per_row(3)).astype(BF16)

        def expert(wg, wu, wd):
            he = _silu(_dot(h, wg[...])) * _dot(h, wu[...])
            return _dot(he.astype(BF16), wd[...])

        y = g_lo * expert(wg_lo, wu_lo, wd_lo) + g_hi * expert(wg_hi, wu_hi, wd_hi)
        _store_row_tiles(o_ref, _layer_norm(ALPHA * x + (1.0 + per_row(5)) * y, lg_ref[...], lb_ref[...]))


def _moe_tiles(e_lo, e_hi, blk, new, used, xs, meta, mod6, layer, w_gate, w_up, w_down, ln_g, ln_b):
    tp = xs.shape[0]
    tm = ROW_TILE
    sel = jnp.zeros((3, 128, D_MODEL), F32).at[:, :mod6.shape[0]].set(jnp.transpose(mod6[:, 3:6, :], (1, 0, 2)))
    tab = jnp.concatenate(_split3_bf16(sel), axis=1)
    tiles = lambda index: pl.BlockSpec((tm, N_LANE_GROUPS, 128), index)
    wspec = lambda shape, which: pl.BlockSpec((None, None) + shape, lambda i, *s: (layer, s[which][i], 0, 0))
    full = lambda a: pl.BlockSpec(a.shape, lambda i, *s: (0,) * a.ndim)
    lg, lb = ln_g.reshape(1, D_MODEL), ln_b.reshape(1, D_MODEL)
    gu, dn = (D_MODEL, D_EXPERT), (D_EXPERT, D_MODEL)
    return pl.pallas_call(
        _moe_body,
        grid_spec=pltpu.PrefetchScalarGridSpec(
            num_scalar_prefetch=5,
            grid=(tp // tm,),
            in_specs=[tiles(lambda i, *s: (s[2][i], 0, 0)), pl.BlockSpec((tm, 8), lambda i, *s: (s[2][i], 0)), full(tab),
                      wspec(gu, 0), wspec(gu, 0), wspec(dn, 0), wspec(gu, 1), wspec(gu, 1), wspec(dn, 1),
                      full(lg), full(lb)],
            out_specs=tiles(lambda i, *s: (i, 0, 0)),
            scratch_shapes=[pltpu.VMEM(gu, BF16), pltpu.VMEM(gu, BF16), pltpu.VMEM(dn, BF16)] * 2),
        out_shape=jax.ShapeDtypeStruct((tp, N_LANE_GROUPS, 128), F32),
        compiler_params=_cparams("arbitrary"),
        name="moe_tiles",
    )(e_lo, e_hi, blk, new, used, xs, meta, tab, w_gate, w_up, w_down, w_gate, w_up, w_down, lg, lb)


def _moe_layer(x, mod6, router_w, router_bias, layer, w_gate, w_up, w_down, ln_g, ln_b, bsz, seq):
    t = x.shape[0]
    tm = ROW_TILE
    n_tiles = t // tm + N_CLASSES
    tp = n_tiles * tm
    info, seen = _router(x, mod6, router_w, router_bias, seq)
    cls = info[0].astype(jnp.int32)
    rank = info[1].astype(jnp.int32)
    seen = seen[:, :N_CLASSES, 0].astype(jnp.int32)
    cnt = seen[-1]
    tiles_c = (cnt + tm - 1) // tm
    tile_end = jnp.cumsum(tiles_c)
    offs = (tile_end - tiles_c) * tm
    onehot = cls[:, None] == jnp.arange(N_CLASSES)[None, :]
    dest = rank + jnp.sum(jnp.where(onehot, offs[None, :], 0), axis=1)
    used = tile_end[-1]
    tile_id = jnp.arange(n_tiles)
    blk = jnp.maximum(jnp.minimum(tile_id, used - 1), 0)
    tile_cls = jnp.minimum(jnp.sum(blk[:, None] >= tile_end[None, :], axis=1), N_CLASSES - 1)
    grp, pair = tile_cls // N_PAIRS, tile_cls % N_PAIRS
    e_lo = (grp * EPG + jnp.asarray(PAIR_LO, jnp.int32)[pair]).astype(jnp.int32)
    e_hi = (grp * EPG + jnp.asarray(PAIR_HI, jnp.int32)[pair]).astype(jnp.int32)
    new = jnp.concatenate([jnp.ones((1,), jnp.int32), (tile_cls[1:] != tile_cls[:-1]).astype(jnp.int32)])
    bid = (jnp.arange(t) // seq).astype(F32)
    meta_nat = jnp.stack([bid, info[2], info[3]] + [jnp.zeros((t,), F32)] * 5, axis=1)
    meta = jnp.zeros((tp, 8), F32).at[dest].set(meta_nat)
    pt = PERM_TILE
    before = jnp.concatenate([jnp.zeros((1, N_CLASSES), jnp.int32), seen[:-1]], axis=0)
    length = seen - before
    lstart = jnp.cumsum(length, axis=1) - length
    gstart = offs[None, :] + before
    shift = jnp.broadcast_to((lstart - before)[:, None, :], (t // pt, pt, N_CLASSES)).reshape(t, N_CLASSES)
    pos = rank + jnp.sum(jnp.where(onehot, shift, 0), axis=1)
    runs = (pos.astype(jnp.int32), gstart.reshape(-1).astype(jnp.int32), lstart.reshape(-1).astype(jnp.int32),
            length.reshape(-1).astype(jnp.int32))
    xs = _dispatch(*runs, x, tp)
    ys = _moe_tiles(e_lo, e_hi, blk.astype(jnp.int32), new, used.reshape(1).astype(jnp.int32), xs, meta,
                    mod6, layer, w_gate, w_up, w_down, ln_g, ln_b)
    return _combine(*runs, ys)


def kernel(x, c, router_w, router_bias, ada_w, ada_b, ln1_g, ln1_b, ln2_g, ln2_b, moe_w_gate, moe_w_up, moe_w_down,
           ab_w_in, s5_lam_re, s5_lam_im, s5_log_dt, s5_b_re, s5_b_im, s5_c_re, s5_c_im, s5_d, s5_glu_w, s5_glu_b,
           gla_gate_w, gla_gate_b, gla_norm_g, ab_w_out, ml_w_in, ml_conv_w, ml_conv_b, ml_igate_b, ml_fgate_b,
           ml_norm_g, ml_w_out):
    bsz, seq, d = x.shape
    t = bsz * seq
    mod = _ada_mod(c, ada_w, ada_b)
    xt = x.reshape(t, d)
    for layer in range(DEPTH):
        j = layer // 2
        mod6 = mod[layer]
        if layer % 2 == 0:
            u, q, k, v, r, la = _ab_in(xt, mod6, ab_w_in[j], gla_gate_w[j], gla_gate_b[j], seq)
            tables = _s5_prep(s5_lam_re[j], s5_lam_im[j], s5_log_dt[j], s5_b_re[j], s5_b_im[j], s5_c_re[j], s5_c_im[j])
            ys = _s5_conv(u, tables, bsz, seq)
            yb = _gla(q, k, la, v, r, gla_norm_g[j], bsz, seq)
            w_out = ab_w_out[j].astype(BF16)
            consts = [s5_d[j].reshape(1, S5_W), s5_glu_w[j].astype(BF16), s5_glu_b[j].reshape(1, S5_W),
                      w_out[:S5_W], w_out[S5_W:]]
            xt = _out_call(_ab_out_body, "ab_out", [ys, u, yb], xt, mod6, consts, ln1_g[layer], ln1_b[layer], seq)
        else:
            qk, v, og, gates, v_t, gates_t = _ml_in(xt, mod6, ml_w_in[j], ml_igate_b[j], ml_fgate_b[j], seq)
            y = _mlstm(qk, v, og, gates, v_t, gates_t, ml_conv_w[j], ml_conv_b[j], ml_norm_g[j], bsz, seq)
            xt = _out_call(_ml_out_body, "ml_out", [y], xt, mod6, [ml_w_out[j].astype(BF16)],
                           ln1_g[layer], ln1_b[layer], seq)
        xt = _moe_layer(xt, mod6, router_w, router_bias, layer, moe_w_gate, moe_w_up, moe_w_down,
                        ln2_g[layer], ln2_b[layer], bsz, seq)
    return xt.reshape(bsz, seq, d)
```

```python
import functools
import math

import jax
import jax.numpy as jnp
from jax import lax
from jax.experimental import pallas as pl
from jax.experimental.pallas import tpu as pltpu

F32 = jnp.float32
BF16 = jnp.bfloat16

D_MODEL = 1024
DEPTH = 4
S5_W = 512
S5_GROUP = 16
S5_GROUPS = 32
S5_STATE = 64
S5_CHUNK = 16
S5_SCAN_ROWS = 8
GLA_HEADS = 4
GLA_DV = 128
GLA_DK = 64
GLA_QK = 256
GLA_V = 512
GLA_RANK = 16
GLA_GATE_NORM = 16.0
GLA_SUB = 16
AB_IN = S5_W + 2 * GLA_QK + 2 * GLA_V + GLA_RANK
AB_IN_PAD = 2176
ML_HEADS = 8
ML_W = 1024
ML_DH = 128
ML_IN = 4 * ML_W + 2 * ML_HEADS
ML_IN_PAD = 4224
CONV_W = 4
CHUNK = 64
ML_CHUNK = 128
N_EXPERTS = 16
N_GROUPS = 4
EPG = 4
D_EXPERT = 256
N_PAIRS = 6
N_CLASSES = N_GROUPS * N_PAIRS
PAIR_LO = (0, 0, 0, 1, 1, 2)
PAIR_HI = (1, 2, 3, 2, 3, 3)
ALPHA = (2.0 * DEPTH) ** 0.25
LN_EPS = 1e-5
RMS_EPS = 1e-6

V7X_VMEM_BYTES = 64 * 1024 * 1024
VMEM_LIMIT = (V7X_VMEM_BYTES * 3) // 4
ROW_TILE = 256
SEQ_TILE = 256
ROUTE_TILE = 512
PERM_TILE = 512
assert ROUTE_TILE == PERM_TILE
N_LANE_GROUPS = D_MODEL // 128


def _cparams(*sem):
    return pltpu.CompilerParams(dimension_semantics=sem, vmem_limit_bytes=VMEM_LIMIT)


def _sigmoid(x):
    return 1.0 / (1.0 + jnp.exp(-x))


def _silu(x):
    return x * _sigmoid(x)


def _log_sigmoid(z):
    return jnp.minimum(z, 0.0) - jnp.log1p(jnp.exp(-jnp.abs(z)))


def _gelu_tanh(x):
    return 0.5 * x * (1.0 + jnp.tanh(math.sqrt(2.0 / math.pi) * (x + 0.044715 * (x * x * x))))


def _layer_norm(x, g, b):
    mu = jnp.mean(x, axis=-1, keepdims=True)
    xc = x - mu
    var = jnp.mean(xc * xc, axis=-1, keepdims=True)
    return xc * lax.rsqrt(var + LN_EPS) * g + b


def _split3_bf16(x):
    hi = x.astype(BF16)
    r = x - hi.astype(F32)
    mid = r.astype(BF16)
    lo = (r - mid.astype(F32)).astype(BF16)
    return hi, mid, lo


def _dot(a, b):
    return jnp.dot(a, b, preferred_element_type=F32)


def _dot_nt(a, b):
    return lax.dot_general(a, b, (((1,), (1,)), ((), ())), preferred_element_type=F32)


def _dot_tn(a, b):
    return lax.dot_general(a, b, (((0,), (0,)), ((), ())), preferred_element_type=F32)


def _ada_body(c_ref, w_ref, b_ref, o_ref):
    c = c_ref[...]
    o_ref[0] = _dot(_silu(c), w_ref[0]) + b_ref[0]


def _ada_mod(c, ada_w, ada_b):
    bsz = c.shape[0]
    tn = 1536
    cp = jnp.zeros((8, D_MODEL), F32).at[:bsz].set(c)
    out = pl.pallas_call(
        _ada_body,
        grid=(DEPTH, 6 * D_MODEL // tn),
        in_specs=[pl.BlockSpec((8, D_MODEL), lambda l, j: (0, 0)),
                  pl.BlockSpec((1, D_MODEL, tn), lambda l, j: (l, 0, j)),
                  pl.BlockSpec((1, 1, tn), lambda l, j: (l, 0, j))],
        out_specs=pl.BlockSpec((1, 8, tn), lambda l, j: (l, 0, j)),
        out_shape=jax.ShapeDtypeStruct((DEPTH, 8, 6 * D_MODEL), F32),
        compiler_params=_cparams("parallel", "parallel"),
        name="ada_mod",
    )(cp, ada_w, ada_b.reshape(DEPTH, 1, 6 * D_MODEL))
    return out[:, :bsz].reshape(DEPTH, bsz, 6, D_MODEL)


def _ab_in_body(x_ref, mod_ref, w_ref, gw_ref, gb_ref, u_ref, q_ref, k_ref, v_ref, r_ref, la_ref):
    x = x_ref[...]
    h = (x * (1.0 + mod_ref[0, 1:2, :]) + mod_ref[0, 0:1, :]).astype(BF16)

    def seg(a, b):
        return _dot(h, w_ref[:, a:b])

    u_ref[...] = seg(0, 512)
    q_ref[...] = seg(512, 768)
    k_ref[...] = seg(768, 1024)
    v_ref[...] = seg(1024, 1536)
    r_ref[...] = seg(1536, 2048)
    a_lr = seg(2048, AB_IN_PAD)
    z = _dot(a_lr.astype(BF16), gw_ref[...]) + gb_ref[...]
    la_ref[...] = _log_sigmoid(z) * (1.0 / GLA_GATE_NORM)


def _ab_in(x, mod6, w_in, gate_w, gate_b, seq):
    t = x.shape[0]
    tm = ROW_TILE
    per_b = seq // tm
    w = jnp.zeros((D_MODEL, AB_IN_PAD), BF16).at[:, :AB_IN].set(w_in.astype(BF16))
    gw = jnp.zeros((AB_IN_PAD - 2048, GLA_QK), BF16).at[:GLA_RANK].set(gate_w.astype(BF16))
    row = lambda n: pl.BlockSpec((tm, n), lambda i: (i, 0))
    full = lambda a: pl.BlockSpec(a.shape, lambda i: (0,) * a.ndim)
    outs = [S5_W, GLA_QK, GLA_QK, GLA_V, GLA_V, GLA_QK]
    return pl.pallas_call(
        _ab_in_body,
        grid=(t // tm,),
        in_specs=[row(D_MODEL),
                  pl.BlockSpec((1, 6, D_MODEL), lambda i: (i // per_b, 0, 0)),
                  full(w), full(gw), pl.BlockSpec((1, GLA_QK), lambda i: (0, 0))],
        out_specs=[row(n) for n in outs],
        out_shape=[jax.ShapeDtypeStruct((t, n), F32) for n in outs],
        compiler_params=_cparams("parallel"),
        name="ab_in",
    )(x, mod6, w, gw, gate_b.reshape(1, GLA_QK))


def _s5_prep(lam_re, lam_im, log_dt, b_re, b_im, c_re, c_im):
    hp = lax.Precision.HIGHEST
    n = S5_CHUNK
    lre, lim = lam_re.astype(F32), lam_im.astype(F32)
    dt = jnp.exp(log_dt.astype(F32))[:, None]
    d = jnp.arange(n + 1, dtype=F32)[:, None, None]
    mag = jnp.exp(lre * dt * d)
    ang = lim * dt * d
    pw_re, pw_im = mag * jnp.cos(ang), mag * jnp.sin(ang)
    den = lre * lre + lim * lim
    nre, nim = pw_re[1] - 1.0, pw_im[1]
    coef_re = (nre * lre + nim * lim) / den
    coef_im = (nim * lre - nre * lim) / den
    bre, bim = b_re.astype(F32), b_im.astype(F32)
    bb_re = coef_re[..., None] * bre - coef_im[..., None] * bim
    bb_im = coef_re[..., None] * bim + coef_im[..., None] * bre
    cre, cim = c_re.astype(F32), c_im.astype(F32)
    cp_re = cre[None] * pw_re[:, :, None, :] - cim[None] * pw_im[:, :, None, :]
    cp_im = cre[None] * pw_im[:, :, None, :] + cim[None] * pw_re[:, :, None, :]
    kern = (jnp.einsum('dgcp,gpe->dgce', cp_re, bb_re, precision=hp)
            - jnp.einsum('dgcp,gpe->dgce', cp_im, bb_im, precision=hp))
    lag_is = (jnp.arange(n)[None, None, :] - jnp.arange(n)[None, :, None]
              == jnp.arange(n + 1)[:, None, None]).astype(F32)
    m_intra = jnp.einsum('djt,dgce->gjetc', lag_is, kern, precision=hp)
    m_intra = m_intra.reshape(S5_GROUPS, n * S5_GROUP, n * S5_GROUP)
    pws_re, pws_im = pw_re[n - 1 - jnp.arange(n)], pw_im[n - 1 - jnp.arange(n)]
    ws_re = pws_re[..., None] * bb_re[None] - pws_im[..., None] * bb_im[None]
    ws_im = pws_re[..., None] * bb_im[None] + pws_im[..., None] * bb_re[None]
    w_state = jnp.transpose(jnp.concatenate([ws_re, ws_im], axis=2), (1, 0, 3, 2))
    w_state = w_state.reshape(S5_GROUPS, n * S5_GROUP, 2 * S5_STATE)
    wi_re = jnp.transpose(cp_re[1:], (1, 3, 0, 2))
    wi_im = -jnp.transpose(cp_im[1:], (1, 3, 0, 2))
    w_inter = jnp.concatenate([wi_re, wi_im], axis=1).reshape(S5_GROUPS, 2 * S5_STATE, n * S5_GROUP)
    r = (n * jnp.arange(S5_SCAN_ROWS + 1, dtype=F32))[None, :, None]
    cmag, cang = jnp.exp(lre[:, None, :] * dt[:, None, :] * r), lim[:, None, :] * dt[:, None, :] * r
    a_pows = jnp.concatenate([cmag * jnp.cos(cang), cmag * jnp.sin(cang)], axis=1)
    return m_intra.astype(BF16), w_state.astype(BF16), w_inter.astype(BF16), a_pows


def _s5_body(u_ref, m_ref, ws_ref, wi_ref, a_ref, y_ref, ere, eim, sre, sim, *, bsz):
    u = u_ref[0]
    e = _dot(u, ws_ref[0])
    ere[...] = e[:, :S5_STATE]
    eim[...] = e[:, S5_STATE:]
    n_pow = S5_SCAN_ROWS + 1
    pr, pi = a_ref[0, 0:n_pow, :], a_ref[0, n_pow:2 * n_pow, :]
    tab_r, tab_i = pr[0:S5_SCAN_ROWS], pi[0:S5_SCAN_ROWS]
    rowi = lax.broadcasted_iota(jnp.int32, (S5_SCAN_ROWS, S5_STATE), 0)
    nk = ere.shape[0] // bsz

    def cmul(a_r, a_i, x_r, x_i):
        return a_r * x_r - a_i * x_i, a_r * x_i + a_i * x_r

    def shifted(x, s):
        return jnp.where(rowi >= s, pltpu.roll(x, s, 0), 0.0)

    def tile(m, carry):
        out = []
        for b in range(bsz):
            c_r, c_i = carry[2 * b], carry[2 * b + 1]
            r0 = pl.multiple_of(b * nk + m * S5_SCAN_ROWS, S5_SCAN_ROWS)
            x_r, x_i = ere[pl.ds(r0, S5_SCAN_ROWS), :], eim[pl.ds(r0, S5_SCAN_ROWS), :]
            for s in (1, 2, 4):
                d_r, d_i = cmul(pr[s:s + 1], pi[s:s + 1], shifted(x_r, s), shifted(x_i, s))
                x_r, x_i = x_r + d_r, x_i + d_i
            t_r, t_i = cmul(tab_r, tab_i, c_r, c_i)
            sre[pl.ds(r0, S5_SCAN_ROWS), :] = shifted(x_r, 1) + t_r
            sim[pl.ds(r0, S5_SCAN_ROWS), :] = shifted(x_i, 1) + t_i
            n_r, n_i = cmul(pr[S5_SCAN_ROWS:n_pow], pi[S5_SCAN_ROWS:n_pow], c_r, c_i)
            last = slice(S5_SCAN_ROWS - 1, S5_SCAN_ROWS)
            out += [n_r + jnp.broadcast_to(x_r[last], c_r.shape), n_i + jnp.broadcast_to(x_i[last], c_i.shape)]
        return tuple(out)

    z = jnp.zeros((S5_SCAN_ROWS, S5_STATE), F32)
    lax.fori_loop(0, nk // S5_SCAN_ROWS, tile, (z,) * (2 * bsz))
    y = _dot(u, m_ref[0])
    y = y + _dot(sre[...].astype(BF16), wi_ref[0, :S5_STATE, :])
    y = y + _dot(sim[...].astype(BF16), wi_ref[0, S5_STATE:, :])
    y_ref[0] = y


def _s5_pack_body(u_ref, o_ref):
    nk = u_ref.shape[0] // S5_CHUNK
    by_step = [u_ref[pl.ds(t, nk, stride=S5_CHUNK), :] for t in range(S5_CHUNK)]
    for g in range(128 // S5_GROUP):
        cols = slice(S5_GROUP * g, S5_GROUP * (g + 1))
        o_ref[g] = jnp.concatenate([x[:, cols] for x in by_step], axis=1).astype(o_ref.dtype)


def _s5_unpack_body(y_ref, o_ref):
    nk = y_ref.shape[1]
    groups = [y_ref[g] for g in range(128 // S5_GROUP)]
    for t in range(S5_CHUNK):
        cols = slice(S5_GROUP * t, S5_GROUP * (t + 1))
        o_ref[pl.ds(t, nk, stride=S5_CHUNK), :] = jnp.concatenate([y[:, cols] for y in groups], axis=1)


def _s5_conv(u, tables, bsz, seq):
    m_intra, w_state, w_inter, a_pows = tables
    t = bsz * seq
    rows = t // S5_CHUNK
    lanes = S5_CHUNK * S5_GROUP
    tm = PERM_TILE
    gpb = 128 // S5_GROUP
    tok = pl.BlockSpec((tm, 128), lambda i, j: (i, j))
    chunked = pl.BlockSpec((gpb, tm // S5_CHUNK, lanes), lambda i, j: (j, i, 0))
    uf = pl.pallas_call(
        _s5_pack_body,
        grid=(t // tm, S5_W // 128),
        in_specs=[tok],
        out_specs=chunked,
        out_shape=jax.ShapeDtypeStruct((S5_GROUPS, rows, lanes), BF16),
        compiler_params=_cparams("parallel", "parallel"),
        name="s5_pack",
    )(u)
    grp = lambda shape: pl.BlockSpec((1,) + shape, lambda g: (g, 0, 0))
    yf = pl.pallas_call(
        functools.partial(_s5_body, bsz=bsz),
        grid=(S5_GROUPS,),
        in_specs=[grp((rows, lanes)), grp((lanes, lanes)), grp((lanes, 2 * S5_STATE)),
                  grp((2 * S5_STATE, lanes)), grp(a_pows.shape[1:])],
        out_specs=grp((rows, lanes)),
        out_shape=jax.ShapeDtypeStruct((S5_GROUPS, rows, lanes), F32),
        scratch_shapes=[pltpu.VMEM((rows, S5_STATE), F32)] * 4,
        compiler_params=_cparams("parallel"),
        name="s5_conv",
    )(uf, m_intra, w_state, w_inter, a_pows)
    return pl.pallas_call(
        _s5_unpack_body,
        grid=(t // tm, S5_W // 128),
        in_specs=[chunked],
        out_specs=tok,
        out_shape=jax.ShapeDtypeStruct((t, S5_W), F32),
        compiler_params=_cparams("parallel", "parallel"),
        name="s5_unpack",
    )(yf)


def _gla_body(q_ref, k_ref, g_ref, v_ref, r_ref, ng_ref, y_ref, s_ref, bc_ref, o_ref):
    tl = q_ref.shape[0]
    scale = GLA_DK ** -0.5

    @pl.when(pl.program_id(1) == 0)
    def _():
        s_ref[...] = jnp.zeros_like(s_ref)

    row = lax.broadcasted_iota(jnp.int32, (CHUNK, CHUNK), 0)
    col = lax.broadcasted_iota(jnp.int32, (CHUNK, CHUNK), 1)
    tri = col <= row
    tref = col < (row // GLA_SUB) * GLA_SUB
    cum_mat = jnp.concatenate([tri, tref], axis=0).astype(BF16)
    lane_head = lax.broadcasted_iota(jnp.int32, (GLA_SUB, GLA_QK), 1) // GLA_DK
    bd_mask = (lax.broadcasted_iota(jnp.int32, (GLA_QK, GLA_V), 0) // GLA_DK
               == lax.broadcasted_iota(jnp.int32, (GLA_QK, GLA_V), 1) // GLA_DV)
    ones_bd = bd_mask.astype(BF16)
    ones_v = jnp.ones((CHUNK, GLA_V), BF16)

    for c in range(tl // CHUNK):
        r0 = c * CHUNK
        qc = q_ref[pl.ds(r0, CHUNK), :] * scale
        kc = k_ref[pl.ds(r0, CHUNK), :]
        g_parts = _split3_bf16(g_ref[pl.ds(r0, CHUNK), :])
        vb = v_ref[pl.ds(r0, CHUNK), :].astype(BF16)
        br = sum(_dot(cum_mat, part) for part in g_parts)
        bc, ref = br[:CHUNK], br[CHUNK:]
        bc_ref[pl.ds(r0, CHUNK), :] = bc
        qt = qc * jnp.exp(bc - ref)
        state = s_ref[...]
        o = _dot((qt * jnp.exp(ref)).astype(BF16), state.astype(BF16))
        blocks = [o[:GLA_SUB]]
        for i in range(1, CHUNK // GLA_SUB):
            ri = bc[GLA_SUB * i - 1:GLA_SUB * i, :]
            kt = (kc * jnp.exp(jnp.minimum(ri - bc, 0.0))).astype(BF16)
            qi = qt[GLA_SUB * i:GLA_SUB * (i + 1)]
            lhs = jnp.concatenate([jnp.where(lane_head == h, qi, 0.0) for h in range(GLA_HEADS)], axis=0)
            att = _dot_nt(lhs.astype(BF16), kt)
            att = jnp.where(col < GLA_SUB * i, att, 0.0)
            ov = _dot(att.astype(BF16), vb)
            oi = jnp.concatenate([ov[GLA_SUB * h:GLA_SUB * (h + 1), GLA_DV * h:GLA_DV * (h + 1)]
                                  for h in range(GLA_HEADS)], axis=1)
            blocks.append(o[GLA_SUB * i:GLA_SUB * (i + 1)] + oi)
        o_ref[pl.ds(r0, CHUNK), :] = jnp.concatenate(blocks, axis=0)
        bl = bc[CHUNK - 1:CHUNK, :]
        kh = (kc * jnp.exp(bl - bc)).astype(BF16)
        upd = _dot_tn(kh, vb)
        dcol = sum(_dot_tn(part, ones_v) for part in g_parts)
        s_ref[...] = jnp.exp(dcol) * state + jnp.where(bd_mask, upd, 0.0)

    q = q_ref[...] * scale
    k = k_ref[...]
    v = v_ref[...]
    bc = bc_ref[...]
    rmod = lax.broadcasted_iota(jnp.int32, (tl, 1), 0) % GLA_SUB
    od = jnp.zeros((tl, GLA_V), F32)
    for d in range(GLA_SUB):
        kd = k if d == 0 else pltpu.roll(k, d, 0)
        bd = bc if d == 0 else pltpu.roll(bc, d, 0)
        vd = v if d == 0 else pltpu.roll(v, d, 0)
        valid = rmod >= d
        e = jnp.exp(jnp.where(valid, bc - bd, 0.0))
        p = jnp.where(valid, q * kd * e, 0.0).astype(BF16)
        od = od + _dot(p, ones_bd) * vd
    o = o_ref[...] + od
    ng = ng_ref[...]
    outs = []
    for h in range(GLA_HEADS):
        oh = o[:, GLA_DV * h:GLA_DV * (h + 1)]
        outs.append(oh * lax.rsqrt(jnp.mean(oh * oh, axis=-1, keepdims=True) + RMS_EPS))
    y_ref[...] = jnp.concatenate(outs, axis=1) * ng * _silu(r_ref[...])


def _gla(q, k, la, v, r, norm_g, bsz, seq):
    tl = SEQ_TILE
    nl = seq // tl
    blk = lambda n: pl.BlockSpec((tl, n), lambda b, l: (b * nl + l, 0))
    return pl.pallas_call(
        _gla_body,
        grid=(bsz, nl),
        in_specs=[blk(GLA_QK), blk(GLA_QK), blk(GLA_QK), blk(GLA_V), blk(GLA_V),
                  pl.BlockSpec((1, GLA_V), lambda b, l: (0, 0))],
        out_specs=blk(GLA_V),
        out_shape=jax.ShapeDtypeStruct((bsz * seq, GLA_V), F32),
        scratch_shapes=[pltpu.VMEM((GLA_QK, GLA_V), F32), pltpu.VMEM((tl, GLA_QK), F32),
                        pltpu.VMEM((tl, GLA_V), F32)],
        compiler_params=_cparams("parallel", "arbitrary"),
        name="gla",
    )(q, k, la, v, r, norm_g.reshape(1, GLA_V))


def _residual_ln(x, y, mod_ref, gate_row, lg_ref, lb_ref):
    return _layer_norm(ALPHA * x + (1.0 + mod_ref[0, gate_row:gate_row + 1, :]) * y, lg_ref[...], lb_ref[...])


def _ab_out_body(ys_ref, u_ref, yb_ref, x_ref, mod_ref, d_ref, gw_ref, gb_ref, wa_ref, wb_ref, lg_ref, lb_ref, o_ref):
    z = _gelu_tanh(ys_ref[...] + d_ref[...] * u_ref[...])
    ya = z * _sigmoid(_dot(z.astype(BF16), gw_ref[...]) + gb_ref[...])
    y = _dot(ya.astype(BF16), wa_ref[...]) + _dot(yb_ref[...].astype(BF16), wb_ref[...])
    o_ref[...] = _residual_ln(x_ref[...], y, mod_ref, 2, lg_ref, lb_ref)


def _ml_out_body(y_ref, x_ref, mod_ref, w_ref, lg_ref, lb_ref, o_ref):
    y = _dot(y_ref[...].astype(BF16), w_ref[...])
    o_ref[...] = _residual_ln(x_ref[...], y, mod_ref, 2, lg_ref, lb_ref)


def _out_call(body, name, row_inputs, x, mod6, consts, ln_g, ln_b, seq):
    t = x.shape[0]
    tm = ROW_TILE
    per_b = seq // tm
    row = lambda a: pl.BlockSpec((tm, a.shape[1]), lambda i: (i, 0))
    full = lambda a: pl.BlockSpec(a.shape, lambda i: (0,) * a.ndim)
    lg, lb = ln_g.reshape(1, D_MODEL), ln_b.reshape(1, D_MODEL)
    return pl.pallas_call(
        body,
        grid=(t // tm,),
        in_specs=([row(a) for a in row_inputs] + [row(x), pl.BlockSpec((1, 6, D_MODEL), lambda i: (i // per_b, 0, 0))]
                  + [full(a) for a in consts] + [full(lg), full(lb)]),
        out_specs=row(x),
        out_shape=jax.ShapeDtypeStruct((t, D_MODEL), F32),
        compiler_params=_cparams("parallel"),
        name=name,
    )(*row_inputs, x, mod6, *consts, lg, lb)


def _ml_in_body(x_ref, mod_ref, w_ref, wt_ref, gb_ref, gbt_ref, qk_ref, v_ref, o_ref, gt_ref, vt_ref, gtt_ref):
    x = x_ref[...]
    h = (x * (1.0 + mod_ref[0, 1:2, :]) + mod_ref[0, 0:1, :]).astype(BF16)
    qk_ref[...] = _dot(h, w_ref[:, 0:2 * ML_W])
    v_ref[...] = _dot(h, w_ref[:, 2 * ML_W:3 * ML_W])
    o_ref[...] = _dot(h, w_ref[:, 3 * ML_W:4 * ML_W])
    gt_ref[...] = _dot(h, w_ref[:, 4 * ML_W:ML_IN_PAD]) + gb_ref[...]
    vt_ref[...] = _dot_nt(wt_ref[0:ML_W, :], h)
    gtt_ref[...] = _dot_nt(wt_ref[ML_W:, :], h) + gbt_ref[...]


def _ml_in(x, mod6, w_in, igate_b, fgate_b, seq):
    t = x.shape[0]
    tm = ROW_TILE
    per_b = seq // tm
    n_gate = ML_IN_PAD - 4 * ML_W
    w = jnp.zeros((D_MODEL, ML_IN_PAD), BF16).at[:, :ML_IN].set(w_in.astype(BF16))
    wt = w[:, 2 * ML_W:3 * ML_W].T
    wt = jnp.concatenate([wt, w[:, 4 * ML_W:].T], axis=0)
    gb = jnp.zeros((1, n_gate), F32).at[0, :2 * ML_HEADS].set(jnp.concatenate([igate_b, fgate_b]))
    row = lambda n: pl.BlockSpec((tm, n), lambda i: (i, 0))
    colblk = lambda n: pl.BlockSpec((n, tm), lambda i: (0, i))
    full = lambda a: pl.BlockSpec(a.shape, lambda i: (0,) * a.ndim)
    outs = [2 * ML_W, ML_W, ML_W, n_gate]
    return pl.pallas_call(
        _ml_in_body,
        grid=(t // tm,),
        in_specs=[row(D_MODEL), pl.BlockSpec((1, 6, D_MODEL), lambda i: (i // per_b, 0, 0)), full(w), full(wt),
                  full(gb), pl.BlockSpec((n_gate, 1), lambda i: (0, 0))],
        out_specs=[row(n) for n in outs] + [colblk(ML_W), colblk(n_gate)],
        out_shape=([jax.ShapeDtypeStruct((t, n), F32) for n in outs]
                   + [jax.ShapeDtypeStruct((ML_W, t), F32), jax.ShapeDtypeStruct((n_gate, t), F32)]),
        compiler_params=_cparams("parallel"),
        name="ml_in",
    )(x, mod6, w, wt, gb, gb.reshape(n_gate, 1))


def _mlstm_body(q_ref, k_ref, v_ref, og_ref, gt_ref, vt_ref, gtt_ref, wq_ref, wk_ref, bq_ref, bk_ref, ng_ref,
                y_ref, cq_ref, ck_ref, qs_ref, ks_ref, s_ref, n_ref, m_ref):
    tl = q_ref.shape[0]

    @pl.when(pl.program_id(1) == 0)
    def _():
        for ref in (s_ref, n_ref, m_ref, cq_ref, ck_ref):
            ref[...] = jnp.zeros_like(ref)

    def conv(x_ref, carry_ref, w_ref, b_ref, out_ref, scale):
        for h in range(ML_HEADS):
            cols = slice(ML_DH * h, ML_DH * (h + 1))
            x = x_ref[:, cols]
            xin = jnp.concatenate([carry_ref[:, cols], x], axis=0)
            acc = jnp.zeros((tl, ML_DH), F32) + b_ref[:, cols]
            for i in range(CONV_W):
                sh = CONV_W - 1 - i
                xs = xin if sh == 0 else pltpu.roll(xin, sh, 0)
                acc = acc + w_ref[i:i + 1, cols] * xs[8:8 + tl]
            carry_ref[:, cols] = x[tl - 8:tl]
            out_ref[:, cols] = _silu(acc) * scale

    conv(q_ref, cq_ref, wq_ref, bq_ref, qs_ref, 1.0)
    conv(k_ref, ck_ref, wk_ref, bk_ref, ks_ref, ML_DH ** -0.5)

    cs = ML_CHUNK
    row = lax.broadcasted_iota(jnp.int32, (cs, cs), 0)
    col = lax.broadcasted_iota(jnp.int32, (cs, cs), 1)
    causal = col <= row
    tri = causal.astype(BF16)
    tri_t = (row <= col).astype(BF16)

    for c in range(tl // cs):
        rows = slice(cs * c, cs * (c + 1))
        gt = gt_ref[rows, :]
        gt_t = gtt_ref[:, rows]
        lf_c = _log_sigmoid(gt)
        lf_r = _log_sigmoid(gt_t[ML_HEADS:2 * ML_HEADS])
        b_c_all = sum(_dot(tri, part) for part in _split3_bf16(lf_c))
        b_r_all = sum(_dot(part, tri_t) for part in _split3_bf16(lf_r))
        per_head = lambda f: jnp.stack([f(h) for h in range(ML_HEADS)])
        head_cols = lambda h: slice(ML_DH * h, ML_DH * (h + 1))
        q3 = per_head(lambda h: qs_ref[rows, head_cols(h)])
        k3 = per_head(lambda h: ks_ref[rows, head_cols(h)])
        vb = per_head(lambda h: v_ref[rows, head_cols(h)]).astype(BF16)
        vtb = per_head(lambda h: vt_ref[head_cols(h), rows]).astype(BF16)
        i_c = per_head(lambda h: gt[:, h:h + 1])
        b_c = per_head(lambda h: b_c_all[:, ML_HEADS + h:ML_HEADS + h + 1])
        i_r = per_head(lambda h: gt_t[h:h + 1, :])
        b_r = per_head(lambda h: b_r_all[h:h + 1, :])
        m = m_ref[:, :, 0:1]
        dmat = jnp.where(causal[None], b_c - b_r + i_r, -jnp.inf)
        inter_log = b_c + m
        m_t = jnp.maximum(inter_log, jnp.max(dmat, axis=2, keepdims=True))
        w_intra = jnp.exp(dmat - m_t)
        w_inter = jnp.exp(inter_log - m_t)
        qb = q3.astype(BF16)
        s = jnp.einsum('htd,hsd->hts', qb, k3.astype(BF16), preferred_element_type=F32) * w_intra
        state, nrm = s_ref[...], n_ref[...]
        num = (w_inter * jnp.einsum('htd,hvd->htv', qb, state.astype(BF16), preferred_element_type=F32)
               + jnp.einsum('hts,hsv->htv', s.astype(BF16), vb, preferred_element_type=F32))
        den = w_inter * jnp.sum(q3 * nrm, axis=2, keepdims=True) + jnp.sum(s, axis=2, keepdims=True)
        hc = num / jnp.maximum(jnp.abs(den), jnp.exp(-m_t))
        hc = hc * lax.rsqrt(jnp.mean(hc * hc, axis=2, keepdims=True) + RMS_EPS)
        for h in range(ML_HEADS):
            cols = head_cols(h)
            y_ref[rows, cols] = _sigmoid(og_ref[rows, cols]) * (hc[h] * ng_ref[:, cols])
        b_last = b_c[:, cs - 1:cs, :]
        gs_c = b_last - b_c + i_c
        gs_r = b_last - b_r + i_r
        m_new = jnp.maximum(b_last + m, jnp.max(gs_r, axis=2, keepdims=True))
        dec = jnp.exp(b_last + m - m_new)
        kw = k3 * jnp.exp(gs_c - m_new)
        s_ref[...] = dec * state + jnp.einsum('hvt,htd->hvd', vtb, kw.astype(BF16), preferred_element_type=F32)
        n_ref[...] = dec * nrm + jnp.sum(kw, axis=1, keepdims=True)
        m_ref[...] = jnp.broadcast_to(m_new, m_ref.shape)


def _mlstm(qk, v, og, gates, v_t, gates_t, conv_w, conv_b, norm_g, bsz, seq):
    tl = SEQ_TILE
    nl = seq // tl
    blk = lambda n, off: pl.BlockSpec((tl, n), lambda b, l: (b * nl + l, off))
    blk_t = lambda n: pl.BlockSpec((n, tl), lambda b, l: (0, b * nl + l))
    par = lambda rows, off: pl.BlockSpec((rows, ML_W), lambda b, l: (0, off))
    state = [pltpu.VMEM((ML_HEADS, ML_DH, ML_DH), F32), pltpu.VMEM((ML_HEADS, 1, ML_DH), F32),
             pltpu.VMEM((ML_HEADS, 1, ML_DH), F32)]
    return pl.pallas_call(
        _mlstm_body,
        grid=(bsz, nl),
        in_specs=[blk(ML_W, 0), blk(ML_W, 1), blk(ML_W, 0), blk(ML_W, 0), blk(gates.shape[1], 0),
                  blk_t(ML_W), blk_t(gates_t.shape[0]),
                  par(CONV_W, 0), par(CONV_W, 1), par(1, 0), par(1, 1), par(1, 0)],
        out_specs=blk(ML_W, 0),
        out_shape=jax.ShapeDtypeStruct((bsz * seq, ML_W), F32),
        scratch_shapes=[pltpu.VMEM((8, ML_W), F32), pltpu.VMEM((8, ML_W), F32),
                        pltpu.VMEM((tl, ML_W), F32), pltpu.VMEM((tl, ML_W), F32)] + state,
        compiler_params=_cparams("parallel", "arbitrary"),
        name="mlstm",
    )(qk, qk, v, og, gates, v_t, gates_t, conv_w, conv_w, conv_b.reshape(1, -1), conv_b.reshape(1, -1),
      norm_g.reshape(1, ML_W))


def _router_body(x_ref, mod_ref, rw_ref, rb_ref, info_ref, cnt_ref, carry_ref):
    tr = x_ref.shape[0]

    @pl.when(pl.program_id(0) == 0)
    def _():
        carry_ref[...] = jnp.zeros_like(carry_ref)

    h = x_ref[...] * (1.0 + mod_ref[0, 4:5, :]) + mod_ref[0, 3:4, :]
    logits = lax.dot_general(rw_ref[...], h, (((1,), (1,)), ((), ())), precision=lax.Precision.HIGHEST,
                             preferred_element_type=F32)
    aff = _sigmoid(logits)
    sel = aff + rb_ref[...]
    s = [sel[e:e + 1, :] for e in range(N_EXPERTS)]
    a = [aff[e:e + 1, :] for e in range(N_EXPERTS)]

    def top2_sum(v):
        best = v[0] + v[1]
        for lo_i, hi_i in zip(PAIR_LO[1:], PAIR_HI[1:]):
            best = jnp.maximum(best, v[lo_i] + v[hi_i])
        return best

    score = [top2_sum(s[EPG * g:EPG * (g + 1)]) for g in range(N_GROUPS)]
    g_idx = jnp.zeros((1, tr), jnp.int32)
    best = score[0]
    for g in range(1, N_GROUPS):
        upd = score[g] > best
        g_idx = jnp.where(upd, g, g_idx)
        best = jnp.where(upd, score[g], best)

    def pick_group(rows, j):
        out = rows[j]
        for g in range(1, N_GROUPS):
            out = jnp.where(g_idx == g, rows[EPG * g + j], out)
        return out

    sg = [pick_group(s, j) for j in range(EPG)]
    ag = [pick_group(a, j) for j in range(EPG)]
    first = jnp.zeros((1, tr), jnp.int32)
    best = sg[0]
    for j in range(1, EPG):
        upd = sg[j] > best
        first = jnp.where(upd, j, first)
        best = jnp.where(upd, sg[j], best)
    second = jnp.zeros((1, tr), jnp.int32)
    best2 = jnp.full((1, tr), -jnp.inf, F32)
    for j in range(EPG):
        upd = (first != j) & (sg[j] > best2)
        second = jnp.where(upd, j, second)
        best2 = jnp.where(upd, sg[j], best2)
    lo = jnp.minimum(first, second)
    hi = jnp.maximum(first, second)
    pair = jnp.where(lo == 0, hi - 1, jnp.where(lo == 1, hi + 1, N_PAIRS - 1))
    cls = g_idx * N_PAIRS + pair

    def pick_local(rows, idx):
        out = rows[0]
        for j in range(1, EPG):
            out = jnp.where(idx == j, rows[j], out)
        return out

    a_lo, a_hi = pick_local(ag, lo), pick_local(ag, hi)
    tot = a_lo + a_hi
    onehot = (lax.broadcasted_iota(jnp.int32, (32, tr), 0) == cls).astype(F32)
    before = (lax.broadcasted_iota(jnp.int32, (tr, tr), 0) < lax.broadcasted_iota(jnp.int32, (tr, tr), 1)).astype(BF16)
    cum = _dot(onehot.astype(BF16), before) + carry_ref[:, 0:1]
    rank = jnp.sum(onehot * cum, axis=0, keepdims=True)
    carry_ref[...] = carry_ref[...] + jnp.sum(onehot, axis=1, keepdims=True)
    cnt_ref[0] = carry_ref[...]
    zero = jnp.zeros((1, tr), F32)
    info_ref[...] = jnp.concatenate([cls.astype(F32), rank, a_lo / tot, a_hi / tot, zero, zero, zero, zero], axis=0)


def _router(x, mod6, router_w, router_bias, seq):
    t = x.shape[0]
    tr = ROUTE_TILE
    per_b = seq // tr
    return pl.pallas_call(
        _router_body,
        grid=(t // tr,),
        in_specs=[pl.BlockSpec((tr, D_MODEL), lambda i: (i, 0)),
                  pl.BlockSpec((1, 6, D_MODEL), lambda i: (i // per_b, 0, 0)),
                  pl.BlockSpec((N_EXPERTS, D_MODEL), lambda i: (0, 0)),
                  pl.BlockSpec((N_EXPERTS, 1), lambda i: (0, 0))],
        out_specs=[pl.BlockSpec((8, tr), lambda i: (0, i)), pl.BlockSpec((1, 32, 128), lambda i: (i, 0, 0))],
        out_shape=[jax.ShapeDtypeStruct((8, t), F32), jax.ShapeDtypeStruct((t // tr, 32, 128), F32)],
        scratch_shapes=[pltpu.VMEM((32, 128), F32)],
        compiler_params=_cparams("arbitrary"),
        name="router",
    )(x, mod6, router_w.T.astype(F32), router_bias.reshape(N_EXPERTS, 1).astype(F32))


def _from_row_tiles(ref):
    return jnp.concatenate([ref[:, j, :] for j in range(N_LANE_GROUPS)], axis=1)


def _store_row_tiles(ref, x):
    for j in range(N_LANE_GROUPS):
        ref[:, j, :] = x[:, 128 * j:128 * (j + 1)]


def _run_dmas(make_copy, length):
    off = jnp.int32(0)
    size = PERM_TILE
    while size >= 1:
        has = (length & size) != 0

        @pl.when(has)
        def _(off=off, size=size):
            make_copy(off, size).start()

        off = off + jnp.where(has, size, 0)
        size //= 2


def _dispatch_body(pos_ref, gstart_ref, lstart_ref, len_ref, x_ref, dst_in_ref, dst_ref, stage, ordered, sem):
    del dst_in_ref
    i = pl.program_id(0)
    tm = x_ref.shape[0]
    _store_row_tiles(stage, x_ref[...])

    def place(r, c):
        ordered[pos_ref[i * tm + r]] = stage[r]
        return c

    lax.fori_loop(0, tm, place, 0, unroll=8)
    for c in range(N_CLASSES):
        run = i * N_CLASSES + c
        g0, l0 = gstart_ref[run], lstart_ref[run]
        _run_dmas(lambda off, n: pltpu.make_async_copy(ordered.at[pl.ds(l0 + off, n)], dst_ref.at[pl.ds(g0 + off, n)],
                                                       sem.at[0]), len_ref[run])
    pltpu.make_async_copy(ordered, dst_ref.at[pl.ds(0, tm)], sem.at[0]).wait()


def _dispatch(pos, gstart, lstart, length, x, dst0):
    t = x.shape[0]
    tm = PERM_TILE
    return pl.pallas_call(
        _dispatch_body,
        grid_spec=pltpu.PrefetchScalarGridSpec(
            num_scalar_prefetch=4,
            grid=(t // tm,),
            in_specs=[pl.BlockSpec((tm, D_MODEL), lambda i, *s: (i, 0)), pl.BlockSpec(memory_space=pl.ANY)],
            out_specs=pl.BlockSpec(memory_space=pl.ANY),
            scratch_shapes=[pltpu.VMEM((tm, N_LANE_GROUPS, 128), F32)] * 2 + [pltpu.SemaphoreType.DMA((1,))]),
        out_shape=jax.ShapeDtypeStruct(dst0.shape, F32),
        input_output_aliases={5: 0},
        compiler_params=_cparams("arbitrary"),
        name="moe_dispatch",
    )(pos, gstart, lstart, length, x, dst0)


def _combine_body(pos_ref, gstart_ref, lstart_ref, len_ref, src_ref, o_ref, ordered, stage, sem):
    i = pl.program_id(0)
    tm = o_ref.shape[0]
    for c in range(N_CLASSES):
        run = i * N_CLASSES + c
        g0, l0 = gstart_ref[run], lstart_ref[run]
        _run_dmas(lambda off, n: pltpu.make_async_copy(src_ref.at[pl.ds(g0 + off, n)], ordered.at[pl.ds(l0 + off, n)],
                                                       sem.at[0]), len_ref[run])
    pltpu.make_async_copy(src_ref.at[pl.ds(0, tm)], ordered, sem.at[0]).wait()

    def take(r, c):
        stage[r] = ordered[pos_ref[i * tm + r]]
        return c

    lax.fori_loop(0, tm, take, 0, unroll=8)
    o_ref[...] = _from_row_tiles(stage)


def _combine(pos, gstart, lstart, length, src):
    t = pos.shape[0]
    tm = PERM_TILE
    return pl.pallas_call(
        _combine_body,
        grid_spec=pltpu.PrefetchScalarGridSpec(
            num_scalar_prefetch=4,
            grid=(t // tm,),
            in_specs=[pl.BlockSpec(memory_space=pl.ANY)],
            out_specs=pl.BlockSpec((tm, D_MODEL), lambda i, *s: (i, 0)),
            scratch_shapes=[pltpu.VMEM((tm, N_LANE_GROUPS, 128), F32)] * 2 + [pltpu.SemaphoreType.DMA((1,))]),
        out_shape=jax.ShapeDtypeStruct((t, D_MODEL), F32),
        compiler_params=_cparams("arbitrary"),
        name="moe_combine",
    )(pos, gstart, lstart, length, src)


def _moe_body(e_lo_ref, e_hi_ref, blk_ref, new_ref, used_ref, x_ref, meta_ref, mod_ref,
              wg_lo32, wu_lo32, wd_lo32, wg_hi32, wu_hi32, wd_hi32, lg_ref, lb_ref, o_ref,
              wg_lo, wu_lo, wd_lo, wg_hi, wu_hi, wd_hi):
    del e_lo_ref, e_hi_ref, blk_ref
    i = pl.program_id(0)
    used = used_ref[0]

    @pl.when(new_ref[i] == 1)
    def _():
        for src, dst in ((wg_lo32, wg_lo), (wu_lo32, wu_lo), (wd_lo32, wd_lo),
                         (wg_hi32, wg_hi), (wu_hi32, wu_hi), (wd_hi32, wd_hi)):
            dst[...] = src[...].astype(BF16)

    @pl.when(i >= used)
    def _():
        o_ref[...] = jnp.zeros_like(o_ref)

    @pl.when(i < used)
    def _():
        x = _from_row_tiles(x_ref)
        meta = meta_ref[...]
        bid, g_lo, g_hi = meta[:, 0:1], meta[:, 1:2], meta[:, 2:3]
        nb = mod_ref.shape[0]

        def per_row(r):
            out = mod_ref[0, r:r + 1, :]
            for b in range(1, nb):
                out = jnp.where(bid == float(b), mod_ref[b, r:r + 1, :], out)
            return out

        h = (x * (1.0 + per_row(4)) + per_row(3)).astype(BF16)

        def expert(wg, wu, wd):
            he = _silu(_dot(h, wg[...])) * _dot(h, wu[...])
            return _dot(he.astype(BF16), wd[...])

        y = g_lo * expert(wg_lo, wu_lo, wd_lo) + g_hi * expert(wg_hi, wu_hi, wd_hi)
        _store_row_tiles(o_ref, _layer_norm(ALPHA * x + (1.0 + per_row(5)) * y, lg_ref[...], lb_ref[...]))


def _moe_tiles(e_lo, e_hi, blk, new, used, xs, meta, mod6, layer, w_gate, w_up, w_down, ln_g, ln_b):
    tp = xs.shape[0]
    tm = ROW_TILE
    tiles = lambda index: pl.BlockSpec((tm, N_LANE_GROUPS, 128), index)
    wspec = lambda shape, which: pl.BlockSpec((None, None) + shape, lambda i, *s: (layer, s[which][i], 0, 0))
    full = lambda a: pl.BlockSpec(a.shape, lambda i, *s: (0,) * a.ndim)
    lg, lb = ln_g.reshape(1, D_MODEL), ln_b.reshape(1, D_MODEL)
    gu, dn = (D_MODEL, D_EXPERT), (D_EXPERT, D_MODEL)
    return pl.pallas_call(
        _moe_body,
        grid_spec=pltpu.PrefetchScalarGridSpec(
            num_scalar_prefetch=5,
            grid=(tp // tm,),
            in_specs=[tiles(lambda i, *s: (s[2][i], 0, 0)), pl.BlockSpec((tm, 8), lambda i, *s: (s[2][i], 0)), full(mod6),
                      wspec(gu, 0), wspec(gu, 0), wspec(dn, 0), wspec(gu, 1), wspec(gu, 1), wspec(dn, 1),
                      full(lg), full(lb)],
            out_specs=tiles(lambda i, *s: (i, 0, 0)),
            scratch_shapes=[pltpu.VMEM(gu, BF16), pltpu.VMEM(gu, BF16), pltpu.VMEM(dn, BF16)] * 2),
        out_shape=jax.ShapeDtypeStruct((tp, N_LANE_GROUPS, 128), F32),
        compiler_params=_cparams("arbitrary"),
        name="moe_tiles",
    )(e_lo, e_hi, blk, new, used, xs, meta, mod6, w_gate, w_up, w_down, w_gate, w_up, w_down, lg, lb)


def _moe_slots(t):
    n_tiles = t // ROW_TILE + N_CLASSES
    return n_tiles, n_tiles * ROW_TILE


def _moe_layer(x, slots, mod6, router_w, router_bias, layer, w_gate, w_up, w_down, ln_g, ln_b, bsz, seq):
    t = x.shape[0]
    tm = ROW_TILE
    n_tiles, tp = _moe_slots(t)
    info, seen = _router(x, mod6, router_w, router_bias, seq)
    cls = info[0].astype(jnp.int32)
    rank = info[1].astype(jnp.int32)
    seen = seen[:, :N_CLASSES, 0].astype(jnp.int32)
    cnt = seen[-1]
    tiles_c = (cnt + tm - 1) // tm
    tile_end = jnp.cumsum(tiles_c)
    offs = (tile_end - tiles_c) * tm
    onehot = cls[:, None] == jnp.arange(N_CLASSES)[None, :]
    dest = rank + jnp.sum(jnp.where(onehot, offs[None, :], 0), axis=1)
    used = tile_end[-1]
    tile_id = jnp.arange(n_tiles)
    blk = jnp.maximum(jnp.minimum(tile_id, used - 1), 0)
    tile_cls = jnp.minimum(jnp.sum(blk[:, None] >= tile_end[None, :], axis=1), N_CLASSES - 1)
    grp, pair = tile_cls // N_PAIRS, tile_cls % N_PAIRS
    e_lo = (grp * EPG + jnp.asarray(PAIR_LO, jnp.int32)[pair]).astype(jnp.int32)
    e_hi = (grp * EPG + jnp.asarray(PAIR_HI, jnp.int32)[pair]).astype(jnp.int32)
    new = jnp.concatenate([jnp.ones((1,), jnp.int32), (tile_cls[1:] != tile_cls[:-1]).astype(jnp.int32)])
    bid = (jnp.arange(t) // seq).astype(F32)
    meta_nat = jnp.stack([bid, info[2], info[3]] + [jnp.zeros((t,), F32)] * 5, axis=1)
    meta = jnp.zeros((tp, 8), F32).at[dest].set(meta_nat)
    pt = PERM_TILE
    before = jnp.concatenate([jnp.zeros((1, N_CLASSES), jnp.int32), seen[:-1]], axis=0)
    length = seen - before
    lstart = jnp.cumsum(length, axis=1) - length
    gstart = offs[None, :] + before
    shift = jnp.broadcast_to((lstart - before)[:, None, :], (t // pt, pt, N_CLASSES)).reshape(t, N_CLASSES)
    pos = rank + jnp.sum(jnp.where(onehot, shift, 0), axis=1)
    runs = (pos.astype(jnp.int32), gstart.reshape(-1).astype(jnp.int32), lstart.reshape(-1).astype(jnp.int32),
            length.reshape(-1).astype(jnp.int32))
    xs = _dispatch(*runs, x, slots)
    ys = _moe_tiles(e_lo, e_hi, blk.astype(jnp.int32), new, used.reshape(1).astype(jnp.int32), xs, meta,
                    mod6, layer, w_gate, w_up, w_down, ln_g, ln_b)
    return _combine(*runs, ys), xs


def kernel(x, c, router_w, router_bias, ada_w, ada_b, ln1_g, ln1_b, ln2_g, ln2_b, moe_w_gate, moe_w_up, moe_w_down,
           ab_w_in, s5_lam_re, s5_lam_im, s5_log_dt, s5_b_re, s5_b_im, s5_c_re, s5_c_im, s5_d, s5_glu_w, s5_glu_b,
           gla_gate_w, gla_gate_b, gla_norm_g, ab_w_out, ml_w_in, ml_conv_w, ml_conv_b, ml_igate_b, ml_fgate_b,
           ml_norm_g, ml_w_out):
    bsz, seq, d = x.shape
    t = bsz * seq
    mod = _ada_mod(c, ada_w, ada_b)
    xt = x.reshape(t, d)
    slots = jnp.zeros((_moe_slots(t)[1], N_LANE_GROUPS, 128), F32)
    for layer in range(DEPTH):
        j = layer // 2
        mod6 = mod[layer]
        if layer % 2 == 0:
            u, q, k, v, r, la = _ab_in(xt, mod6, ab_w_in[j], gla_gate_w[j], gla_gate_b[j], seq)
            tables = _s5_prep(s5_lam_re[j], s5_lam_im[j], s5_log_dt[j], s5_b_re[j], s5_b_im[j], s5_c_re[j], s5_c_im[j])
            ys = _s5_conv(u, tables, bsz, seq)
            yb = _gla(q, k, la, v, r, gla_norm_g[j], bsz, seq)
            w_out = ab_w_out[j].astype(BF16)
            consts = [s5_d[j].reshape(1, S5_W), s5_glu_w[j].astype(BF16), s5_glu_b[j].reshape(1, S5_W),
                      w_out[:S5_W], w_out[S5_W:]]
            xt = _out_call(_ab_out_body, "ab_out", [ys, u, yb], xt, mod6, consts, ln1_g[layer], ln1_b[layer], seq)
        else:
            qk, v, og, gates, v_t, gates_t = _ml_in(xt, mod6, ml_w_in[j], ml_igate_b[j], ml_fgate_b[j], seq)
            y = _mlstm(qk, v, og, gates, v_t, gates_t, ml_conv_w[j], ml_conv_b[j], ml_norm_g[j], bsz, seq)
            xt = _out_call(_ml_out_body, "ml_out", [y], xt, mod6, [ml_w_out[j].astype(BF16)],
                           ln1_g[layer], ln1_b[layer], seq)
        xt, slots = _moe_layer(xt, slots, mod6, router_w, router_bias, layer, moe_w_gate, moe_w_up, moe_w_down,
                               ln2_g[layer], ln2_b[layer], bsz, seq)
    return xt.reshape(bsz, seq, d)
```

```python
import functools
import math

import jax
import jax.numpy as jnp
from jax import lax
from jax.experimental import pallas as pl
from jax.experimental.pallas import tpu as pltpu

F32 = jnp.float32
BF16 = jnp.bfloat16

D_MODEL = 1024
DEPTH = 4
S5_W = 512
S5_GROUP = 16
S5_GROUPS = 32
S5_STATE = 64
S5_CHUNK = 16
S5_SCAN_ROWS = 8
GLA_HEADS = 4
GLA_DV = 128
GLA_DK = 64
GLA_QK = 256
GLA_V = 512
GLA_RANK = 16
GLA_GATE_NORM = 16.0
GLA_SUB = 16
AB_IN = S5_W + 2 * GLA_QK + 2 * GLA_V + GLA_RANK
AB_IN_PAD = 2176
ML_HEADS = 8
ML_W = 1024
ML_DH = 128
ML_IN = 4 * ML_W + 2 * ML_HEADS
ML_IN_PAD = 4224
CONV_W = 4
CHUNK = 64
ML_CHUNK = 128
N_EXPERTS = 16
N_GROUPS = 4
EPG = 4
D_EXPERT = 256
N_PAIRS = 6
N_CLASSES = N_GROUPS * N_PAIRS
PAIR_LO = (0, 0, 0, 1, 1, 2)
PAIR_HI = (1, 2, 3, 2, 3, 3)
ALPHA = (2.0 * DEPTH) ** 0.25
LN_EPS = 1e-5
RMS_EPS = 1e-6

V7X_VMEM_BYTES = 64 * 1024 * 1024
VMEM_LIMIT = (V7X_VMEM_BYTES * 3) // 4
ROW_TILE = 256
SEQ_TILE = 256
ROUTE_TILE = 512
PERM_TILE = 512
assert ROUTE_TILE == PERM_TILE
N_LANE_GROUPS = D_MODEL // 128


def _cparams(*sem):
    return pltpu.CompilerParams(dimension_semantics=sem, vmem_limit_bytes=VMEM_LIMIT)


def _sigmoid(x):
    return 1.0 / (1.0 + jnp.exp(-x))


def _silu(x):
    return x * _sigmoid(x)


def _log_sigmoid(z):
    return jnp.minimum(z, 0.0) - jnp.log1p(jnp.exp(-jnp.abs(z)))


def _gelu_tanh(x):
    return 0.5 * x * (1.0 + jnp.tanh(math.sqrt(2.0 / math.pi) * (x + 0.044715 * (x * x * x))))


def _layer_norm(x, g, b):
    mu = jnp.mean(x, axis=-1, keepdims=True)
    xc = x - mu
    var = jnp.mean(xc * xc, axis=-1, keepdims=True)
    return xc * lax.rsqrt(var + LN_EPS) * g + b


def _split3_bf16(x):
    hi = x.astype(BF16)
    r = x - hi.astype(F32)
    mid = r.astype(BF16)
    lo = (r - mid.astype(F32)).astype(BF16)
    return hi, mid, lo


def _dot(a, b):
    return jnp.dot(a, b, preferred_element_type=F32)


def _dot_nt(a, b):
    return lax.dot_general(a, b, (((1,), (1,)), ((), ())), preferred_element_type=F32)


def _dot_tn(a, b):
    return lax.dot_general(a, b, (((0,), (0,)), ((), ())), preferred_element_type=F32)


def _ada_body(c_ref, w_ref, b_ref, o_ref):
    c = c_ref[...]
    o_ref[0] = _dot(_silu(c), w_ref[0]) + b_ref[0]


def _ada_mod(c, ada_w, ada_b):
    bsz = c.shape[0]
    tn = 1536
    cp = jnp.zeros((8, D_MODEL), F32).at[:bsz].set(c)
    out = pl.pallas_call(
        _ada_body,
        grid=(DEPTH, 6 * D_MODEL // tn),
        in_specs=[pl.BlockSpec((8, D_MODEL), lambda l, j: (0, 0)),
                  pl.BlockSpec((1, D_MODEL, tn), lambda l, j: (l, 0, j)),
                  pl.BlockSpec((1, 1, tn), lambda l, j: (l, 0, j))],
        out_specs=pl.BlockSpec((1, 8, tn), lambda l, j: (l, 0, j)),
        out_shape=jax.ShapeDtypeStruct((DEPTH, 8, 6 * D_MODEL), F32),
        compiler_params=_cparams("parallel", "parallel"),
        name="ada_mod",
    )(cp, ada_w, ada_b.reshape(DEPTH, 1, 6 * D_MODEL))
    return out[:, :bsz].reshape(DEPTH, bsz, 6, D_MODEL)


def _ab_in_body(x_ref, mod_ref, w_ref, gw_ref, gb_ref, u_ref, q_ref, k_ref, v_ref, r_ref, la_ref):
    x = x_ref[...]
    h = (x * (1.0 + mod_ref[0, 1:2, :]) + mod_ref[0, 0:1, :]).astype(BF16)

    def seg(a, b):
        return _dot(h, w_ref[:, a:b])

    u_ref[...] = seg(0, 512)
    q_ref[...] = seg(512, 768)
    k_ref[...] = seg(768, 1024)
    v_ref[...] = seg(1024, 1536)
    r_ref[...] = seg(1536, 2048)
    a_lr = seg(2048, AB_IN_PAD)
    z = _dot(a_lr.astype(BF16), gw_ref[...]) + gb_ref[...]
    la_ref[...] = _log_sigmoid(z) * (1.0 / GLA_GATE_NORM)


def _ab_in(x, mod6, w_in, gate_w, gate_b, seq):
    t = x.shape[0]
    tm = ROW_TILE
    per_b = seq // tm
    w = jnp.zeros((D_MODEL, AB_IN_PAD), BF16).at[:, :AB_IN].set(w_in.astype(BF16))
    gw = jnp.zeros((AB_IN_PAD - 2048, GLA_QK), BF16).at[:GLA_RANK].set(gate_w.astype(BF16))
    row = lambda n: pl.BlockSpec((tm, n), lambda i: (i, 0))
    full = lambda a: pl.BlockSpec(a.shape, lambda i: (0,) * a.ndim)
    outs = [S5_W, GLA_QK, GLA_QK, GLA_V, GLA_V, GLA_QK]
    return pl.pallas_call(
        _ab_in_body,
        grid=(t // tm,),
        in_specs=[row(D_MODEL),
                  pl.BlockSpec((1, 6, D_MODEL), lambda i: (i // per_b, 0, 0)),
                  full(w), full(gw), pl.BlockSpec((1, GLA_QK), lambda i: (0, 0))],
        out_specs=[row(n) for n in outs],
        out_shape=[jax.ShapeDtypeStruct((t, n), F32) for n in outs],
        compiler_params=_cparams("parallel"),
        name="ab_in",
    )(x, mod6, w, gw, gate_b.reshape(1, GLA_QK))


def _s5_prep(lam_re, lam_im, log_dt, b_re, b_im, c_re, c_im):
    hp = lax.Precision.HIGHEST
    n = S5_CHUNK
    lre, lim = lam_re.astype(F32), lam_im.astype(F32)
    dt = jnp.exp(log_dt.astype(F32))[:, None]
    d = jnp.arange(n + 1, dtype=F32)[:, None, None]
    mag = jnp.exp(lre * dt * d)
    ang = lim * dt * d
    pw_re, pw_im = mag * jnp.cos(ang), mag * jnp.sin(ang)
    den = lre * lre + lim * lim
    nre, nim = pw_re[1] - 1.0, pw_im[1]
    coef_re = (nre * lre + nim * lim) / den
    coef_im = (nim * lre - nre * lim) / den
    bre, bim = b_re.astype(F32), b_im.astype(F32)
    bb_re = coef_re[..., None] * bre - coef_im[..., None] * bim
    bb_im = coef_re[..., None] * bim + coef_im[..., None] * bre
    cre, cim = c_re.astype(F32), c_im.astype(F32)
    cp_re = cre[None] * pw_re[:, :, None, :] - cim[None] * pw_im[:, :, None, :]
    cp_im = cre[None] * pw_im[:, :, None, :] + cim[None] * pw_re[:, :, None, :]
    kern = (jnp.einsum('dgcp,gpe->dgce', cp_re, bb_re, precision=hp)
            - jnp.einsum('dgcp,gpe->dgce', cp_im, bb_im, precision=hp))
    lag_is = (jnp.arange(n)[None, None, :] - jnp.arange(n)[None, :, None]
              == jnp.arange(n + 1)[:, None, None]).astype(F32)
    m_intra = jnp.einsum('djt,dgce->gjetc', lag_is, kern, precision=hp)
    m_intra = m_intra.reshape(S5_GROUPS, n * S5_GROUP, n * S5_GROUP)
    pws_re, pws_im = pw_re[n - 1 - jnp.arange(n)], pw_im[n - 1 - jnp.arange(n)]
    ws_re = pws_re[..., None] * bb_re[None] - pws_im[..., None] * bb_im[None]
    ws_im = pws_re[..., None] * bb_im[None] + pws_im[..., None] * bb_re[None]
    w_state = jnp.transpose(jnp.concatenate([ws_re, ws_im], axis=2), (1, 0, 3, 2))
    w_state = w_state.reshape(S5_GROUPS, n * S5_GROUP, 2 * S5_STATE)
    wi_re = jnp.transpose(cp_re[1:], (1, 3, 0, 2))
    wi_im = -jnp.transpose(cp_im[1:], (1, 3, 0, 2))
    w_inter = jnp.concatenate([wi_re, wi_im], axis=1).reshape(S5_GROUPS, 2 * S5_STATE, n * S5_GROUP)
    r = (n * jnp.arange(S5_SCAN_ROWS + 1, dtype=F32))[None, :, None]
    cmag, cang = jnp.exp(lre[:, None, :] * dt[:, None, :] * r), lim[:, None, :] * dt[:, None, :] * r
    a_pows = jnp.concatenate([cmag * jnp.cos(cang), cmag * jnp.sin(cang)], axis=1)
    return m_intra.astype(BF16), w_state.astype(BF16), w_inter.astype(BF16), a_pows


def _s5_body(u_ref, m_ref, ws_ref, wi_ref, a_ref, y_ref, ere, eim, sre, sim, *, bsz):
    u = u_ref[0]
    e = _dot(u, ws_ref[0])
    ere[...] = e[:, :S5_STATE]
    eim[...] = e[:, S5_STATE:]
    n_pow = S5_SCAN_ROWS + 1
    pr, pi = a_ref[0, 0:n_pow, :], a_ref[0, n_pow:2 * n_pow, :]
    tab_r, tab_i = pr[0:S5_SCAN_ROWS], pi[0:S5_SCAN_ROWS]
    rowi = lax.broadcasted_iota(jnp.int32, (S5_SCAN_ROWS, S5_STATE), 0)
    nk = ere.shape[0] // bsz

    def cmul(a_r, a_i, x_r, x_i):
        return a_r * x_r - a_i * x_i, a_r * x_i + a_i * x_r

    def shifted(x, s):
        return jnp.where(rowi >= s, pltpu.roll(x, s, 0), 0.0)

    def tile(m, carry):
        out = []
        for b in range(bsz):
            c_r, c_i = carry[2 * b], carry[2 * b + 1]
            r0 = pl.multiple_of(b * nk + m * S5_SCAN_ROWS, S5_SCAN_ROWS)
            x_r, x_i = ere[pl.ds(r0, S5_SCAN_ROWS), :], eim[pl.ds(r0, S5_SCAN_ROWS), :]
            for s in (1, 2, 4):
                d_r, d_i = cmul(pr[s:s + 1], pi[s:s + 1], shifted(x_r, s), shifted(x_i, s))
                x_r, x_i = x_r + d_r, x_i + d_i
            t_r, t_i = cmul(tab_r, tab_i, c_r, c_i)
            sre[pl.ds(r0, S5_SCAN_ROWS), :] = shifted(x_r, 1) + t_r
            sim[pl.ds(r0, S5_SCAN_ROWS), :] = shifted(x_i, 1) + t_i
            n_r, n_i = cmul(pr[S5_SCAN_ROWS:n_pow], pi[S5_SCAN_ROWS:n_pow], c_r, c_i)
            last = slice(S5_SCAN_ROWS - 1, S5_SCAN_ROWS)
            out += [n_r + jnp.broadcast_to(x_r[last], c_r.shape), n_i + jnp.broadcast_to(x_i[last], c_i.shape)]
        return tuple(out)

    z = jnp.zeros((S5_SCAN_ROWS, S5_STATE), F32)
    lax.fori_loop(0, nk // S5_SCAN_ROWS, tile, (z,) * (2 * bsz))
    y = _dot(u, m_ref[0])
    y = y + _dot(sre[...].astype(BF16), wi_ref[0, :S5_STATE, :])
    y = y + _dot(sim[...].astype(BF16), wi_ref[0, S5_STATE:, :])
    y_ref[0] = y


def _s5_pack_body(u_ref, o_ref):
    nk = u_ref.shape[0] // S5_CHUNK
    by_step = [u_ref[pl.ds(t, nk, stride=S5_CHUNK), :] for t in range(S5_CHUNK)]
    for g in range(128 // S5_GROUP):
        cols = slice(S5_GROUP * g, S5_GROUP * (g + 1))
        o_ref[g] = jnp.concatenate([x[:, cols] for x in by_step], axis=1).astype(o_ref.dtype)


def _s5_unpack_body(y_ref, o_ref):
    nk = y_ref.shape[1]
    groups = [y_ref[g] for g in range(128 // S5_GROUP)]
    for t in range(S5_CHUNK):
        cols = slice(S5_GROUP * t, S5_GROUP * (t + 1))
        o_ref[pl.ds(t, nk, stride=S5_CHUNK), :] = jnp.concatenate([y[:, cols] for y in groups], axis=1)


def _s5_conv(u, tables, bsz, seq):
    m_intra, w_state, w_inter, a_pows = tables
    t = bsz * seq
    rows = t // S5_CHUNK
    lanes = S5_CHUNK * S5_GROUP
    tm = PERM_TILE
    gpb = 128 // S5_GROUP
    tok = pl.BlockSpec((tm, 128), lambda i, j: (i, j))
    chunked = pl.BlockSpec((gpb, tm // S5_CHUNK, lanes), lambda i, j: (j, i, 0))
    uf = pl.pallas_call(
        _s5_pack_body,
        grid=(t // tm, S5_W // 128),
        in_specs=[tok],
        out_specs=chunked,
        out_shape=jax.ShapeDtypeStruct((S5_GROUPS, rows, lanes), BF16),
        compiler_params=_cparams("parallel", "parallel"),
        name="s5_pack",
    )(u)
    grp = lambda shape: pl.BlockSpec((1,) + shape, lambda g: (g, 0, 0))
    yf = pl.pallas_call(
        functools.partial(_s5_body, bsz=bsz),
        grid=(S5_GROUPS,),
        in_specs=[grp((rows, lanes)), grp((lanes, lanes)), grp((lanes, 2 * S5_STATE)),
                  grp((2 * S5_STATE, lanes)), grp(a_pows.shape[1:])],
        out_specs=grp((rows, lanes)),
        out_shape=jax.ShapeDtypeStruct((S5_GROUPS, rows, lanes), F32),
        scratch_shapes=[pltpu.VMEM((rows, S5_STATE), F32)] * 4,
        compiler_params=_cparams("parallel"),
        name="s5_conv",
    )(uf, m_intra, w_state, w_inter, a_pows)
    return pl.pallas_call(
        _s5_unpack_body,
        grid=(t // tm, S5_W // 128),
        in_specs=[chunked],
        out_specs=tok,
        out_shape=jax.ShapeDtypeStruct((t, S5_W), F32),
        compiler_params=_cparams("parallel", "parallel"),
        name="s5_unpack",
    )(yf)


def _gla_body(q_ref, k_ref, g_ref, v_ref, r_ref, ng_ref, y_ref, s_ref, bc_ref, o_ref):
    tl = q_ref.shape[0]
    scale = GLA_DK ** -0.5

    @pl.when(pl.program_id(1) == 0)
    def _():
        s_ref[...] = jnp.zeros_like(s_ref)

    row = lax.broadcasted_iota(jnp.int32, (CHUNK, CHUNK), 0)
    col = lax.broadcasted_iota(jnp.int32, (CHUNK, CHUNK), 1)
    tri = col <= row
    tref = col < (row // GLA_SUB) * GLA_SUB
    cum_mat = jnp.concatenate([tri, tref], axis=0).astype(BF16)
    lane_head = lax.broadcasted_iota(jnp.int32, (GLA_SUB, GLA_QK), 1) // GLA_DK
    bd_mask = (lax.broadcasted_iota(jnp.int32, (GLA_QK, GLA_V), 0) // GLA_DK
               == lax.broadcasted_iota(jnp.int32, (GLA_QK, GLA_V), 1) // GLA_DV)
    ones_bd = bd_mask.astype(BF16)
    ones_v = jnp.ones((CHUNK, GLA_V), BF16)

    for c in range(tl // CHUNK):
        r0 = c * CHUNK
        qc = q_ref[pl.ds(r0, CHUNK), :] * scale
        kc = k_ref[pl.ds(r0, CHUNK), :]
        g_parts = _split3_bf16(g_ref[pl.ds(r0, CHUNK), :])
        vb = v_ref[pl.ds(r0, CHUNK), :].astype(BF16)
        br = sum(_dot(cum_mat, part) for part in g_parts)
        bc, ref = br[:CHUNK], br[CHUNK:]
        bc_ref[pl.ds(r0, CHUNK), :] = bc
        qt = qc * jnp.exp(bc - ref)
        state = s_ref[...]
        o = _dot((qt * jnp.exp(ref)).astype(BF16), state.astype(BF16))
        blocks = [o[:GLA_SUB]]
        for i in range(1, CHUNK // GLA_SUB):
            ri = bc[GLA_SUB * i - 1:GLA_SUB * i, :]
            kt = (kc * jnp.exp(jnp.minimum(ri - bc, 0.0))).astype(BF16)
            qi = qt[GLA_SUB * i:GLA_SUB * (i + 1)]
            lhs = jnp.concatenate([jnp.where(lane_head == h, qi, 0.0) for h in range(GLA_HEADS)], axis=0)
            att = _dot_nt(lhs.astype(BF16), kt)
            att = jnp.where(col < GLA_SUB * i, att, 0.0)
            ov = _dot(att.astype(BF16), vb)
            oi = jnp.concatenate([ov[GLA_SUB * h:GLA_SUB * (h + 1), GLA_DV * h:GLA_DV * (h + 1)]
                                  for h in range(GLA_HEADS)], axis=1)
            blocks.append(o[GLA_SUB * i:GLA_SUB * (i + 1)] + oi)
        o_ref[pl.ds(r0, CHUNK), :] = jnp.concatenate(blocks, axis=0)
        bl = bc[CHUNK - 1:CHUNK, :]
        kh = (kc * jnp.exp(bl - bc)).astype(BF16)
        upd = _dot_tn(kh, vb)
        dcol = sum(_dot_tn(part, ones_v) for part in g_parts)
        s_ref[...] = jnp.exp(dcol) * state + jnp.where(bd_mask, upd, 0.0)

    q = q_ref[...] * scale
    k = k_ref[...]
    v = v_ref[...]
    bc = bc_ref[...]
    rmod = lax.broadcasted_iota(jnp.int32, (tl, 1), 0) % GLA_SUB
    od = jnp.zeros((tl, GLA_V), F32)
    for d in range(GLA_SUB):
        kd = k if d == 0 else pltpu.roll(k, d, 0)
        bd = bc if d == 0 else pltpu.roll(bc, d, 0)
        vd = v if d == 0 else pltpu.roll(v, d, 0)
        valid = rmod >= d
        e = jnp.exp(jnp.where(valid, bc - bd, 0.0))
        p = jnp.where(valid, q * kd * e, 0.0).astype(BF16)
        od = od + _dot(p, ones_bd) * vd
    o = o_ref[...] + od
    ng = ng_ref[...]
    outs = []
    for h in range(GLA_HEADS):
        oh = o[:, GLA_DV * h:GLA_DV * (h + 1)]
        outs.append(oh * lax.rsqrt(jnp.mean(oh * oh, axis=-1, keepdims=True) + RMS_EPS))
    y_ref[...] = jnp.concatenate(outs, axis=1) * ng * _silu(r_ref[...])


def _gla(q, k, la, v, r, norm_g, bsz, seq):
    tl = SEQ_TILE
    nl = seq // tl
    blk = lambda n: pl.BlockSpec((tl, n), lambda b, l: (b * nl + l, 0))
    return pl.pallas_call(
        _gla_body,
        grid=(bsz, nl),
        in_specs=[blk(GLA_QK), blk(GLA_QK), blk(GLA_QK), blk(GLA_V), blk(GLA_V),
                  pl.BlockSpec((1, GLA_V), lambda b, l: (0, 0))],
        out_specs=blk(GLA_V),
        out_shape=jax.ShapeDtypeStruct((bsz * seq, GLA_V), F32),
        scratch_shapes=[pltpu.VMEM((GLA_QK, GLA_V), F32), pltpu.VMEM((tl, GLA_QK), F32),
                        pltpu.VMEM((tl, GLA_V), F32)],
        compiler_params=_cparams("parallel", "arbitrary"),
        name="gla",
    )(q, k, la, v, r, norm_g.reshape(1, GLA_V))


def _residual_ln(x, y, mod_ref, gate_row, lg_ref, lb_ref):
    return _layer_norm(ALPHA * x + (1.0 + mod_ref[0, gate_row:gate_row + 1, :]) * y, lg_ref[...], lb_ref[...])


def _ab_out_body(ys_ref, u_ref, yb_ref, x_ref, mod_ref, d_ref, gw_ref, gb_ref, wa_ref, wb_ref, lg_ref, lb_ref, o_ref):
    z = _gelu_tanh(ys_ref[...] + d_ref[...] * u_ref[...])
    ya = z * _sigmoid(_dot(z.astype(BF16), gw_ref[...]) + gb_ref[...])
    y = _dot(ya.astype(BF16), wa_ref[...]) + _dot(yb_ref[...].astype(BF16), wb_ref[...])
    o_ref[...] = _residual_ln(x_ref[...], y, mod_ref, 2, lg_ref, lb_ref)


def _ml_out_body(y_ref, x_ref, mod_ref, w_ref, lg_ref, lb_ref, o_ref):
    y = _dot(y_ref[...].astype(BF16), w_ref[...])
    o_ref[...] = _residual_ln(x_ref[...], y, mod_ref, 2, lg_ref, lb_ref)


def _out_call(body, name, row_inputs, x, mod6, consts, ln_g, ln_b, seq):
    t = x.shape[0]
    tm = ROW_TILE
    per_b = seq // tm
    row = lambda a: pl.BlockSpec((tm, a.shape[1]), lambda i: (i, 0))
    full = lambda a: pl.BlockSpec(a.shape, lambda i: (0,) * a.ndim)
    lg, lb = ln_g.reshape(1, D_MODEL), ln_b.reshape(1, D_MODEL)
    return pl.pallas_call(
        body,
        grid=(t // tm,),
        in_specs=([row(a) for a in row_inputs] + [row(x), pl.BlockSpec((1, 6, D_MODEL), lambda i: (i // per_b, 0, 0))]
                  + [full(a) for a in consts] + [full(lg), full(lb)]),
        out_specs=row(x),
        out_shape=jax.ShapeDtypeStruct((t, D_MODEL), F32),
        compiler_params=_cparams("parallel"),
        name=name,
    )(*row_inputs, x, mod6, *consts, lg, lb)


def _ml_in_body(x_ref, mod_ref, w_ref, wt_ref, gb_ref, gbt_ref, qk_ref, v_ref, o_ref, gt_ref, vt_ref, gtt_ref):
    x = x_ref[...]
    h = (x * (1.0 + mod_ref[0, 1:2, :]) + mod_ref[0, 0:1, :]).astype(BF16)
    qk_ref[...] = _dot(h, w_ref[:, 0:2 * ML_W])
    v_ref[...] = _dot(h, w_ref[:, 2 * ML_W:3 * ML_W])
    o_ref[...] = _dot(h, w_ref[:, 3 * ML_W:4 * ML_W])
    gt_ref[...] = _dot(h, w_ref[:, 4 * ML_W:ML_IN_PAD]) + gb_ref[...]
    vt_ref[...] = _dot_nt(wt_ref[0:ML_W, :], h)
    gtt_ref[...] = _dot_nt(wt_ref[ML_W:, :], h) + gbt_ref[...]


def _ml_in(x, mod6, w_in, igate_b, fgate_b, seq):
    t = x.shape[0]
    tm = ROW_TILE
    per_b = seq // tm
    n_gate = ML_IN_PAD - 4 * ML_W
    w = jnp.zeros((D_MODEL, ML_IN_PAD), BF16).at[:, :ML_IN].set(w_in.astype(BF16))
    wt = w[:, 2 * ML_W:3 * ML_W].T
    wt = jnp.concatenate([wt, w[:, 4 * ML_W:].T], axis=0)
    gb = jnp.zeros((1, n_gate), F32).at[0, :2 * ML_HEADS].set(jnp.concatenate([igate_b, fgate_b]))
    row = lambda n: pl.BlockSpec((tm, n), lambda i: (i, 0))
    colblk = lambda n: pl.BlockSpec((n, tm), lambda i: (0, i))
    full = lambda a: pl.BlockSpec(a.shape, lambda i: (0,) * a.ndim)
    outs = [2 * ML_W, ML_W, ML_W, n_gate]
    return pl.pallas_call(
        _ml_in_body,
        grid=(t // tm,),
        in_specs=[row(D_MODEL), pl.BlockSpec((1, 6, D_MODEL), lambda i: (i // per_b, 0, 0)), full(w), full(wt),
                  full(gb), pl.BlockSpec((n_gate, 1), lambda i: (0, 0))],
        out_specs=[row(n) for n in outs] + [colblk(ML_W), colblk(n_gate)],
        out_shape=([jax.ShapeDtypeStruct((t, n), F32) for n in outs]
                   + [jax.ShapeDtypeStruct((ML_W, t), F32), jax.ShapeDtypeStruct((n_gate, t), F32)]),
        compiler_params=_cparams("parallel"),
        name="ml_in",
    )(x, mod6, w, wt, gb, gb.reshape(n_gate, 1))


def _mlstm_body(q_ref, k_ref, v_ref, og_ref, gt_ref, vt_ref, gtt_ref, wq_ref, wk_ref, bq_ref, bk_ref, ng_ref,
                y_ref, cq_ref, ck_ref, qs_ref, ks_ref, s_ref, n_ref, m_ref):
    tl = q_ref.shape[0]

    @pl.when(pl.program_id(1) == 0)
    def _():
        for ref in (s_ref, n_ref, m_ref, cq_ref, ck_ref):
            ref[...] = jnp.zeros_like(ref)

    def conv(x_ref, carry_ref, w_ref, b_ref, out_ref, scale):
        for h in range(ML_HEADS):
            cols = slice(ML_DH * h, ML_DH * (h + 1))
            x = x_ref[:, cols]
            xin = jnp.concatenate([carry_ref[:, cols], x], axis=0)
            acc = jnp.zeros((tl, ML_DH), F32) + b_ref[:, cols]
            for i in range(CONV_W):
                sh = CONV_W - 1 - i
                xs = xin if sh == 0 else pltpu.roll(xin, sh, 0)
                acc = acc + w_ref[i:i + 1, cols] * xs[8:8 + tl]
            carry_ref[:, cols] = x[tl - 8:tl]
            out_ref[:, cols] = _silu(acc) * scale

    conv(q_ref, cq_ref, wq_ref, bq_ref, qs_ref, 1.0)
    conv(k_ref, ck_ref, wk_ref, bk_ref, ks_ref, ML_DH ** -0.5)

    cs = ML_CHUNK
    row = lax.broadcasted_iota(jnp.int32, (cs, cs), 0)
    col = lax.broadcasted_iota(jnp.int32, (cs, cs), 1)
    causal = col <= row
    tri = causal.astype(BF16)
    tri_t = (row <= col).astype(BF16)

    for c in range(tl // cs):
        rows = slice(cs * c, cs * (c + 1))
        gt = gt_ref[rows, :]
        gt_t = gtt_ref[:, rows]
        lf_c = _log_sigmoid(gt)
        lf_r = _log_sigmoid(gt_t[ML_HEADS:2 * ML_HEADS])
        b_c_all = sum(_dot(tri, part) for part in _split3_bf16(lf_c))
        b_r_all = sum(_dot(part, tri_t) for part in _split3_bf16(lf_r))
        per_head = lambda f: jnp.stack([f(h) for h in range(ML_HEADS)])
        head_cols = lambda h: slice(ML_DH * h, ML_DH * (h + 1))
        q3 = per_head(lambda h: qs_ref[rows, head_cols(h)])
        k3 = per_head(lambda h: ks_ref[rows, head_cols(h)])
        vb = per_head(lambda h: v_ref[rows, head_cols(h)]).astype(BF16)
        vtb = per_head(lambda h: vt_ref[head_cols(h), rows]).astype(BF16)
        i_c = per_head(lambda h: gt[:, h:h + 1])
        b_c = per_head(lambda h: b_c_all[:, ML_HEADS + h:ML_HEADS + h + 1])
        i_r = per_head(lambda h: gt_t[h:h + 1, :])
        b_r = per_head(lambda h: b_r_all[h:h + 1, :])
        m = m_ref[:, :, 0:1]
        dmat = jnp.where(causal[None], b_c - b_r + i_r, -jnp.inf)
        inter_log = b_c + m
        m_t = jnp.maximum(inter_log, jnp.max(dmat, axis=2, keepdims=True))
        w_intra = jnp.exp(dmat - m_t)
        w_inter = jnp.exp(inter_log - m_t)
        qb = q3.astype(BF16)
        s = jnp.einsum('htd,hsd->hts', qb, k3.astype(BF16), preferred_element_type=F32) * w_intra
        state, nrm = s_ref[...], n_ref[...]
        num = (w_inter * jnp.einsum('htd,hvd->htv', qb, state.astype(BF16), preferred_element_type=F32)
               + jnp.einsum('hts,hsv->htv', s.astype(BF16), vb, preferred_element_type=F32))
        den = w_inter * jnp.sum(q3 * nrm, axis=2, keepdims=True) + jnp.sum(s, axis=2, keepdims=True)
        hc = num / jnp.maximum(jnp.abs(den), jnp.exp(-m_t))
        hc = hc * lax.rsqrt(jnp.mean(hc * hc, axis=2, keepdims=True) + RMS_EPS)
        for h in range(ML_HEADS):
            cols = head_cols(h)
            y_ref[rows, cols] = _sigmoid(og_ref[rows, cols]) * (hc[h] * ng_ref[:, cols])
        b_last = b_c[:, cs - 1:cs, :]
        gs_c = b_last - b_c + i_c
        gs_r = b_last - b_r + i_r
        m_new = jnp.maximum(b_last + m, jnp.max(gs_r, axis=2, keepdims=True))
        dec = jnp.exp(b_last + m - m_new)
        kw = k3 * jnp.exp(gs_c - m_new)
        s_ref[...] = dec * state + jnp.einsum('hvt,htd->hvd', vtb, kw.astype(BF16), preferred_element_type=F32)
        n_ref[...] = dec * nrm + jnp.sum(kw, axis=1, keepdims=True)
        m_ref[...] = jnp.broadcast_to(m_new, m_ref.shape)


def _mlstm(qk, v, og, gates, v_t, gates_t, conv_w, conv_b, norm_g, bsz, seq):
    tl = SEQ_TILE
    nl = seq // tl
    blk = lambda n, off: pl.BlockSpec((tl, n), lambda b, l: (b * nl + l, off))
    blk_t = lambda n: pl.BlockSpec((n, tl), lambda b, l: (0, b * nl + l))
    par = lambda rows, off: pl.BlockSpec((rows, ML_W), lambda b, l: (0, off))
    state = [pltpu.VMEM((ML_HEADS, ML_DH, ML_DH), F32), pltpu.VMEM((ML_HEADS, 1, ML_DH), F32),
             pltpu.VMEM((ML_HEADS, 1, ML_DH), F32)]
    return pl.pallas_call(
        _mlstm_body,
        grid=(bsz, nl),
        in_specs=[blk(ML_W, 0), blk(ML_W, 1), blk(ML_W, 0), blk(ML_W, 0), blk(gates.shape[1], 0),
                  blk_t(ML_W), blk_t(gates_t.shape[0]),
                  par(CONV_W, 0), par(CONV_W, 1), par(1, 0), par(1, 1), par(1, 0)],
        out_specs=blk(ML_W, 0),
        out_shape=jax.ShapeDtypeStruct((bsz * seq, ML_W), F32),
        scratch_shapes=[pltpu.VMEM((8, ML_W), F32), pltpu.VMEM((8, ML_W), F32),
                        pltpu.VMEM((tl, ML_W), F32), pltpu.VMEM((tl, ML_W), F32)] + state,
        compiler_params=_cparams("parallel", "arbitrary"),
        name="mlstm",
    )(qk, qk, v, og, gates, v_t, gates_t, conv_w, conv_w, conv_b.reshape(1, -1), conv_b.reshape(1, -1),
      norm_g.reshape(1, ML_W))


def _router_body(x_ref, mod_ref, rw_ref, rb_ref, info_ref, cnt_ref, carry_ref):
    tr = x_ref.shape[0]

    @pl.when(pl.program_id(0) == 0)
    def _():
        carry_ref[...] = jnp.zeros_like(carry_ref)

    h = x_ref[...] * (1.0 + mod_ref[0, 4:5, :]) + mod_ref[0, 3:4, :]
    h_hi = h.astype(BF16)
    h_lo = (h - h_hi.astype(F32)).astype(BF16)
    w = rw_ref[...]
    w_hi = w.astype(BF16)
    w_lo = (w - w_hi.astype(F32)).astype(BF16)
    logits = _dot_nt(w_hi, h_hi) + (_dot_nt(w_lo, h_hi) + _dot_nt(w_hi, h_lo))
    aff = _sigmoid(logits)
    sel = aff + rb_ref[...]
    s = [sel[e:e + 1, :] for e in range(N_EXPERTS)]
    a = [aff[e:e + 1, :] for e in range(N_EXPERTS)]

    def top2_sum(v):
        best = v[0] + v[1]
        for lo_i, hi_i in zip(PAIR_LO[1:], PAIR_HI[1:]):
            best = jnp.maximum(best, v[lo_i] + v[hi_i])
        return best

    score = [top2_sum(s[EPG * g:EPG * (g + 1)]) for g in range(N_GROUPS)]
    g_idx = jnp.zeros((1, tr), jnp.int32)
    best = score[0]
    for g in range(1, N_GROUPS):
        upd = score[g] > best
        g_idx = jnp.where(upd, g, g_idx)
        best = jnp.where(upd, score[g], best)

    def pick_group(rows, j):
        out = rows[j]
        for g in range(1, N_GROUPS):
            out = jnp.where(g_idx == g, rows[EPG * g + j], out)
        return out

    sg = [pick_group(s, j) for j in range(EPG)]
    ag = [pick_group(a, j) for j in range(EPG)]
    first = jnp.zeros((1, tr), jnp.int32)
    best = sg[0]
    for j in range(1, EPG):
        upd = sg[j] > best
        first = jnp.where(upd, j, first)
        best = jnp.where(upd, sg[j], best)
    second = jnp.zeros((1, tr), jnp.int32)
    best2 = jnp.full((1, tr), -jnp.inf, F32)
    for j in range(EPG):
        upd = (first != j) & (sg[j] > best2)
        second = jnp.where(upd, j, second)
        best2 = jnp.where(upd, sg[j], best2)
    lo = jnp.minimum(first, second)
    hi = jnp.maximum(first, second)
    pair = jnp.where(lo == 0, hi - 1, jnp.where(lo == 1, hi + 1, N_PAIRS - 1))
    cls = g_idx * N_PAIRS + pair

    def pick_local(rows, idx):
        out = rows[0]
        for j in range(1, EPG):
            out = jnp.where(idx == j, rows[j], out)
        return out

    a_lo, a_hi = pick_local(ag, lo), pick_local(ag, hi)
    tot = a_lo + a_hi
    onehot = (lax.broadcasted_iota(jnp.int32, (32, tr), 0) == cls).astype(F32)
    before = (lax.broadcasted_iota(jnp.int32, (tr, tr), 0) < lax.broadcasted_iota(jnp.int32, (tr, tr), 1)).astype(BF16)
    cum = _dot(onehot.astype(BF16), before) + carry_ref[:, 0:1]
    rank = jnp.sum(onehot * cum, axis=0, keepdims=True)
    carry_ref[...] = carry_ref[...] + jnp.sum(onehot, axis=1, keepdims=True)
    cnt_ref[0] = carry_ref[...]
    zero = jnp.zeros((1, tr), F32)
    info_ref[...] = jnp.concatenate([cls.astype(F32), rank, a_lo / tot, a_hi / tot, zero, zero, zero, zero], axis=0)


def _router(x, mod6, router_w, router_bias, seq):
    t = x.shape[0]
    tr = ROUTE_TILE
    per_b = seq // tr
    return pl.pallas_call(
        _router_body,
        grid=(t // tr,),
        in_specs=[pl.BlockSpec((tr, D_MODEL), lambda i: (i, 0)),
                  pl.BlockSpec((1, 6, D_MODEL), lambda i: (i // per_b, 0, 0)),
                  pl.BlockSpec((N_EXPERTS, D_MODEL), lambda i: (0, 0)),
                  pl.BlockSpec((N_EXPERTS, 1), lambda i: (0, 0))],
        out_specs=[pl.BlockSpec((8, tr), lambda i: (0, i)), pl.BlockSpec((1, 32, 128), lambda i: (i, 0, 0))],
        out_shape=[jax.ShapeDtypeStruct((8, t), F32), jax.ShapeDtypeStruct((t // tr, 32, 128), F32)],
        scratch_shapes=[pltpu.VMEM((32, 128), F32)],
        compiler_params=_cparams("arbitrary"),
        name="router",
    )(x, mod6, router_w.T.astype(F32), router_bias.reshape(N_EXPERTS, 1).astype(F32))


def _from_row_tiles(ref):
    return jnp.concatenate([ref[:, j, :] for j in range(N_LANE_GROUPS)], axis=1)


def _store_row_tiles(ref, x):
    for j in range(N_LANE_GROUPS):
        ref[:, j, :] = x[:, 128 * j:128 * (j + 1)]


def _run_dmas(make_copy, length):
    off = jnp.int32(0)
    size = PERM_TILE
    while size >= 1:
        has = (length & size) != 0

        @pl.when(has)
        def _(off=off, size=size):
            make_copy(off, size).start()

        off = off + jnp.where(has, size, 0)
        size //= 2


def _dispatch_body(pos_ref, gstart_ref, lstart_ref, len_ref, x_ref, dst_in_ref, dst_ref, stage, ordered, sem):
    del dst_in_ref
    i = pl.program_id(0)
    tm = x_ref.shape[0]
    _store_row_tiles(stage, x_ref[...])

    def place(r, c):
        ordered[pos_ref[i * tm + r]] = stage[r]
        return c

    lax.fori_loop(0, tm, place, 0, unroll=8)
    for c in range(N_CLASSES):
        run = i * N_CLASSES + c
        g0, l0 = gstart_ref[run], lstart_ref[run]
        _run_dmas(lambda off, n: pltpu.make_async_copy(ordered.at[pl.ds(l0 + off, n)], dst_ref.at[pl.ds(g0 + off, n)],
                                                       sem.at[0]), len_ref[run])
    pltpu.make_async_copy(ordered, dst_ref.at[pl.ds(0, tm)], sem.at[0]).wait()


def _dispatch(pos, gstart, lstart, length, x, dst0):
    t = x.shape[0]
    tm = PERM_TILE
    return pl.pallas_call(
        _dispatch_body,
        grid_spec=pltpu.PrefetchScalarGridSpec(
            num_scalar_prefetch=4,
            grid=(t // tm,),
            in_specs=[pl.BlockSpec((tm, D_MODEL), lambda i, *s: (i, 0)), pl.BlockSpec(memory_space=pl.ANY)],
            out_specs=pl.BlockSpec(memory_space=pl.ANY),
            scratch_shapes=[pltpu.VMEM((tm, N_LANE_GROUPS, 128), F32)] * 2 + [pltpu.SemaphoreType.DMA((1,))]),
        out_shape=jax.ShapeDtypeStruct(dst0.shape, F32),
        input_output_aliases={5: 0},
        compiler_params=_cparams("arbitrary"),
        name="moe_dispatch",
    )(pos, gstart, lstart, length, x, dst0)


def _combine_body(pos_ref, gstart_ref, lstart_ref, len_ref, src_ref, o_ref, ordered, stage, sem):
    i = pl.program_id(0)
    tm = o_ref.shape[0]
    for c in range(N_CLASSES):
        run = i * N_CLASSES + c
        g0, l0 = gstart_ref[run], lstart_ref[run]
        _run_dmas(lambda off, n: pltpu.make_async_copy(src_ref.at[pl.ds(g0 + off, n)], ordered.at[pl.ds(l0 + off, n)],
                                                       sem.at[0]), len_ref[run])
    pltpu.make_async_copy(src_ref.at[pl.ds(0, tm)], ordered, sem.at[0]).wait()

    def take(r, c):
        stage[r] = ordered[pos_ref[i * tm + r]]
        return c

    lax.fori_loop(0, tm, take, 0, unroll=8)
    o_ref[...] = _from_row_tiles(stage)


def _combine(pos, gstart, lstart, length, src):
    t = pos.shape[0]
    tm = PERM_TILE
    return pl.pallas_call(
        _combine_body,
        grid_spec=pltpu.PrefetchScalarGridSpec(
            num_scalar_prefetch=4,
            grid=(t // tm,),
            in_specs=[pl.BlockSpec(memory_space=pl.ANY)],
            out_specs=pl.BlockSpec((tm, D_MODEL), lambda i, *s: (i, 0)),
            scratch_shapes=[pltpu.VMEM((tm, N_LANE_GROUPS, 128), F32)] * 2 + [pltpu.SemaphoreType.DMA((1,))]),
        out_shape=jax.ShapeDtypeStruct((t, D_MODEL), F32),
        compiler_params=_cparams("arbitrary"),
        name="moe_combine",
    )(pos, gstart, lstart, length, src)


def _moe_body(e_lo_ref, e_hi_ref, blk_ref, new_ref, used_ref, x_ref, meta_ref, mod_ref,
              wg_lo32, wu_lo32, wd_lo32, wg_hi32, wu_hi32, wd_hi32, lg_ref, lb_ref, o_ref,
              wg_lo, wu_lo, wd_lo, wg_hi, wu_hi, wd_hi):
    del e_lo_ref, e_hi_ref, blk_ref
    i = pl.program_id(0)
    used = used_ref[0]

    @pl.when(new_ref[i] == 1)
    def _():
        for src, dst in ((wg_lo32, wg_lo), (wu_lo32, wu_lo), (wd_lo32, wd_lo),
                         (wg_hi32, wg_hi), (wu_hi32, wu_hi), (wd_hi32, wd_hi)):
            dst[...] = src[...].astype(BF16)

    @pl.when(i >= used)
    def _():
        o_ref[...] = jnp.zeros_like(o_ref)

    @pl.when(i < used)
    def _():
        x = _from_row_tiles(x_ref)
        meta = meta_ref[...]
        bid, g_lo, g_hi = meta[:, 0:1], meta[:, 1:2], meta[:, 2:3]
        nb = mod_ref.shape[0]

        def per_row(r):
            out = mod_ref[0, r:r + 1, :]
            for b in range(1, nb):
                out = jnp.where(bid == float(b), mod_ref[b, r:r + 1, :], out)
            return out

        h = (x * (1.0 + per_row(4)) + per_row(3)).astype(BF16)

        def expert(wg, wu, wd):
            he = _silu(_dot(h, wg[...])) * _dot(h, wu[...])
            return _dot(he.astype(BF16), wd[...])

        y = g_lo * expert(wg_lo, wu_lo, wd_lo) + g_hi * expert(wg_hi, wu_hi, wd_hi)
        _store_row_tiles(o_ref, _layer_norm(ALPHA * x + (1.0 + per_row(5)) * y, lg_ref[...], lb_ref[...]))


def _moe_tiles(e_lo, e_hi, blk, new, used, xs, meta, mod6, layer, w_gate, w_up, w_down, ln_g, ln_b):
    tp = xs.shape[0]
    tm = ROW_TILE
    tiles = lambda index: pl.BlockSpec((tm, N_LANE_GROUPS, 128), index)
    wspec = lambda shape, which: pl.BlockSpec((None, None) + shape, lambda i, *s: (layer, s[which][i], 0, 0))
    full = lambda a: pl.BlockSpec(a.shape, lambda i, *s: (0,) * a.ndim)
    lg, lb = ln_g.reshape(1, D_MODEL), ln_b.reshape(1, D_MODEL)
    gu, dn = (D_MODEL, D_EXPERT), (D_EXPERT, D_MODEL)
    return pl.pallas_call(
        _moe_body,
        grid_spec=pltpu.PrefetchScalarGridSpec(
            num_scalar_prefetch=5,
            grid=(tp // tm,),
            in_specs=[tiles(lambda i, *s: (s[2][i], 0, 0)), pl.BlockSpec((tm, 8), lambda i, *s: (s[2][i], 0)), full(mod6),
                      wspec(gu, 0), wspec(gu, 0), wspec(dn, 0), wspec(gu, 1), wspec(gu, 1), wspec(dn, 1),
                      full(lg), full(lb)],
            out_specs=tiles(lambda i, *s: (i, 0, 0)),
            scratch_shapes=[pltpu.VMEM(gu, BF16), pltpu.VMEM(gu, BF16), pltpu.VMEM(dn, BF16)] * 2),
        out_shape=jax.ShapeDtypeStruct((tp, N_LANE_GROUPS, 128), F32),
        compiler_params=_cparams("arbitrary"),
        name="moe_tiles",
    )(e_lo, e_hi, blk, new, used, xs, meta, mod6, w_gate, w_up, w_down, w_gate, w_up, w_down, lg, lb)


def _moe_slots(t):
    n_tiles = t // ROW_TILE + N_CLASSES
    return n_tiles, n_tiles * ROW_TILE


def _moe_layer(x, slots, mod6, router_w, router_bias, layer, w_gate, w_up, w_down, ln_g, ln_b, bsz, seq):
    t = x.shape[0]
    tm = ROW_TILE
    n_tiles, tp = _moe_slots(t)
    info, seen = _router(x, mod6, router_w, router_bias, seq)
    cls = info[0].astype(jnp.int32)
    rank = info[1].astype(jnp.int32)
    seen = seen[:, :N_CLASSES, 0].astype(jnp.int32)
    cnt = seen[-1]
    tiles_c = (cnt + tm - 1) // tm
    tile_end = jnp.cumsum(tiles_c)
    offs = (tile_end - tiles_c) * tm
    onehot = cls[:, None] == jnp.arange(N_CLASSES)[None, :]
    dest = rank + jnp.sum(jnp.where(onehot, offs[None, :], 0), axis=1)
    used = tile_end[-1]
    tile_id = jnp.arange(n_tiles)
    blk = jnp.maximum(jnp.minimum(tile_id, used - 1), 0)
    tile_cls = jnp.minimum(jnp.sum(blk[:, None] >= tile_end[None, :], axis=1), N_CLASSES - 1)
    grp, pair = tile_cls // N_PAIRS, tile_cls % N_PAIRS
    e_lo = (grp * EPG + jnp.asarray(PAIR_LO, jnp.int32)[pair]).astype(jnp.int32)
    e_hi = (grp * EPG + jnp.asarray(PAIR_HI, jnp.int32)[pair]).astype(jnp.int32)
    new = jnp.concatenate([jnp.ones((1,), jnp.int32), (tile_cls[1:] != tile_cls[:-1]).astype(jnp.int32)])
    bid = (jnp.arange(t) // seq).astype(F32)
    meta_nat = jnp.stack([bid, info[2], info[3]] + [jnp.zeros((t,), F32)] * 5, axis=1)
    meta = jnp.zeros((tp, 8), F32).at[dest].set(meta_nat)
    pt = PERM_TILE
    before = jnp.concatenate([jnp.zeros((1, N_CLASSES), jnp.int32), seen[:-1]], axis=0)
    length = seen - before
    lstart = jnp.cumsum(length, axis=1) - length
    gstart = offs[None, :] + before
    shift = jnp.broadcast_to((lstart - before)[:, None, :], (t // pt, pt, N_CLASSES)).reshape(t, N_CLASSES)
    pos = rank + jnp.sum(jnp.where(onehot, shift, 0), axis=1)
    runs = (pos.astype(jnp.int32), gstart.reshape(-1).astype(jnp.int32), lstart.reshape(-1).astype(jnp.int32),
            length.reshape(-1).astype(jnp.int32))
    xs = _dispatch(*runs, x, slots)
    ys = _moe_tiles(e_lo, e_hi, blk.astype(jnp.int32), new, used.reshape(1).astype(jnp.int32), xs, meta,
                    mod6, layer, w_gate, w_up, w_down, ln_g, ln_b)
    return _combine(*runs, ys), xs


def kernel(x, c, router_w, router_bias, ada_w, ada_b, ln1_g, ln1_b, ln2_g, ln2_b, moe_w_gate, moe_w_up, moe_w_down,
           ab_w_in, s5_lam_re, s5_lam_im, s5_log_dt, s5_b_re, s5_b_im, s5_c_re, s5_c_im, s5_d, s5_glu_w, s5_glu_b,
           gla_gate_w, gla_gate_b, gla_norm_g, ab_w_out, ml_w_in, ml_conv_w, ml_conv_b, ml_igate_b, ml_fgate_b,
           ml_norm_g, ml_w_out):
    bsz, seq, d = x.shape
    t = bsz * seq
    mod = _ada_mod(c, ada_w, ada_b)
    xt = x.reshape(t, d)
    slots = jnp.zeros((_moe_slots(t)[1], N_LANE_GROUPS, 128), F32)
    for layer in range(DEPTH):
        j = layer // 2
        mod6 = mod[layer]
        if layer % 2 == 0:
            u, q, k, v, r, la = _ab_in(xt, mod6, ab_w_in[j], gla_gate_w[j], gla_gate_b[j], seq)
            tables = _s5_prep(s5_lam_re[j], s5_lam_im[j], s5_log_dt[j], s5_b_re[j], s5_b_im[j], s5_c_re[j], s5_c_im[j])
            ys = _s5_conv(u, tables, bsz, seq)
            yb = _gla(q, k, la, v, r, gla_norm_g[j], bsz, seq)
            w_out = ab_w_out[j].astype(BF16)
            consts = [s5_d[j].reshape(1, S5_W), s5_glu_w[j].astype(BF16), s5_glu_b[j].reshape(1, S5_W),
                      w_out[:S5_W], w_out[S5_W:]]
            xt = _out_call(_ab_out_body, "ab_out", [ys, u, yb], xt, mod6, consts, ln1_g[layer], ln1_b[layer], seq)
        else:
            qk, v, og, gates, v_t, gates_t = _ml_in(xt, mod6, ml_w_in[j], ml_igate_b[j], ml_fgate_b[j], seq)
            y = _mlstm(qk, v, og, gates, v_t, gates_t, ml_conv_w[j], ml_conv_b[j], ml_norm_g[j], bsz, seq)
            xt = _out_call(_ml_out_body, "ml_out", [y], xt, mod6, [ml_w_out[j].astype(BF16)],
                           ln1_g[layer], ln1_b[layer], seq)
        xt, slots = _moe_layer(xt, slots, mod6, router_w, router_bias, layer, moe_w_gate, moe_w_up, moe_w_down,
                               ln2_g[layer], ln2_b[layer], bsz, seq)
    return xt.reshape(bsz, seq, d)
```

```python
import functools
import math

import jax
import jax.numpy as jnp
from jax import lax
from jax.experimental import pallas as pl
from jax.experimental.pallas import tpu as pltpu

F32 = jnp.float32
BF16 = jnp.bfloat16

D_MODEL = 1024
DEPTH = 4
S5_W = 512
S5_GROUP = 16
S5_GROUPS = 32
S5_STATE = 64
S5_CHUNK = 16
S5_SCAN_ROWS = 8
GLA_HEADS = 4
GLA_DV = 128
GLA_DK = 64
GLA_QK = 256
GLA_V = 512
GLA_RANK = 16
GLA_GATE_NORM = 16.0
GLA_SUB = 16
AB_IN = S5_W + 2 * GLA_QK + 2 * GLA_V + GLA_RANK
AB_IN_PAD = 2176
ML_HEADS = 8
ML_W = 1024
ML_DH = 128
ML_IN = 4 * ML_W + 2 * ML_HEADS
ML_IN_PAD = 4224
CONV_W = 4
CHUNK = 64
ML_CHUNK = 128
N_EXPERTS = 16
N_GROUPS = 4
EPG = 4
D_EXPERT = 256
N_PAIRS = 6
N_CLASSES = N_GROUPS * N_PAIRS
PAIR_LO = (0, 0, 0, 1, 1, 2)
PAIR_HI = (1, 2, 3, 2, 3, 3)
ALPHA = (2.0 * DEPTH) ** 0.25
LN_EPS = 1e-5
RMS_EPS = 1e-6

V7X_VMEM_BYTES = 64 * 1024 * 1024
VMEM_LIMIT = (V7X_VMEM_BYTES * 3) // 4
ROW_TILE = 256
SEQ_TILE = 256
ROUTE_TILE = 512
PERM_TILE = 512
assert ROUTE_TILE == PERM_TILE
N_LANE_GROUPS = D_MODEL // 128


def _cparams(*sem):
    return pltpu.CompilerParams(dimension_semantics=sem, vmem_limit_bytes=VMEM_LIMIT)


def _sigmoid(x):
    return 1.0 / (1.0 + jnp.exp(-x))


def _silu(x):
    return x * _sigmoid(x)


def _log_sigmoid(z):
    return jnp.minimum(z, 0.0) - jnp.log1p(jnp.exp(-jnp.abs(z)))


def _gelu_tanh(x):
    return 0.5 * x * (1.0 + jnp.tanh(math.sqrt(2.0 / math.pi) * (x + 0.044715 * (x * x * x))))


def _layer_norm(x, g, b):
    mu = jnp.mean(x, axis=-1, keepdims=True)
    xc = x - mu
    var = jnp.mean(xc * xc, axis=-1, keepdims=True)
    return xc * lax.rsqrt(var + LN_EPS) * g + b


def _split3_bf16(x):
    hi = x.astype(BF16)
    r = x - hi.astype(F32)
    mid = r.astype(BF16)
    lo = (r - mid.astype(F32)).astype(BF16)
    return hi, mid, lo


def _dot(a, b):
    return jnp.dot(a, b, preferred_element_type=F32)


def _dot_nt(a, b):
    return lax.dot_general(a, b, (((1,), (1,)), ((), ())), preferred_element_type=F32)


def _dot_tn(a, b):
    return lax.dot_general(a, b, (((0,), (0,)), ((), ())), preferred_element_type=F32)


def _ada_body(c_ref, w_ref, b_ref, o_ref):
    c = c_ref[...]
    o_ref[0] = _dot(_silu(c), w_ref[0]) + b_ref[0]


def _ada_mod(c, ada_w, ada_b):
    bsz = c.shape[0]
    tn = 1536
    cp = jnp.zeros((8, D_MODEL), F32).at[:bsz].set(c)
    out = pl.pallas_call(
        _ada_body,
        grid=(DEPTH, 6 * D_MODEL // tn),
        in_specs=[pl.BlockSpec((8, D_MODEL), lambda l, j: (0, 0)),
                  pl.BlockSpec((1, D_MODEL, tn), lambda l, j: (l, 0, j)),
                  pl.BlockSpec((1, 1, tn), lambda l, j: (l, 0, j))],
        out_specs=pl.BlockSpec((1, 8, tn), lambda l, j: (l, 0, j)),
        out_shape=jax.ShapeDtypeStruct((DEPTH, 8, 6 * D_MODEL), F32),
        compiler_params=_cparams("parallel", "parallel"),
        name="ada_mod",
    )(cp, ada_w, ada_b.reshape(DEPTH, 1, 6 * D_MODEL))
    return out[:, :bsz].reshape(DEPTH, bsz, 6, D_MODEL)


def _ab_in_body(x_ref, mod_ref, w_ref, gw_ref, gb_ref, u_ref, q_ref, k_ref, v_ref, r_ref, la_ref):
    x = x_ref[...]
    h = (x * (1.0 + mod_ref[0, 1:2, :]) + mod_ref[0, 0:1, :]).astype(BF16)

    def seg(a, b):
        return _dot(h, w_ref[:, a:b])

    u_ref[...] = seg(0, 512)
    q_ref[...] = seg(512, 768)
    k_ref[...] = seg(768, 1024)
    v_ref[...] = seg(1024, 1536)
    r_ref[...] = seg(1536, 2048)
    a_lr = seg(2048, AB_IN_PAD)
    z = _dot(a_lr.astype(BF16), gw_ref[...]) + gb_ref[...]
    la_ref[...] = _log_sigmoid(z) * (1.0 / GLA_GATE_NORM)


def _ab_in(x, mod6, w_in, gate_w, gate_b, seq):
    t = x.shape[0]
    tm = ROW_TILE
    per_b = seq // tm
    w = jnp.zeros((D_MODEL, AB_IN_PAD), BF16).at[:, :AB_IN].set(w_in.astype(BF16))
    gw = jnp.zeros((AB_IN_PAD - 2048, GLA_QK), BF16).at[:GLA_RANK].set(gate_w.astype(BF16))
    row = lambda n: pl.BlockSpec((tm, n), lambda i: (i, 0))
    full = lambda a: pl.BlockSpec(a.shape, lambda i: (0,) * a.ndim)
    outs = [S5_W, GLA_QK, GLA_QK, GLA_V, GLA_V, GLA_QK]
    return pl.pallas_call(
        _ab_in_body,
        grid=(t // tm,),
        in_specs=[row(D_MODEL),
                  pl.BlockSpec((1, 6, D_MODEL), lambda i: (i // per_b, 0, 0)),
                  full(w), full(gw), pl.BlockSpec((1, GLA_QK), lambda i: (0, 0))],
        out_specs=[row(n) for n in outs],
        out_shape=[jax.ShapeDtypeStruct((t, n), F32) for n in outs],
        compiler_params=_cparams("parallel"),
        name="ab_in",
    )(x, mod6, w, gw, gate_b.reshape(1, GLA_QK))


def _s5_prep(lam_re, lam_im, log_dt, b_re, b_im, c_re, c_im):
    hp = lax.Precision.HIGHEST
    n = S5_CHUNK
    lre, lim = lam_re.astype(F32), lam_im.astype(F32)
    dt = jnp.exp(log_dt.astype(F32))[:, None]
    d = jnp.arange(n + 1, dtype=F32)[:, None, None]
    mag = jnp.exp(lre * dt * d)
    ang = lim * dt * d
    pw_re, pw_im = mag * jnp.cos(ang), mag * jnp.sin(ang)
    den = lre * lre + lim * lim
    nre, nim = pw_re[1] - 1.0, pw_im[1]
    coef_re = (nre * lre + nim * lim) / den
    coef_im = (nim * lre - nre * lim) / den
    bre, bim = b_re.astype(F32), b_im.astype(F32)
    bb_re = coef_re[..., None] * bre - coef_im[..., None] * bim
    bb_im = coef_re[..., None] * bim + coef_im[..., None] * bre
    cre, cim = c_re.astype(F32), c_im.astype(F32)
    cp_re = cre[None] * pw_re[:, :, None, :] - cim[None] * pw_im[:, :, None, :]
    cp_im = cre[None] * pw_im[:, :, None, :] + cim[None] * pw_re[:, :, None, :]
    kern = (jnp.einsum('dgcp,gpe->dgce', cp_re, bb_re, precision=hp)
            - jnp.einsum('dgcp,gpe->dgce', cp_im, bb_im, precision=hp))
    lag_is = (jnp.arange(n)[None, None, :] - jnp.arange(n)[None, :, None]
              == jnp.arange(n + 1)[:, None, None]).astype(F32)
    m_intra = jnp.einsum('djt,dgce->gjetc', lag_is, kern, precision=hp)
    m_intra = m_intra.reshape(S5_GROUPS, n * S5_GROUP, n * S5_GROUP)
    pws_re, pws_im = pw_re[n - 1 - jnp.arange(n)], pw_im[n - 1 - jnp.arange(n)]
    ws_re = pws_re[..., None] * bb_re[None] - pws_im[..., None] * bb_im[None]
    ws_im = pws_re[..., None] * bb_im[None] + pws_im[..., None] * bb_re[None]
    w_state = jnp.transpose(jnp.concatenate([ws_re, ws_im], axis=2), (1, 0, 3, 2))
    w_state = w_state.reshape(S5_GROUPS, n * S5_GROUP, 2 * S5_STATE)
    wi_re = jnp.transpose(cp_re[1:], (1, 3, 0, 2))
    wi_im = -jnp.transpose(cp_im[1:], (1, 3, 0, 2))
    w_inter = jnp.concatenate([wi_re, wi_im], axis=1).reshape(S5_GROUPS, 2 * S5_STATE, n * S5_GROUP)
    r = (n * jnp.arange(S5_SCAN_ROWS + 1, dtype=F32))[None, :, None]
    cmag, cang = jnp.exp(lre[:, None, :] * dt[:, None, :] * r), lim[:, None, :] * dt[:, None, :] * r
    a_pows = jnp.concatenate([cmag * jnp.cos(cang), cmag * jnp.sin(cang)], axis=1)
    return m_intra.astype(BF16), w_state.astype(BF16), w_inter.astype(BF16), a_pows


def _s5_body(u_ref, m_ref, ws_ref, wi_ref, a_ref, y_ref, ere, eim, sre, sim, *, bsz):
    u = u_ref[0]
    e = _dot(u, ws_ref[0])
    ere[...] = e[:, :S5_STATE]
    eim[...] = e[:, S5_STATE:]
    n_pow = S5_SCAN_ROWS + 1
    pr, pi = a_ref[0, 0:n_pow, :], a_ref[0, n_pow:2 * n_pow, :]
    tab_r, tab_i = pr[0:S5_SCAN_ROWS], pi[0:S5_SCAN_ROWS]
    rowi = lax.broadcasted_iota(jnp.int32, (S5_SCAN_ROWS, S5_STATE), 0)
    nk = ere.shape[0] // bsz

    def cmul(a_r, a_i, x_r, x_i):
        return a_r * x_r - a_i * x_i, a_r * x_i + a_i * x_r

    def shifted(x, s):
        return jnp.where(rowi >= s, pltpu.roll(x, s, 0), 0.0)

    def tile(m, carry):
        out = []
        for b in range(bsz):
            c_r, c_i = carry[2 * b], carry[2 * b + 1]
            r0 = pl.multiple_of(b * nk + m * S5_SCAN_ROWS, S5_SCAN_ROWS)
            x_r, x_i = ere[pl.ds(r0, S5_SCAN_ROWS), :], eim[pl.ds(r0, S5_SCAN_ROWS), :]
            for s in (1, 2, 4):
                d_r, d_i = cmul(pr[s:s + 1], pi[s:s + 1], shifted(x_r, s), shifted(x_i, s))
                x_r, x_i = x_r + d_r, x_i + d_i
            t_r, t_i = cmul(tab_r, tab_i, c_r, c_i)
            sre[pl.ds(r0, S5_SCAN_ROWS), :] = shifted(x_r, 1) + t_r
            sim[pl.ds(r0, S5_SCAN_ROWS), :] = shifted(x_i, 1) + t_i
            n_r, n_i = cmul(pr[S5_SCAN_ROWS:n_pow], pi[S5_SCAN_ROWS:n_pow], c_r, c_i)
            last = slice(S5_SCAN_ROWS - 1, S5_SCAN_ROWS)
            out += [n_r + jnp.broadcast_to(x_r[last], c_r.shape), n_i + jnp.broadcast_to(x_i[last], c_i.shape)]
        return tuple(out)

    z = jnp.zeros((S5_SCAN_ROWS, S5_STATE), F32)
    lax.fori_loop(0, nk // S5_SCAN_ROWS, tile, (z,) * (2 * bsz))
    y = _dot(u, m_ref[0])
    y = y + _dot(sre[...].astype(BF16), wi_ref[0, :S5_STATE, :])
    y = y + _dot(sim[...].astype(BF16), wi_ref[0, S5_STATE:, :])
    y_ref[0] = y


def _s5_pack_body(u_ref, o_ref):
    nk = u_ref.shape[0] // S5_CHUNK
    by_step = [u_ref[pl.ds(t, nk, stride=S5_CHUNK), :] for t in range(S5_CHUNK)]
    for g in range(128 // S5_GROUP):
        cols = slice(S5_GROUP * g, S5_GROUP * (g + 1))
        o_ref[g] = jnp.concatenate([x[:, cols] for x in by_step], axis=1).astype(o_ref.dtype)


def _s5_unpack_body(y_ref, o_ref):
    nk = y_ref.shape[1]
    groups = [y_ref[g] for g in range(128 // S5_GROUP)]
    for t in range(S5_CHUNK):
        cols = slice(S5_GROUP * t, S5_GROUP * (t + 1))
        o_ref[pl.ds(t, nk, stride=S5_CHUNK), :] = jnp.concatenate([y[:, cols] for y in groups], axis=1)


def _s5_conv(u, tables, bsz, seq):
    m_intra, w_state, w_inter, a_pows = tables
    t = bsz * seq
    rows = t // S5_CHUNK
    lanes = S5_CHUNK * S5_GROUP
    tm = PERM_TILE
    gpb = 128 // S5_GROUP
    tok = pl.BlockSpec((tm, 128), lambda i, j: (i, j))
    chunked = pl.BlockSpec((gpb, tm // S5_CHUNK, lanes), lambda i, j: (j, i, 0))
    uf = pl.pallas_call(
        _s5_pack_body,
        grid=(t // tm, S5_W // 128),
        in_specs=[tok],
        out_specs=chunked,
        out_shape=jax.ShapeDtypeStruct((S5_GROUPS, rows, lanes), BF16),
        compiler_params=_cparams("parallel", "parallel"),
        name="s5_pack",
    )(u)
    grp = lambda shape: pl.BlockSpec((1,) + shape, lambda g: (g, 0, 0))
    yf = pl.pallas_call(
        functools.partial(_s5_body, bsz=bsz),
        grid=(S5_GROUPS,),
        in_specs=[grp((rows, lanes)), grp((lanes, lanes)), grp((lanes, 2 * S5_STATE)),
                  grp((2 * S5_STATE, lanes)), grp(a_pows.shape[1:])],
        out_specs=grp((rows, lanes)),
        out_shape=jax.ShapeDtypeStruct((S5_GROUPS, rows, lanes), F32),
        scratch_shapes=[pltpu.VMEM((rows, S5_STATE), F32)] * 4,
        compiler_params=_cparams("parallel"),
        name="s5_conv",
    )(uf, m_intra, w_state, w_inter, a_pows)
    return pl.pallas_call(
        _s5_unpack_body,
        grid=(t // tm, S5_W // 128),
        in_specs=[chunked],
        out_specs=tok,
        out_shape=jax.ShapeDtypeStruct((t, S5_W), F32),
        compiler_params=_cparams("parallel", "parallel"),
        name="s5_unpack",
    )(yf)


def _gla_body(q_ref, k_ref, g_ref, v_ref, r_ref, ng_ref, y_ref, s_ref, bc_ref, o_ref):
    tl = q_ref.shape[0]
    scale = GLA_DK ** -0.5

    @pl.when(pl.program_id(1) == 0)
    def _():
        s_ref[...] = jnp.zeros_like(s_ref)

    row = lax.broadcasted_iota(jnp.int32, (CHUNK, CHUNK), 0)
    col = lax.broadcasted_iota(jnp.int32, (CHUNK, CHUNK), 1)
    tri = col <= row
    tref = col < (row // GLA_SUB) * GLA_SUB
    cum_mat = jnp.concatenate([tri, tref], axis=0).astype(BF16)
    lane_head = lax.broadcasted_iota(jnp.int32, (GLA_SUB, GLA_QK), 1) // GLA_DK
    bd_mask = (lax.broadcasted_iota(jnp.int32, (GLA_QK, GLA_V), 0) // GLA_DK
               == lax.broadcasted_iota(jnp.int32, (GLA_QK, GLA_V), 1) // GLA_DV)
    ones_bd = bd_mask.astype(BF16)
    ones_v = jnp.ones((CHUNK, GLA_V), BF16)

    for c in range(tl // CHUNK):
        r0 = c * CHUNK
        qc = q_ref[pl.ds(r0, CHUNK), :] * scale
        kc = k_ref[pl.ds(r0, CHUNK), :]
        g_parts = _split3_bf16(g_ref[pl.ds(r0, CHUNK), :])
        vb = v_ref[pl.ds(r0, CHUNK), :].astype(BF16)
        br = sum(_dot(cum_mat, part) for part in g_parts)
        bc, ref = br[:CHUNK], br[CHUNK:]
        bc_ref[pl.ds(r0, CHUNK), :] = bc
        qt = qc * jnp.exp(bc - ref)
        state = s_ref[...]
        o = _dot((qt * jnp.exp(ref)).astype(BF16), state.astype(BF16))
        blocks = [o[:GLA_SUB]]
        for i in range(1, CHUNK // GLA_SUB):
            ri = bc[GLA_SUB * i - 1:GLA_SUB * i, :]
            kt = (kc * jnp.exp(jnp.minimum(ri - bc, 0.0))).astype(BF16)
            qi = qt[GLA_SUB * i:GLA_SUB * (i + 1)]
            lhs = jnp.concatenate([jnp.where(lane_head == h, qi, 0.0) for h in range(GLA_HEADS)], axis=0)
            att = _dot_nt(lhs.astype(BF16), kt)
            att = jnp.where(col < GLA_SUB * i, att, 0.0)
            ov = _dot(att.astype(BF16), vb)
            oi = jnp.concatenate([ov[GLA_SUB * h:GLA_SUB * (h + 1), GLA_DV * h:GLA_DV * (h + 1)]
                                  for h in range(GLA_HEADS)], axis=1)
            blocks.append(o[GLA_SUB * i:GLA_SUB * (i + 1)] + oi)
        o_ref[pl.ds(r0, CHUNK), :] = jnp.concatenate(blocks, axis=0)
        bl = bc[CHUNK - 1:CHUNK, :]
        kh = (kc * jnp.exp(bl - bc)).astype(BF16)
        upd = _dot_tn(kh, vb)
        dcol = sum(_dot_tn(part, ones_v) for part in g_parts)
        s_ref[...] = jnp.exp(dcol) * state + jnp.where(bd_mask, upd, 0.0)

    q = q_ref[...] * scale
    k = k_ref[...]
    vb = v_ref[...].astype(BF16)
    bc = bc_ref[...]
    rmod = lax.broadcasted_iota(jnp.int32, (tl, 1), 0) % GLA_SUB
    band = 128
    lag_of = (lax.broadcasted_iota(jnp.int32, (band, band), 0) - lax.broadcasted_iota(jnp.int32, (band, band), 1))
    att = [[jnp.zeros((band, band), F32) for _ in range(GLA_HEADS)] for _ in range(tl // band)]
    for d in range(GLA_SUB):
        kd = k if d == 0 else pltpu.roll(k, d, 0)
        bd = bc if d == 0 else pltpu.roll(bc, d, 0)
        valid = rmod >= d
        e = jnp.exp(jnp.where(valid, bc - bd, 0.0))
        p = jnp.where(valid, q * kd * e, 0.0).astype(BF16)
        score = _dot(p, ones_bd)
        hit = lag_of == d
        for b in range(tl // band):
            for h in range(GLA_HEADS):
                att[b][h] = jnp.where(hit, score[band * b:band * (b + 1), GLA_DV * h:GLA_DV * (h + 1)], att[b][h])
    od = jnp.concatenate(
        [jnp.concatenate([_dot(att[b][h].astype(BF16), vb[band * b:band * (b + 1), GLA_DV * h:GLA_DV * (h + 1)])
                          for h in range(GLA_HEADS)], axis=1) for b in range(tl // band)], axis=0)
    o = o_ref[...] + od
    ng = ng_ref[...]
    outs = []
    for h in range(GLA_HEADS):
        oh = o[:, GLA_DV * h:GLA_DV * (h + 1)]
        outs.append(oh * lax.rsqrt(jnp.mean(oh * oh, axis=-1, keepdims=True) + RMS_EPS))
    y_ref[...] = jnp.concatenate(outs, axis=1) * ng * _silu(r_ref[...])


def _gla(q, k, la, v, r, norm_g, bsz, seq):
    tl = SEQ_TILE
    nl = seq // tl
    blk = lambda n: pl.BlockSpec((tl, n), lambda b, l: (b * nl + l, 0))
    return pl.pallas_call(
        _gla_body,
        grid=(bsz, nl),
        in_specs=[blk(GLA_QK), blk(GLA_QK), blk(GLA_QK), blk(GLA_V), blk(GLA_V),
                  pl.BlockSpec((1, GLA_V), lambda b, l: (0, 0))],
        out_specs=blk(GLA_V),
        out_shape=jax.ShapeDtypeStruct((bsz * seq, GLA_V), F32),
        scratch_shapes=[pltpu.VMEM((GLA_QK, GLA_V), F32), pltpu.VMEM((tl, GLA_QK), F32),
                        pltpu.VMEM((tl, GLA_V), F32)],
        compiler_params=_cparams("parallel", "arbitrary"),
        name="gla",
    )(q, k, la, v, r, norm_g.reshape(1, GLA_V))


def _residual_ln(x, y, mod_ref, gate_row, lg_ref, lb_ref):
    return _layer_norm(ALPHA * x + (1.0 + mod_ref[0, gate_row:gate_row + 1, :]) * y, lg_ref[...], lb_ref[...])


def _ab_out_body(ys_ref, u_ref, yb_ref, x_ref, mod_ref, d_ref, gw_ref, gb_ref, wa_ref, wb_ref, lg_ref, lb_ref, o_ref):
    z = _gelu_tanh(ys_ref[...] + d_ref[...] * u_ref[...])
    ya = z * _sigmoid(_dot(z.astype(BF16), gw_ref[...]) + gb_ref[...])
    y = _dot(ya.astype(BF16), wa_ref[...]) + _dot(yb_ref[...].astype(BF16), wb_ref[...])
    o_ref[...] = _residual_ln(x_ref[...], y, mod_ref, 2, lg_ref, lb_ref)


def _ml_out_body(y_ref, x_ref, mod_ref, w_ref, lg_ref, lb_ref, o_ref):
    y = _dot(y_ref[...].astype(BF16), w_ref[...])
    o_ref[...] = _residual_ln(x_ref[...], y, mod_ref, 2, lg_ref, lb_ref)


def _out_call(body, name, row_inputs, x, mod6, consts, ln_g, ln_b, seq):
    t = x.shape[0]
    tm = ROW_TILE
    per_b = seq // tm
    row = lambda a: pl.BlockSpec((tm, a.shape[1]), lambda i: (i, 0))
    full = lambda a: pl.BlockSpec(a.shape, lambda i: (0,) * a.ndim)
    lg, lb = ln_g.reshape(1, D_MODEL), ln_b.reshape(1, D_MODEL)
    return pl.pallas_call(
        body,
        grid=(t // tm,),
        in_specs=([row(a) for a in row_inputs] + [row(x), pl.BlockSpec((1, 6, D_MODEL), lambda i: (i // per_b, 0, 0))]
                  + [full(a) for a in consts] + [full(lg), full(lb)]),
        out_specs=row(x),
        out_shape=jax.ShapeDtypeStruct((t, D_MODEL), F32),
        compiler_params=_cparams("parallel"),
        name=name,
    )(*row_inputs, x, mod6, *consts, lg, lb)


def _ml_in_body(x_ref, mod_ref, w_ref, wt_ref, gb_ref, gbt_ref, qk_ref, v_ref, o_ref, gt_ref, vt_ref, gtt_ref):
    x = x_ref[...]
    h = (x * (1.0 + mod_ref[0, 1:2, :]) + mod_ref[0, 0:1, :]).astype(BF16)
    qk_ref[...] = _dot(h, w_ref[:, 0:2 * ML_W])
    v_ref[...] = _dot(h, w_ref[:, 2 * ML_W:3 * ML_W])
    o_ref[...] = _dot(h, w_ref[:, 3 * ML_W:4 * ML_W])
    gt_ref[...] = _dot(h, w_ref[:, 4 * ML_W:ML_IN_PAD]) + gb_ref[...]
    vt_ref[...] = _dot_nt(wt_ref[0:ML_W, :], h)
    gtt_ref[...] = _dot_nt(wt_ref[ML_W:, :], h) + gbt_ref[...]


def _ml_in(x, mod6, w_in, igate_b, fgate_b, seq):
    t = x.shape[0]
    tm = ROW_TILE
    per_b = seq // tm
    n_gate = ML_IN_PAD - 4 * ML_W
    w = jnp.zeros((D_MODEL, ML_IN_PAD), BF16).at[:, :ML_IN].set(w_in.astype(BF16))
    wt = w[:, 2 * ML_W:3 * ML_W].T
    wt = jnp.concatenate([wt, w[:, 4 * ML_W:].T], axis=0)
    gb = jnp.zeros((1, n_gate), F32).at[0, :2 * ML_HEADS].set(jnp.concatenate([igate_b, fgate_b]))
    row = lambda n: pl.BlockSpec((tm, n), lambda i: (i, 0))
    colblk = lambda n: pl.BlockSpec((n, tm), lambda i: (0, i))
    full = lambda a: pl.BlockSpec(a.shape, lambda i: (0,) * a.ndim)
    outs = [2 * ML_W, ML_W, ML_W, n_gate]
    return pl.pallas_call(
        _ml_in_body,
        grid=(t // tm,),
        in_specs=[row(D_MODEL), pl.BlockSpec((1, 6, D_MODEL), lambda i: (i // per_b, 0, 0)), full(w), full(wt),
                  full(gb), pl.BlockSpec((n_gate, 1), lambda i: (0, 0))],
        out_specs=[row(n) for n in outs] + [colblk(ML_W), colblk(n_gate)],
        out_shape=([jax.ShapeDtypeStruct((t, n), F32) for n in outs]
                   + [jax.ShapeDtypeStruct((ML_W, t), F32), jax.ShapeDtypeStruct((n_gate, t), F32)]),
        compiler_params=_cparams("parallel"),
        name="ml_in",
    )(x, mod6, w, wt, gb, gb.reshape(n_gate, 1))


def _mlstm_body(q_ref, k_ref, v_ref, og_ref, gt_ref, vt_ref, gtt_ref, wq_ref, wk_ref, bq_ref, bk_ref, ng_ref,
                y_ref, cq_ref, ck_ref, qs_ref, ks_ref, s_ref, n_ref, m_ref):
    tl = q_ref.shape[0]

    @pl.when(pl.program_id(1) == 0)
    def _():
        for ref in (s_ref, n_ref, m_ref, cq_ref, ck_ref):
            ref[...] = jnp.zeros_like(ref)

    def conv(x_ref, carry_ref, w_ref, b_ref, out_ref, scale):
        for h in range(ML_HEADS):
            cols = slice(ML_DH * h, ML_DH * (h + 1))
            x = x_ref[:, cols]
            xin = jnp.concatenate([carry_ref[:, cols], x], axis=0)
            acc = jnp.zeros((tl, ML_DH), F32) + b_ref[:, cols]
            for i in range(CONV_W):
                sh = CONV_W - 1 - i
                xs = xin if sh == 0 else pltpu.roll(xin, sh, 0)
                acc = acc + w_ref[i:i + 1, cols] * xs[8:8 + tl]
            carry_ref[:, cols] = x[tl - 8:tl]
            out_ref[:, cols] = _silu(acc) * scale

    conv(q_ref, cq_ref, wq_ref, bq_ref, qs_ref, 1.0)
    conv(k_ref, ck_ref, wk_ref, bk_ref, ks_ref, ML_DH ** -0.5)

    cs = ML_CHUNK
    row = lax.broadcasted_iota(jnp.int32, (cs, cs), 0)
    col = lax.broadcasted_iota(jnp.int32, (cs, cs), 1)
    causal = col <= row
    tri = causal.astype(BF16)
    tri_t = (row <= col).astype(BF16)

    for c in range(tl // cs):
        rows = slice(cs * c, cs * (c + 1))
        gt = gt_ref[rows, :]
        gt_t = gtt_ref[:, rows]
        lf_c = _log_sigmoid(gt)
        lf_r = _log_sigmoid(gt_t[ML_HEADS:2 * ML_HEADS])
        b_c_all = sum(_dot(tri, part) for part in _split3_bf16(lf_c))
        b_r_all = sum(_dot(part, tri_t) for part in _split3_bf16(lf_r))
        per_head = lambda f: jnp.stack([f(h) for h in range(ML_HEADS)])
        head_cols = lambda h: slice(ML_DH * h, ML_DH * (h + 1))
        q3 = per_head(lambda h: qs_ref[rows, head_cols(h)])
        k3 = per_head(lambda h: ks_ref[rows, head_cols(h)])
        vb = per_head(lambda h: v_ref[rows, head_cols(h)]).astype(BF16)
        vtb = per_head(lambda h: vt_ref[head_cols(h), rows]).astype(BF16)
        i_c = per_head(lambda h: gt[:, h:h + 1])
        b_c = per_head(lambda h: b_c_all[:, ML_HEADS + h:ML_HEADS + h + 1])
        i_r = per_head(lambda h: gt_t[h:h + 1, :])
        b_r = per_head(lambda h: b_r_all[h:h + 1, :])
        m = m_ref[:, :, 0:1]
        dmat = jnp.where(causal[None], b_c - b_r + i_r, -jnp.inf)
        inter_log = b_c + m
        m_t = jnp.maximum(inter_log, jnp.max(dmat, axis=2, keepdims=True))
        w_intra = jnp.exp(dmat - m_t)
        w_inter = jnp.exp(inter_log - m_t)
        qb = q3.astype(BF16)
        s = jnp.einsum('htd,hsd->hts', qb, k3.astype(BF16), preferred_element_type=F32) * w_intra
        state, nrm = s_ref[...], n_ref[...]
        num = (w_inter * jnp.einsum('htd,hvd->htv', qb, state.astype(BF16), preferred_element_type=F32)
               + jnp.einsum('hts,hsv->htv', s.astype(BF16), vb, preferred_element_type=F32))
        den = w_inter * jnp.sum(q3 * nrm, axis=2, keepdims=True) + jnp.sum(s, axis=2, keepdims=True)
        hc = num / jnp.maximum(jnp.abs(den), jnp.exp(-m_t))
        hc = hc * lax.rsqrt(jnp.mean(hc * hc, axis=2, keepdims=True) + RMS_EPS)
        for h in range(ML_HEADS):
            cols = head_cols(h)
            y_ref[rows, cols] = _sigmoid(og_ref[rows, cols]) * (hc[h] * ng_ref[:, cols])
        b_last = b_c[:, cs - 1:cs, :]
        gs_c = b_last - b_c + i_c
        gs_r = b_last - b_r + i_r
        m_new = jnp.maximum(b_last + m, jnp.max(gs_r, axis=2, keepdims=True))
        dec = jnp.exp(b_last + m - m_new)
        kw = k3 * jnp.exp(gs_c - m_new)
        s_ref[...] = dec * state + jnp.einsum('hvt,htd->hvd', vtb, kw.astype(BF16), preferred_element_type=F32)
        n_ref[...] = dec * nrm + jnp.sum(kw, axis=1, keepdims=True)
        m_ref[...] = jnp.broadcast_to(m_new, m_ref.shape)


def _mlstm(qk, v, og, gates, v_t, gates_t, conv_w, conv_b, norm_g, bsz, seq):
    tl = SEQ_TILE
    nl = seq // tl
    blk = lambda n, off: pl.BlockSpec((tl, n), lambda b, l: (b * nl + l, off))
    blk_t = lambda n: pl.BlockSpec((n, tl), lambda b, l: (0, b * nl + l))
    par = lambda rows, off: pl.BlockSpec((rows, ML_W), lambda b, l: (0, off))
    state = [pltpu.VMEM((ML_HEADS, ML_DH, ML_DH), F32), pltpu.VMEM((ML_HEADS, 1, ML_DH), F32),
             pltpu.VMEM((ML_HEADS, 1, ML_DH), F32)]
    return pl.pallas_call(
        _mlstm_body,
        grid=(bsz, nl),
        in_specs=[blk(ML_W, 0), blk(ML_W, 1), blk(ML_W, 0), blk(ML_W, 0), blk(gates.shape[1], 0),
                  blk_t(ML_W), blk_t(gates_t.shape[0]),
                  par(CONV_W, 0), par(CONV_W, 1), par(1, 0), par(1, 1), par(1, 0)],
        out_specs=blk(ML_W, 0),
        out_shape=jax.ShapeDtypeStruct((bsz * seq, ML_W), F32),
        scratch_shapes=[pltpu.VMEM((8, ML_W), F32), pltpu.VMEM((8, ML_W), F32),
                        pltpu.VMEM((tl, ML_W), F32), pltpu.VMEM((tl, ML_W), F32)] + state,
        compiler_params=_cparams("parallel", "arbitrary"),
        name="mlstm",
    )(qk, qk, v, og, gates, v_t, gates_t, conv_w, conv_w, conv_b.reshape(1, -1), conv_b.reshape(1, -1),
      norm_g.reshape(1, ML_W))


def _router_body(x_ref, mod_ref, rw_ref, rb_ref, info_ref, cnt_ref, carry_ref):
    tr = x_ref.shape[0]

    @pl.when(pl.program_id(0) == 0)
    def _():
        carry_ref[...] = jnp.zeros_like(carry_ref)

    h = x_ref[...] * (1.0 + mod_ref[0, 4:5, :]) + mod_ref[0, 3:4, :]
    h_hi = h.astype(BF16)
    h_lo = (h - h_hi.astype(F32)).astype(BF16)
    w = rw_ref[...]
    w_hi = w.astype(BF16)
    w_lo = (w - w_hi.astype(F32)).astype(BF16)
    logits = _dot_nt(w_hi, h_hi) + (_dot_nt(w_lo, h_hi) + _dot_nt(w_hi, h_lo))
    aff = _sigmoid(logits)
    sel = aff + rb_ref[...]
    s = [sel[e:e + 1, :] for e in range(N_EXPERTS)]
    a = [aff[e:e + 1, :] for e in range(N_EXPERTS)]

    def top2_sum(v):
        best = v[0] + v[1]
        for lo_i, hi_i in zip(PAIR_LO[1:], PAIR_HI[1:]):
            best = jnp.maximum(best, v[lo_i] + v[hi_i])
        return best

    score = [top2_sum(s[EPG * g:EPG * (g + 1)]) for g in range(N_GROUPS)]
    g_idx = jnp.zeros((1, tr), jnp.int32)
    best = score[0]
    for g in range(1, N_GROUPS):
        upd = score[g] > best
        g_idx = jnp.where(upd, g, g_idx)
        best = jnp.where(upd, score[g], best)

    def pick_group(rows, j):
        out = rows[j]
        for g in range(1, N_GROUPS):
            out = jnp.where(g_idx == g, rows[EPG * g + j], out)
        return out

    sg = [pick_group(s, j) for j in range(EPG)]
    ag = [pick_group(a, j) for j in range(EPG)]
    first = jnp.zeros((1, tr), jnp.int32)
    best = sg[0]
    for j in range(1, EPG):
        upd = sg[j] > best
        first = jnp.where(upd, j, first)
        best = jnp.where(upd, sg[j], best)
    second = jnp.zeros((1, tr), jnp.int32)
    best2 = jnp.full((1, tr), -jnp.inf, F32)
    for j in range(EPG):
        upd = (first != j) & (sg[j] > best2)
        second = jnp.where(upd, j, second)
        best2 = jnp.where(upd, sg[j], best2)
    lo = jnp.minimum(first, second)
    hi = jnp.maximum(first, second)
    pair = jnp.where(lo == 0, hi - 1, jnp.where(lo == 1, hi + 1, N_PAIRS - 1))
    cls = g_idx * N_PAIRS + pair

    def pick_local(rows, idx):
        out = rows[0]
        for j in range(1, EPG):
            out = jnp.where(idx == j, rows[j], out)
        return out

    a_lo, a_hi = pick_local(ag, lo), pick_local(ag, hi)
    tot = a_lo + a_hi
    onehot = (lax.broadcasted_iota(jnp.int32, (32, tr), 0) == cls).astype(F32)
    before = (lax.broadcasted_iota(jnp.int32, (tr, tr), 0) < lax.broadcasted_iota(jnp.int32, (tr, tr), 1)).astype(BF16)
    cum = _dot(onehot.astype(BF16), before) + carry_ref[:, 0:1]
    rank = jnp.sum(onehot * cum, axis=0, keepdims=True)
    carry_ref[...] = carry_ref[...] + jnp.sum(onehot, axis=1, keepdims=True)
    cnt_ref[0] = carry_ref[...]
    zero = jnp.zeros((1, tr), F32)
    info_ref[...] = jnp.concatenate([cls.astype(F32), rank, a_lo / tot, a_hi / tot, zero, zero, zero, zero], axis=0)


def _router(x, mod6, router_w, router_bias, seq):
    t = x.shape[0]
    tr = ROUTE_TILE
    per_b = seq // tr
    return pl.pallas_call(
        _router_body,
        grid=(t // tr,),
        in_specs=[pl.BlockSpec((tr, D_MODEL), lambda i: (i, 0)),
                  pl.BlockSpec((1, 6, D_MODEL), lambda i: (i // per_b, 0, 0)),
                  pl.BlockSpec((N_EXPERTS, D_MODEL), lambda i: (0, 0)),
                  pl.BlockSpec((N_EXPERTS, 1), lambda i: (0, 0))],
        out_specs=[pl.BlockSpec((8, tr), lambda i: (0, i)), pl.BlockSpec((1, 32, 128), lambda i: (i, 0, 0))],
        out_shape=[jax.ShapeDtypeStruct((8, t), F32), jax.ShapeDtypeStruct((t // tr, 32, 128), F32)],
        scratch_shapes=[pltpu.VMEM((32, 128), F32)],
        compiler_params=_cparams("arbitrary"),
        name="router",
    )(x, mod6, router_w.T.astype(F32), router_bias.reshape(N_EXPERTS, 1).astype(F32))


def _from_row_tiles(ref):
    return jnp.concatenate([ref[:, j, :] for j in range(N_LANE_GROUPS)], axis=1)


def _store_row_tiles(ref, x):
    for j in range(N_LANE_GROUPS):
        ref[:, j, :] = x[:, 128 * j:128 * (j + 1)]


def _run_dmas(make_copy, length):
    off = jnp.int32(0)
    size = PERM_TILE
    while size >= 1:
        has = (length & size) != 0

        @pl.when(has)
        def _(off=off, size=size):
            make_copy(off, size).start()

        off = off + jnp.where(has, size, 0)
        size //= 2


def _dispatch_body(pos_ref, gstart_ref, lstart_ref, len_ref, x_ref, dst_in_ref, dst_ref, stage, ordered, sem):
    del dst_in_ref
    i = pl.program_id(0)
    tm = x_ref.shape[0]
    _store_row_tiles(stage, x_ref[...])

    def place(r, c):
        ordered[pos_ref[i * tm + r]] = stage[r]
        return c

    lax.fori_loop(0, tm, place, 0, unroll=8)
    for c in range(N_CLASSES):
        run = i * N_CLASSES + c
        g0, l0 = gstart_ref[run], lstart_ref[run]
        _run_dmas(lambda off, n: pltpu.make_async_copy(ordered.at[pl.ds(l0 + off, n)], dst_ref.at[pl.ds(g0 + off, n)],
                                                       sem.at[0]), len_ref[run])
    pltpu.make_async_copy(ordered, dst_ref.at[pl.ds(0, tm)], sem.at[0]).wait()


def _dispatch(pos, gstart, lstart, length, x, dst0):
    t = x.shape[0]
    tm = PERM_TILE
    return pl.pallas_call(
        _dispatch_body,
        grid_spec=pltpu.PrefetchScalarGridSpec(
            num_scalar_prefetch=4,
            grid=(t // tm,),
            in_specs=[pl.BlockSpec((tm, D_MODEL), lambda i, *s: (i, 0)), pl.BlockSpec(memory_space=pl.ANY)],
            out_specs=pl.BlockSpec(memory_space=pl.ANY),
            scratch_shapes=[pltpu.VMEM((tm, N_LANE_GROUPS, 128), F32)] * 2 + [pltpu.SemaphoreType.DMA((1,))]),
        out_shape=jax.ShapeDtypeStruct(dst0.shape, F32),
        input_output_aliases={5: 0},
        compiler_params=_cparams("arbitrary"),
        name="moe_dispatch",
    )(pos, gstart, lstart, length, x, dst0)


def _combine_body(pos_ref, gstart_ref, lstart_ref, len_ref, src_ref, o_ref, ordered, stage, sem):
    i = pl.program_id(0)
    tm = o_ref.shape[0]
    for c in range(N_CLASSES):
        run = i * N_CLASSES + c
        g0, l0 = gstart_ref[run], lstart_ref[run]
        _run_dmas(lambda off, n: pltpu.make_async_copy(src_ref.at[pl.ds(g0 + off, n)], ordered.at[pl.ds(l0 + off, n)],
                                                       sem.at[0]), len_ref[run])
    pltpu.make_async_copy(src_ref.at[pl.ds(0, tm)], ordered, sem.at[0]).wait()

    def take(r, c):
        stage[r] = ordered[pos_ref[i * tm + r]]
        return c

    lax.fori_loop(0, tm, take, 0, unroll=8)
    o_ref[...] = _from_row_tiles(stage)


def _combine(pos, gstart, lstart, length, src):
    t = pos.shape[0]
    tm = PERM_TILE
    return pl.pallas_call(
        _combine_body,
        grid_spec=pltpu.PrefetchScalarGridSpec(
            num_scalar_prefetch=4,
            grid=(t // tm,),
            in_specs=[pl.BlockSpec(memory_space=pl.ANY)],
            out_specs=pl.BlockSpec((tm, D_MODEL), lambda i, *s: (i, 0)),
            scratch_shapes=[pltpu.VMEM((tm, N_LANE_GROUPS, 128), F32)] * 2 + [pltpu.SemaphoreType.DMA((1,))]),
        out_shape=jax.ShapeDtypeStruct((t, D_MODEL), F32),
        compiler_params=_cparams("arbitrary"),
        name="moe_combine",
    )(pos, gstart, lstart, length, src)


def _moe_body(e_lo_ref, e_hi_ref, blk_ref, new_ref, used_ref, x_ref, meta_ref, mod_ref,
              wg_lo32, wu_lo32, wd_lo32, wg_hi32, wu_hi32, wd_hi32, lg_ref, lb_ref, o_ref,
              wg_lo, wu_lo, wd_lo, wg_hi, wu_hi, wd_hi):
    del e_lo_ref, e_hi_ref, blk_ref
    i = pl.program_id(0)
    used = used_ref[0]

    @pl.when(new_ref[i] == 1)
    def _():
        for src, dst in ((wg_lo32, wg_lo), (wu_lo32, wu_lo), (wd_lo32, wd_lo),
                         (wg_hi32, wg_hi), (wu_hi32, wu_hi), (wd_hi32, wd_hi)):
            dst[...] = src[...].astype(BF16)

    @pl.when(i >= used)
    def _():
        o_ref[...] = jnp.zeros_like(o_ref)

    @pl.when(i < used)
    def _():
        x = _from_row_tiles(x_ref)
        meta = meta_ref[...]
        bid, g_lo, g_hi = meta[:, 0:1], meta[:, 1:2], meta[:, 2:3]
        nb = mod_ref.shape[0]

        def per_row(r):
            out = mod_ref[0, r:r + 1, :]
            for b in range(1, nb):
                out = jnp.where(bid == float(b), mod_ref[b, r:r + 1, :], out)
            return out

        h = (x * (1.0 + per_row(4)) + per_row(3)).astype(BF16)

        def expert(wg, wu, wd):
            he = _silu(_dot(h, wg[...])) * _dot(h, wu[...])
            return _dot(he.astype(BF16), wd[...])

        y = g_lo * expert(wg_lo, wu_lo, wd_lo) + g_hi * expert(wg_hi, wu_hi, wd_hi)
        _store_row_tiles(o_ref, _layer_norm(ALPHA * x + (1.0 + per_row(5)) * y, lg_ref[...], lb_ref[...]))


def _moe_tiles(e_lo, e_hi, blk, new, used, xs, meta, mod6, layer, w_gate, w_up, w_down, ln_g, ln_b):
    tp = xs.shape[0]
    tm = ROW_TILE
    tiles = lambda index: pl.BlockSpec((tm, N_LANE_GROUPS, 128), index)
    wspec = lambda shape, which: pl.BlockSpec((None, None) + shape, lambda i, *s: (layer, s[which][i], 0, 0))
    full = lambda a: pl.BlockSpec(a.shape, lambda i, *s: (0,) * a.ndim)
    lg, lb = ln_g.reshape(1, D_MODEL), ln_b.reshape(1, D_MODEL)
    gu, dn = (D_MODEL, D_EXPERT), (D_EXPERT, D_MODEL)
    return pl.pallas_call(
        _moe_body,
        grid_spec=pltpu.PrefetchScalarGridSpec(
            num_scalar_prefetch=5,
            grid=(tp // tm,),
            in_specs=[tiles(lambda i, *s: (s[2][i], 0, 0)), pl.BlockSpec((tm, 8), lambda i, *s: (s[2][i], 0)), full(mod6),
                      wspec(gu, 0), wspec(gu, 0), wspec(dn, 0), wspec(gu, 1), wspec(gu, 1), wspec(dn, 1),
                      full(lg), full(lb)],
            out_specs=tiles(lambda i, *s: (i, 0, 0)),
            scratch_shapes=[pltpu.VMEM(gu, BF16), pltpu.VMEM(gu, BF16), pltpu.VMEM(dn, BF16)] * 2),
        out_shape=jax.ShapeDtypeStruct((tp, N_LANE_GROUPS, 128), F32),
        compiler_params=_cparams("arbitrary"),
        name="moe_tiles",
    )(e_lo, e_hi, blk, new, used, xs, meta, mod6, w_gate, w_up, w_down, w_gate, w_up, w_down, lg, lb)


def _moe_slots(t):
    n_tiles = t // ROW_TILE + N_CLASSES
    return n_tiles, n_tiles * ROW_TILE


def _moe_layer(x, slots, mod6, router_w, router_bias, layer, w_gate, w_up, w_down, ln_g, ln_b, bsz, seq):
    t = x.shape[0]
    tm = ROW_TILE
    n_tiles, tp = _moe_slots(t)
    info, seen = _router(x, mod6, router_w, router_bias, seq)
    cls = info[0].astype(jnp.int32)
    rank = info[1].astype(jnp.int32)
    seen = seen[:, :N_CLASSES, 0].astype(jnp.int32)
    cnt = seen[-1]
    tiles_c = (cnt + tm - 1) // tm
    tile_end = jnp.cumsum(tiles_c)
    offs = (tile_end - tiles_c) * tm
    onehot = cls[:, None] == jnp.arange(N_CLASSES)[None, :]
    dest = rank + jnp.sum(jnp.where(onehot, offs[None, :], 0), axis=1)
    used = tile_end[-1]
    tile_id = jnp.arange(n_tiles)
    blk = jnp.maximum(jnp.minimum(tile_id, used - 1), 0)
    tile_cls = jnp.minimum(jnp.sum(blk[:, None] >= tile_end[None, :], axis=1), N_CLASSES - 1)
    grp, pair = tile_cls // N_PAIRS, tile_cls % N_PAIRS
    e_lo = (grp * EPG + jnp.asarray(PAIR_LO, jnp.int32)[pair]).astype(jnp.int32)
    e_hi = (grp * EPG + jnp.asarray(PAIR_HI, jnp.int32)[pair]).astype(jnp.int32)
    new = jnp.concatenate([jnp.ones((1,), jnp.int32), (tile_cls[1:] != tile_cls[:-1]).astype(jnp.int32)])
    bid = (jnp.arange(t) // seq).astype(F32)
    meta_nat = jnp.stack([bid, info[2], info[3]] + [jnp.zeros((t,), F32)] * 5, axis=1)
    meta = jnp.zeros((tp, 8), F32).at[dest].set(meta_nat)
    pt = PERM_TILE
    before = jnp.concatenate([jnp.zeros((1, N_CLASSES), jnp.int32), seen[:-1]], axis=0)
    length = seen - before
    lstart = jnp.cumsum(length, axis=1) - length
    gstart = offs[None, :] + before
    shift = jnp.broadcast_to((lstart - before)[:, None, :], (t // pt, pt, N_CLASSES)).reshape(t, N_CLASSES)
    pos = rank + jnp.sum(jnp.where(onehot, shift, 0), axis=1)
    runs = (pos.astype(jnp.int32), gstart.reshape(-1).astype(jnp.int32), lstart.reshape(-1).astype(jnp.int32),
            length.reshape(-1).astype(jnp.int32))
    xs = _dispatch(*runs, x, slots)
    ys = _moe_tiles(e_lo, e_hi, blk.astype(jnp.int32), new, used.reshape(1).astype(jnp.int32), xs, meta,
                    mod6, layer, w_gate, w_up, w_down, ln_g, ln_b)
    return _combine(*runs, ys), xs


def kernel(x, c, router_w, router_bias, ada_w, ada_b, ln1_g, ln1_b, ln2_g, ln2_b, moe_w_gate, moe_w_up, moe_w_down,
           ab_w_in, s5_lam_re, s5_lam_im, s5_log_dt, s5_b_re, s5_b_im, s5_c_re, s5_c_im, s5_d, s5_glu_w, s5_glu_b,
           gla_gate_w, gla_gate_b, gla_norm_g, ab_w_out, ml_w_in, ml_conv_w, ml_conv_b, ml_igate_b, ml_fgate_b,
           ml_norm_g, ml_w_out):
    bsz, seq, d = x.shape
    t = bsz * seq
    mod = _ada_mod(c, ada_w, ada_b)
    xt = x.reshape(t, d)
    slots = jnp.zeros((_moe_slots(t)[1], N_LANE_GROUPS, 128), F32)
    for layer in range(DEPTH):
        j = layer // 2
        mod6 = mod[layer]
        if layer % 2 == 0:
            u, q, k, v, r, la = _ab_in(xt, mod6, ab_w_in[j], gla_gate_w[j], gla_gate_b[j], seq)
            tables = _s5_prep(s5_lam_re[j], s5_lam_im[j], s5_log_dt[j], s5_b_re[j], s5_b_im[j], s5_c_re[j], s5_c_im[j])
            ys = _s5_conv(u, tables, bsz, seq)
            yb = _gla(q, k, la, v, r, gla_norm_g[j], bsz, seq)
            w_out = ab_w_out[j].astype(BF16)
            consts = [s5_d[j].reshape(1, S5_W), s5_glu_w[j].astype(BF16), s5_glu_b[j].reshape(1, S5_W),
                      w_out[:S5_W], w_out[S5_W:]]
            xt = _out_call(_ab_out_body, "ab_out", [ys, u, yb], xt, mod6, consts, ln1_g[layer], ln1_b[layer], seq)
        else:
            qk, v, og, gates, v_t, gates_t = _ml_in(xt, mod6, ml_w_in[j], ml_igate_b[j], ml_fgate_b[j], seq)
            y = _mlstm(qk, v, og, gates, v_t, gates_t, ml_conv_w[j], ml_conv_b[j], ml_norm_g[j], bsz, seq)
            xt = _out_call(_ml_out_body, "ml_out", [y], xt, mod6, [ml_w_out[j].astype(BF16)],
                           ln1_g[layer], ln1_b[layer], seq)
        xt, slots = _moe_layer(xt, slots, mod6, router_w, router_bias, layer, moe_w_gate, moe_w_up, moe_w_down,
                               ln2_g[layer], ln2_b[layer], bsz, seq)
    return xt.reshape(bsz, seq, d)
```
